```python
import jax, jax.numpy as jnp
from jax import lax
import numpy as np

D_MODEL = 4096
BATCH = 4
SEQ = 2048
DEPTH = 1
DEC_BATCH = 128
DEC_SEQ = 8
PAST_LEN = 16384
PAGE_SIZE = 128

HEAD_SIZE = 64
D_RWKV = D_MODEL // 2
D_CONV = D_MODEL - D_RWKV
N_HEADS = D_RWKV // HEAD_SIZE
DECAY_LORA = 96
AAA_LORA = 96
GATE_LORA = 256
RWKV_COLS = 3 * D_RWKV + DECAY_LORA + AAA_LORA + GATE_LORA
IN_COLS = RWKV_COLS + 2 * D_CONV
CONV_WIDTH = 31
CONV_BUF = CONV_WIDTH - 1
D_FF = ((8 * D_MODEL // 3 + 255) // 256) * 256
PLE_DIM = 256
RMS_EPS = 1e-6
LN_EPS = 1e-5
GN_EPS = 64e-5

kernel_name = 'rwkv7_conformer_conv_parallel_heads_decoder'


def rmsnorm(x, g):
    x32 = x.astype(jnp.float32)
    y = x32 * lax.rsqrt(jnp.mean(x32 * x32, axis=-1, keepdims=True) + RMS_EPS)
    return (y * g.astype(jnp.float32)).astype(x.dtype)


def layernorm_f32(x, w, b, eps):
    x32 = x.astype(jnp.float32)
    mu = jnp.mean(x32, axis=-1, keepdims=True)
    var = jnp.mean(jnp.square(x32 - mu), axis=-1, keepdims=True)
    return (x32 - mu) * lax.rsqrt(var + eps) * w.astype(jnp.float32) + b.astype(jnp.float32)


def wkv_recurrence(S0, r, w, k, v, kk, a):
    def step(S, inp):
        r_t, w_t, k_t, v_t, kk_t, a_t = inp
        s_kk = jnp.einsum('bhij,bhj->bhi', S, kk_t)
        S = (S * w_t[:, :, None, :]
             - s_kk[..., :, None] * (kk_t * a_t)[..., None, :]
             + v_t[..., :, None] * k_t[..., None, :])
        y_t = jnp.einsum('bhij,bhj->bhi', S, r_t)
        return S, y_t
    xs = tuple(jnp.swapaxes(t, 0, 1) for t in (r, w, k, v, kk, a))
    S, ys = lax.scan(step, S0, xs)
    return S, jnp.swapaxes(ys, 0, 1)


def decoder_layer(h, pe, wkv0, shift0, conv0,
                  g_mix, w_in, mu_shift, w0, w2, a0, a2, g2, k_k, k_a, r_k, lnx_w, lnx_b,
                  conv_w, conv_b, conv_ln_w, conv_ln_b, w_out,
                  g_ffn, w_gate_up, w_down, g_ple, w_ple_gate, w_ple_proj):
    f32 = jnp.float32
    Bn, T, _ = h.shape
    xn = rmsnorm(h, g_mix)
    z = jnp.einsum('btd,dc->btc', xn, w_in)
    zr = z[..., :RWKV_COLS]
    zc = z[..., RWKV_COLS:]
    z_first = jnp.einsum('bd,dc->bc', shift0.astype(xn.dtype), w_in[:, :RWKV_COLS])
    z_prev = jnp.concatenate([z_first[:, None, :], zr[:, :-1, :]], axis=1)
    zm = zr + (z_prev - zr) * mu_shift
    cuts = [D_RWKV, 2 * D_RWKV, 3 * D_RWKV, 3 * D_RWKV + DECAY_LORA, 3 * D_RWKV + DECAY_LORA + AAA_LORA]
    r, k, v, zw, za, zg = jnp.split(zm, cuts, axis=-1)
    w_log = -jax.nn.softplus(-(w0 + jnp.tanh(zw) @ w2).astype(f32)) - 0.5
    decay = jnp.exp(-jnp.exp(w_log))
    a = jax.nn.sigmoid((a0 + za @ a2).astype(f32))
    g = jax.nn.sigmoid(zg) @ g2
    heads = lambda t: t.astype(f32).reshape(Bn, T, N_HEADS, HEAD_SIZE)
    r, k, v, decay, a = heads(r), heads(k), heads(v), heads(decay), heads(a)
    k_k_h = k_k.astype(f32).reshape(N_HEADS, HEAD_SIZE)
    k_a_h = k_a.astype(f32).reshape(N_HEADS, HEAD_SIZE)
    kk = k * k_k_h
    kk = kk / jnp.maximum(jnp.sqrt(jnp.sum(kk * kk, axis=-1, keepdims=True)), 1e-12)
    k = k * (1.0 + (a - 1.0) * k_a_h)
    S, y = wkv_recurrence(wkv0.astype(f32), r, decay, k, v, kk, a)
    ym = jnp.mean(y, axis=-1, keepdims=True)
    yv = jnp.mean(jnp.square(y - ym), axis=-1, keepdims=True)
    y = ((y - ym) * lax.rsqrt(yv + GN_EPS)).reshape(Bn, T, D_RWKV) * lnx_w.astype(f32) + lnx_b.astype(f32)
    bonus = jnp.sum(r * k * r_k.astype(f32), axis=-1, keepdims=True) * v
    y_rwkv = ((y + bonus.reshape(Bn, T, D_RWKV)) * g.astype(f32)).astype(h.dtype)
    u = zc[..., :D_CONV] * jax.nn.sigmoid(zc[..., D_CONV:])
    full = jnp.concatenate([conv0.astype(u.dtype), u], axis=1)
    c = lax.conv_general_dilated(full, conv_w[:, None, :].astype(full.dtype), (1,), 'VALID',
                                 dimension_numbers=('NWC', 'WIO', 'NWC'),
                                 feature_group_count=D_CONV) + conv_b
    y_conv = jax.nn.silu(layernorm_f32(c, conv_ln_w, conv_ln_b, LN_EPS)).astype(h.dtype)
    h = h + jnp.einsum('btc,cd->btd', jnp.concatenate([y_rwkv, y_conv], axis=-1), w_out)
    gu = jnp.einsum('btd,df->btf', rmsnorm(h, g_ffn), w_gate_up)
    gate, up = jnp.split(gu, 2, axis=-1)
    h = h + jnp.einsum('btf,fd->btd', jax.nn.silu(gate) * up, w_down)
    pg = jax.nn.sigmoid(jnp.einsum('btd,de->bte', rmsnorm(h, g_ple), w_ple_gate))
    h = h + pg * jnp.einsum('btp,pd->btd', pe.astype(h.dtype), w_ple_proj)
    new_wkv = S.astype(h.dtype)
    new_shift = xn[:, -1, :]
    new_conv = full[:, -CONV_BUF:, :]
    return h, new_wkv, new_shift, new_conv


def setup_inputs(seed: int = 0) -> dict:
    key = jax.random.key(seed)
    ks = jax.random.split(key, 40)
    f32 = jnp.float32
    nrm = lambda kk, shape, s: s * jax.random.normal(kk, shape, f32)
    L = DEPTH
    return {
        'x_prompt': nrm(ks[0], (BATCH, SEQ, D_MODEL), 1.0),
        'x_sample': nrm(ks[1], (DEC_BATCH, DEC_SEQ, D_MODEL), 1.0),
        'state_wkv': nrm(ks[2], (L, DEC_BATCH, N_HEADS, HEAD_SIZE, HEAD_SIZE), 0.3),
        'state_shift': nrm(ks[3], (L, DEC_BATCH, D_MODEL), 1.0),
        'state_conv': nrm(ks[4], (L, DEC_BATCH, CONV_BUF, D_CONV), 0.5),
        'p_prompt': nrm(ks[5], (L, BATCH, SEQ, PLE_DIM), 1.0),
        'p_sample': nrm(ks[6], (L, DEC_BATCH, DEC_SEQ, PLE_DIM), 1.0),
        'g_mix': 1.0 + nrm(ks[7], (L, D_MODEL), 0.05),
        'w_in': nrm(ks[8], (L, D_MODEL, IN_COLS), D_MODEL ** -0.5),
        'mu_shift': jax.random.uniform(ks[9], (L, RWKV_COLS), f32),
        'w0': jax.random.uniform(ks[10], (L, D_RWKV), f32, -6.0, 1.0),
        'w2': nrm(ks[11], (L, DECAY_LORA, D_RWKV), 0.5 * DECAY_LORA ** -0.5),
        'a0': nrm(ks[12], (L, D_RWKV), 0.5),
        'a2': nrm(ks[13], (L, AAA_LORA, D_RWKV), 0.5 * AAA_LORA ** -0.5),
        'g2': nrm(ks[14], (L, GATE_LORA, D_RWKV), GATE_LORA ** -0.5),
        'k_k': 0.85 + nrm(ks[15], (L, D_RWKV), 0.05),
        'k_a': 1.0 + nrm(ks[16], (L, D_RWKV), 0.05),
        'r_k': nrm(ks[17], (L, N_HEADS, HEAD_SIZE), 0.1),
        'lnx_w': 1.0 + nrm(ks[18], (L, D_RWKV), 0.05),
        'lnx_b': nrm(ks[19], (L, D_RWKV), 0.01),
        'conv_w': nrm(ks[20], (L, CONV_WIDTH, D_CONV), CONV_WIDTH ** -0.5),
        'conv_b': nrm(ks[21], (L, D_CONV), 0.01),
        'conv_ln_w': 1.0 + nrm(ks[22], (L, D_CONV), 0.05),
        'conv_ln_b': nrm(ks[23], (L, D_CONV), 0.01),
        'w_out': nrm(ks[24], (L, D_MODEL, D_MODEL), D_MODEL ** -0.5),
        'g_ffn': 1.0 + nrm(ks[25], (L, D_MODEL), 0.05),
        'w_gate_up': nrm(ks[26], (L, D_MODEL, 2 * D_FF), D_MODEL ** -0.5),
        'w_down': nrm(ks[27], (L, D_FF, D_MODEL), D_FF ** -0.5),
        'g_ple': 1.0 + nrm(ks[28], (L, D_MODEL), 0.05),
        'w_ple_gate': nrm(ks[29], (L, D_MODEL, D_MODEL), D_MODEL ** -0.5),
        'w_ple_proj': nrm(ks[30], (L, PLE_DIM, D_MODEL), 0.5 * PLE_DIM ** -0.5),
        'g_final': 1.0 + nrm(ks[31], (D_MODEL,), 0.05),
    }


def reference(x_prompt, x_sample, state_wkv, state_shift, state_conv, p_prompt, p_sample,
              g_mix, w_in, mu_shift, w0, w2, a0, a2, g2, k_k, k_a, r_k, lnx_w, lnx_b,
              conv_w, conv_b, conv_ln_w, conv_ln_b, w_out, g_ffn, w_gate_up, w_down,
              g_ple, w_ple_gate, w_ple_proj, g_final):
    params = (g_mix, w_in, mu_shift, w0, w2, a0, a2, g2, k_k, k_a, r_k, lnx_w, lnx_b,
              conv_w, conv_b, conv_ln_w, conv_ln_b, w_out, g_ffn, w_gate_up, w_down,
              g_ple, w_ple_gate, w_ple_proj)
    dt = x_prompt.dtype
    hp, hs = x_prompt, x_sample
    wkv_p, shift_p, conv_p, wkv_s, shift_s, conv_s = [], [], [], [], [], []
    for i in range(DEPTH):
        lp = [t[i] for t in params]
        hp, s_w, s_sh, s_c = decoder_layer(
            hp, p_prompt[i],
            jnp.zeros((BATCH, N_HEADS, HEAD_SIZE, HEAD_SIZE), dt),
            jnp.zeros((BATCH, D_MODEL), dt),
            jnp.zeros((BATCH, CONV_BUF, D_CONV), dt), *lp)
        wkv_p.append(s_w); shift_p.append(s_sh); conv_p.append(s_c)
        hs, s_w, s_sh, s_c = decoder_layer(
            hs, p_sample[i], state_wkv[i], state_shift[i], state_conv[i], *lp)
        wkv_s.append(s_w); shift_s.append(s_sh); conv_s.append(s_c)
    y_prompt = rmsnorm(hp, g_final)
    y_sample = rmsnorm(hs, g_final)
    return (y_prompt, y_sample,
            jnp.stack(wkv_p), jnp.stack(shift_p), jnp.stack(conv_p),
            jnp.stack(wkv_s), jnp.stack(shift_s), jnp.stack(conv_s))
```

```python
import functools

import jax
import jax.numpy as jnp
from jax import lax
from jax.experimental import pallas as pl
from jax.experimental.pallas import tpu as pltpu

HEAD_SIZE = 64
RMS_EPS = 1e-6
LN_EPS = 1e-5
GN_EPS = 64e-5
KK_EPS = 1e-12
LANES = 128
VMEM_LIMIT_BYTES = 56 * 2 ** 20

F32 = jnp.float32
BF16 = jnp.bfloat16


def _cparams(*sem):
    return pltpu.CompilerParams(dimension_semantics=sem, vmem_limit_bytes=VMEM_LIMIT_BYTES)


def _round_up(n, m):
    return (n + m - 1) // m * m


def _pick_tile(n, pref):
    if n <= pref:
        return n
    t = pref - pref % LANES
    while t > LANES and n % t:
        t -= LANES
    assert n % t == 0, (n, pref)
    return t


def _mm_body(*refs, n_x, n_w, norm, epi):
    refs = list(refs)
    x_refs = [refs.pop(0) for _ in range(n_x)]
    g_ref = refs.pop(0) if norm else None
    w_refs = [refs.pop(0) for _ in range(n_w)]
    res_ref = refs.pop(0) if epi in ("res", "ple") else None
    out_ref = refs.pop(0)
    xn_ref = refs.pop(0) if norm else None

    if norm:
        @pl.when(pl.program_id(1) == 0)
        def _():
            x = x_refs[0][...].astype(F32)
            ms = jnp.mean(x * x, axis=-1, keepdims=True)
            xn_ref[...] = (x * lax.rsqrt(ms + RMS_EPS) * g_ref[...]).astype(BF16)
        lhs0 = xn_ref[...]
    else:
        lhs0 = x_refs[0][...]

    def dot(a, w_ref):
        return jnp.dot(a, w_ref[...], preferred_element_type=F32)

    if epi in ("store", "res"):
        acc = dot(lhs0, w_refs[0])
        for xr, wr in zip(x_refs[1:], w_refs[1:]):
            acc = acc + dot(xr[...], wr)
        if epi == "res":
            acc = res_ref[...] + acc
        out_ref[...] = acc.astype(out_ref.dtype)
    elif epi == "glu":
        out_ref[...] = (dot(lhs0, w_refs[0]) * jax.nn.sigmoid(dot(lhs0, w_refs[1]))).astype(out_ref.dtype)
    elif epi == "swiglu":
        gate = dot(lhs0, w_refs[0])
        out_ref[...] = (gate * jax.nn.sigmoid(gate) * dot(lhs0, w_refs[1])).astype(out_ref.dtype)
    elif epi == "ple":
        pg = jax.nn.sigmoid(dot(lhs0, w_refs[0]))
        proj = dot(x_refs[1][...].astype(BF16), w_refs[1])
        out_ref[...] = (res_ref[...] + pg * proj).astype(out_ref.dtype)
    else:
        raise ValueError(epi)


def _matmul(xs, ws, *, gain=None, res=None, epi="store", out_dtype=F32, n_out=None,
            w_col_blocks=None, tm=512, tn=512, name="mm"):
    m = xs[0].shape[0]
    n_out = ws[0].shape[1] if n_out is None else n_out
    tm = _pick_tile(m, tm)
    tn = _pick_tile(n_out, tn)
    norm = gain is not None
    w_col_blocks = [0] * len(ws) if w_col_blocks is None else w_col_blocks
    in_specs, args = [], []
    for x in xs:
        in_specs.append(pl.BlockSpec((tm, x.shape[1]), lambda i, j: (i, 0)))
        args.append(x)
    if norm:
        in_specs.append(pl.BlockSpec((1, gain.shape[-1]), lambda i, j: (0, 0)))
        args.append(gain.reshape(1, -1).astype(F32))
    for w, off in zip(ws, w_col_blocks):
        in_specs.append(pl.BlockSpec((w.shape[0], tn), functools.partial(lambda i, j, off: (0, j + off), off=off)))
        args.append(w)
    if res is not None:
        in_specs.append(pl.BlockSpec((tm, tn), lambda i, j: (i, j)))
        args.append(res)
    scratch = [pltpu.VMEM((tm, xs[0].shape[1]), BF16)] if norm else []
    return pl.pallas_call(
        functools.partial(_mm_body, n_x=len(xs), n_w=len(ws), norm=norm, epi=epi),
        grid=(m // tm, n_out // tn),
        in_specs=in_specs,
        out_specs=pl.BlockSpec((tm, tn), lambda i, j: (i, j)),
        out_shape=jax.ShapeDtypeStruct((m, n_out), out_dtype),
        scratch_shapes=scratch,
        compiler_params=_cparams("parallel", "arbitrary"),
        name=name,
    )(*args)


def _rms_body(x_ref, g_ref, o_ref):
    x = x_ref[...].astype(F32)
    ms = jnp.mean(x * x, axis=-1, keepdims=True)
    o_ref[...] = (x * lax.rsqrt(ms + RMS_EPS) * g_ref[...]).astype(o_ref.dtype)


def _rmsnorm(x, gain, *, tm=512, name="rmsnorm"):
    m, d = x.shape
    tm = _pick_tile(m, tm)
    return pl.pallas_call(
        _rms_body,
        grid=(m // tm,),
        in_specs=[pl.BlockSpec((tm, d), lambda i: (i, 0)), pl.BlockSpec((1, d), lambda i: (0, 0))],
        out_specs=pl.BlockSpec((tm, d), lambda i: (i, 0)),
        out_shape=jax.ShapeDtypeStruct((m, d), x.dtype),
        compiler_params=_cparams("parallel"),
        name=name,
    )(x, gain.reshape(1, d).astype(F32))


def _split3(x):
    hi = x.astype(BF16)
    r1 = x - hi.astype(F32)
    mid = r1.astype(BF16)
    lo = (r1 - mid.astype(F32)).astype(BF16)
    return hi, mid, lo


def _dot_f32(a, b_hi, b_lo):
    a_hi, a_mid, _ = _split3(a)
    dot = lambda p, q: jnp.dot(p, q, preferred_element_type=F32)
    return dot(a_hi, b_hi) + (dot(a_mid, b_hi) + dot(a_hi, b_lo))


def _prep_body(z_ref, first_ref, mu_ref, w0_ref, w2h_ref, w2l_ref, a0_ref, a2h_ref, a2l_ref,
               r_ref, k_ref, v_ref, d_ref, a_ref, *, dr, lw, la):
    nb, tt, _ = z_ref.shape
    tpos = lax.broadcasted_iota(jnp.int32, (nb, tt, 1), 1)

    def mix(lo, width):
        z = z_ref[:, :, lo:lo + width]
        zp = jnp.where(tpos == 0, first_ref[:, :, lo:lo + width], pltpu.roll(z, 1, axis=1))
        return z + (zp - z) * mu_ref[:, lo:lo + width]

    r_ref[...] = mix(0, dr)
    k_ref[...] = mix(dr, dr)
    v_ref[...] = mix(2 * dr, dr)
    zw = mix(3 * dr, lw).reshape(nb * tt, lw)
    za = mix(3 * dr + lw, la).reshape(nb * tt, la)
    xw = w0_ref[...] + _dot_f32(jnp.tanh(zw), w2h_ref[...], w2l_ref[...])
    w_log = jnp.minimum(xw, 0.0) - jnp.log1p(jnp.exp(-jnp.abs(xw))) - 0.5
    d_ref[...] = jnp.exp(-jnp.exp(w_log)).reshape(nb, tt, dr)
    xa = a0_ref[...] + _dot_f32(za, a2h_ref[...], a2l_ref[...])
    a_ref[...] = jax.nn.sigmoid(xa).reshape(nb, tt, dr)


def _rwkv_prep(z3, first, mu, w0, w2h, w2l, a0, a2h, a2l, *, dr, nb, name):
    n_seq, tt, cr = z3.shape
    lw, la = w2h.shape[0], a2h.shape[0]
    row = lambda c: pl.BlockSpec((1, c), lambda i: (0, 0))
    full = lambda a: pl.BlockSpec(a.shape, lambda i: (0, 0))
    out_spec = pl.BlockSpec((nb, tt, dr), lambda i: (i, 0, 0))
    out_sd = jax.ShapeDtypeStruct((n_seq, tt, dr), F32)
    return pl.pallas_call(
        functools.partial(_prep_body, dr=dr, lw=lw, la=la),
        grid=(n_seq // nb,),
        in_specs=[pl.BlockSpec((nb, tt, cr), lambda i: (i, 0, 0)),
                  pl.BlockSpec((nb, 1, cr), lambda i: (i, 0, 0)),
                  row(cr), row(dr), full(w2h), full(w2l), row(dr), full(a2h), full(a2l)],
        out_specs=[out_spec] * 5,
        out_shape=[out_sd] * 5,
        compiler_params=_cparams("parallel"),
        name=name,
    )(z3, first, mu, w0, w2h, w2l, a0, a2h, a2l)


def _scan_body(r_ref, k_ref, v_ref, d_ref, a_ref, s0_ref, kk_ref, ka_ref, rk_ref, lw_ref, lb_ref,
               y_ref, s_ref, yrow_ref):
    n = HEAD_SIZE
    tc = r_ref.shape[0]

    @pl.when(pl.program_id(1) == 0)
    def _():
        s_ref[...] = s0_ref[...]

    def colsum(x):
        return jnp.sum(x, axis=0, keepdims=True)

    def step(t, carry):
        r, k, v, w, a = r_ref[t], k_ref[t], v_ref[t], d_ref[t], a_ref[t]
        kk = k * kk_ref[...]
        kk = kk / jnp.maximum(jnp.sqrt(colsum(kk * kk)), KK_EPS)
        k2 = k * (1.0 + (a - 1.0) * ka_ref[...])
        b = kk * a
        for i in range(n):
            s_i = s_ref[i]
            s_kk = colsum(s_i * kk)
            s_i = s_i * w - s_kk * b + v_ref[t, i:i + 1, :] * k2
            s_ref[i] = s_i
            yrow_ref[i:i + 1, :] = colsum(s_i * r)
        y = yrow_ref[...]
        ym = jnp.mean(y, axis=0, keepdims=True)
        yc = y - ym
        yv = jnp.mean(yc * yc, axis=0, keepdims=True)
        yn = yc * lax.rsqrt(yv + GN_EPS) * lw_ref[...] + lb_ref[...]
        bonus = colsum(r * k2 * rk_ref[...]) * v
        y_ref[t] = yn + bonus
        return carry

    lax.fori_loop(0, tc, step, 0)


def _wkv_scan(r, k, v, d, a, s0, kkp, kap, rkp, lnw, lnb, *, tc, name):
    g, t, n, _ = r.shape
    tc = min(tc, t)
    seq = pl.BlockSpec((None, tc, n, LANES), lambda gi, ci: (gi, ci, 0, 0))
    par = pl.BlockSpec((None, n, LANES), lambda gi, ci: (gi, 0, 0))
    st = pl.BlockSpec((None, n, n, LANES), lambda gi, ci: (gi, 0, 0, 0))
    return pl.pallas_call(
        _scan_body,
        grid=(g, t // tc),
        in_specs=[seq] * 5 + [st] + [par] * 5,
        out_specs=[seq, st],
        out_shape=[jax.ShapeDtypeStruct((g, t, n, LANES), F32),
                   jax.ShapeDtypeStruct((g, n, n, LANES), F32)],
        scratch_shapes=[pltpu.VMEM((n, LANES), F32)],
        compiler_params=_cparams("parallel", "arbitrary"),
        name=name,
    )(r, k, v, d, a, s0, kkp, kap, rkp, lnw, lnb)


def _gate_body(y_ref, zg_ref, first_ref, mu_ref, g2h_ref, g2l_ref, o_ref):
    nb, tt, lg = zg_ref.shape
    tpos = lax.broadcasted_iota(jnp.int32, (nb, tt, 1), 1)
    z = zg_ref[...]
    zp = jnp.where(tpos == 0, first_ref[...], pltpu.roll(z, 1, axis=1))
    zg = (z + (zp - z) * mu_ref[...]).reshape(nb * tt, lg)
    g = _dot_f32(jax.nn.sigmoid(zg), g2h_ref[...], g2l_ref[...])
    o_ref[...] = (y_ref[...].reshape(nb * tt, -1) * g).reshape(o_ref.shape).astype(o_ref.dtype)


def _rwkv_gate(y3, z3, first, mu, g2h, g2l, *, zg_block, nb, name):
    n_seq, tt, dr = y3.shape
    lg = g2h.shape[0]
    return pl.pallas_call(
        _gate_body,
        grid=(n_seq // nb,),
        in_specs=[pl.BlockSpec((nb, tt, dr), lambda i: (i, 0, 0)),
                  pl.BlockSpec((nb, tt, lg), lambda i: (i, 0, zg_block)),
                  pl.BlockSpec((nb, 1, lg), lambda i: (i, 0, zg_block)),
                  pl.BlockSpec((1, lg), lambda i: (0, zg_block)),
                  pl.BlockSpec(g2h.shape, lambda i: (0, 0)),
                  pl.BlockSpec(g2l.shape, lambda i: (0, 0))],
        out_specs=pl.BlockSpec((nb, tt, dr), lambda i: (i, 0, 0)),
        out_shape=jax.ShapeDtypeStruct((n_seq, tt, dr), BF16),
        compiler_params=_cparams("parallel"),
        name=name,
    )(y3, z3, first, mu, g2h, g2l)


def _conv_body(u_ref, hist_ref, cw_ref, cb_ref, lw_ref, lb_ref, o_ref, full_ref, *, hist_rows, taps):
    nb, tt, dc = u_ref.shape
    full_ref[:, :hist_rows, :] = hist_ref[...]
    full_ref[:, hist_rows:, :] = u_ref[...]
    lead = hist_rows - (taps - 1)
    acc = jnp.zeros((nb, tt, dc), F32) + cb_ref[...]
    for j in range(taps):
        acc = acc + full_ref[:, lead + j:lead + j + tt, :] * cw_ref[j:j + 1, :]
    mu = jnp.mean(acc, axis=-1, keepdims=True)
    xc = acc - mu
    var = jnp.mean(xc * xc, axis=-1, keepdims=True)
    y = xc * lax.rsqrt(var + LN_EPS) * lw_ref[...] + lb_ref[...]
    o_ref[...] = (y * jax.nn.sigmoid(y)).astype(o_ref.dtype)


def _conv_module(u3, hist, cw, cb, lnw, lnb, *, nb, name):
    n_seq, tt, dc = u3.shape
    hist_rows = hist.shape[1]
    taps = cw.shape[0]
    row = pl.BlockSpec((1, dc), lambda i: (0, 0))
    return pl.pallas_call(
        functools.partial(_conv_body, hist_rows=hist_rows, taps=taps),
        grid=(n_seq // nb,),
        in_specs=[pl.BlockSpec((nb, tt, dc), lambda i: (i, 0, 0)),
                  pl.BlockSpec((nb, hist_rows, dc), lambda i: (i, 0, 0)),
                  pl.BlockSpec((taps, dc), lambda i: (0, 0)), row, row, row],
        out_specs=pl.BlockSpec((nb, tt, dc), lambda i: (i, 0, 0)),
        out_shape=jax.ShapeDtypeStruct((n_seq, tt, dc), BF16),
        scratch_shapes=[pltpu.VMEM((nb, hist_rows + tt, dc), F32)],
        compiler_params=_cparams("parallel"),
        name=name,
    )(u3, hist, cw, cb.reshape(1, dc), lnw.reshape(1, dc), lnb.reshape(1, dc))


def _split_hi_lo(w, rows):
    w = jnp.pad(w.astype(F32), ((0, rows - w.shape[0]), (0, 0)))
    hi = w.astype(BF16)
    return hi, (w - hi.astype(F32)).astype(BF16)


def _layer(h, pe, wkv0, shift0, conv0, lanes_are_batch_head, p):
    bsz, t, d = h.shape
    m = bsz * t
    dr, dc, n = p["dr"], p["dc"], HEAD_SIZE
    nh = dr // n
    cr = p["w_in_r"].shape[1]
    hbuf = conv0.shape[1] if conv0 is not None else p["conv_w"].shape[0] - 1
    x2 = h.reshape(m, d)

    z = _matmul([x2], [p["w_in_r"]], gain=p["g_mix"], tn=512, name="in_proj_rwkv")
    n_cb = dc // _pick_tile(dc, 512)
    u = _matmul([x2], [p["w_in_c"], p["w_in_c"]], gain=p["g_mix"], epi="glu", n_out=dc, tn=512,
                w_col_blocks=[0, n_cb], name="in_proj_glu")
    new_shift = _rmsnorm(h[:, -1, :], p["g_mix"], name="shift_norm")

    if t > 256:
        tt = 256
        s_per = t // tt
        z3 = z.reshape(bsz * s_per, tt, cr)
        first = jnp.concatenate(
            [jnp.zeros((bsz, 1, cr), F32), z.reshape(bsz, t, cr)[:, tt - 1:t - 1:tt, :]], axis=1
        ).reshape(bsz * s_per, 1, cr)
        nb = 1
    else:
        tt, s_per, nb = t, 1, 32
        z3 = z.reshape(bsz, t, cr)
        z_first = _matmul([shift0.astype(BF16)], [p["w_in_r"]], tn=512, name="shift_proj")
        first = z_first.reshape(bsz, 1, cr)
    r, k, v, dec, a = _rwkv_prep(z3, first, p["mu"], p["w0"], p["w2h"], p["w2l"], p["a0"], p["a2h"], p["a2l"],
                                 dr=dr, nb=nb, name="rwkv_prep")

    if lanes_are_batch_head:
        to_scan = lambda x: jnp.transpose(x.reshape(bsz, t, nh, n), (1, 3, 0, 2)).reshape(1, t, n, bsz * nh)
        par = lambda q: jnp.tile(q.reshape(nh, n).T, (1, bsz))[None]
        s0 = jnp.zeros((1, n, n, LANES), F32)
    else:
        to_scan = lambda x: jnp.transpose(x.reshape(bsz, t, nh, n), (2, 1, 3, 0))
        par = lambda q: jnp.broadcast_to(q.reshape(nh, n)[:, :, None], (nh, n, LANES))
        s0 = jnp.transpose(wkv0, (1, 2, 3, 0))
    y_s, s_new = _wkv_scan(to_scan(r), to_scan(k), to_scan(v), to_scan(dec), to_scan(a), s0,
                           par(p["k_k"]), par(p["k_a"]), par(p["r_k"]), par(p["lnx_w"]), par(p["lnx_b"]),
                           tc=32, name="wkv_scan")
    if lanes_are_batch_head:
        y_tok = jnp.transpose(y_s.reshape(t, n, bsz, nh), (2, 0, 3, 1))
        new_wkv = jnp.transpose(s_new.reshape(n, n, bsz, nh), (2, 3, 0, 1))
    else:
        y_tok = jnp.transpose(y_s, (3, 1, 0, 2))
        new_wkv = jnp.transpose(s_new, (3, 0, 1, 2))
    y3 = y_tok.reshape(bsz * s_per, tt, dr)
    y_rwkv = _rwkv_gate(y3, z3, first, p["mu"], p["g2h"], p["g2l"], zg_block=p["zg_block"], nb=nb,
                        name="rwkv_gate").reshape(m, dr)

    u3 = u.reshape(bsz, t, dc)
    hist_rows = _round_up(hbuf, 8)
    if conv0 is None:
        full = jnp.pad(u3, ((0, 0), (hist_rows, 0), (0, 0)))
        ub = u3.reshape(bsz * s_per, tt, dc)
        hist = jnp.stack([full[:, i * tt:i * tt + hist_rows, :] for i in range(s_per)], axis=1)
        hist = hist.reshape(bsz * s_per, hist_rows, dc)
        new_conv = u3[:, t - hbuf:, :]
        cnb = 1
    else:
        ub = u3
        hist = jnp.pad(conv0, ((0, 0), (hist_rows - hbuf, 0), (0, 0)))
        new_conv = jnp.concatenate([conv0, u3], axis=1)[:, -hbuf:, :]
        cnb = 16
    y_conv = _conv_module(ub, hist, p["conv_w"], p["conv_b"], p["conv_ln_w"], p["conv_ln_b"], nb=cnb,
                          name="conv_module").reshape(m, dc)

    h1 = _matmul([y_rwkv, y_conv], [p["w_out_a"], p["w_out_b"]], res=x2, epi="res", tn=512, name="out_proj")
    dff = p["w_down"].shape[0]
    tf = _pick_tile(dff, 512)
    act = _matmul([h1], [p["w_gate_up"], p["w_gate_up"]], gain=p["g_ffn"], epi="swiglu", n_out=dff, tn=tf,
                  out_dtype=BF16, w_col_blocks=[0, dff // tf], name="ffn_gate_up")
    h2 = _matmul([act], [p["w_down"]], res=h1, epi="res", tn=512, name="ffn_down")
    h3 = _matmul([h2, pe.reshape(m, -1)], [p["w_ple_gate"], p["w_ple_proj"]], gain=p["g_ple"], res=h2, epi="ple",
                 tn=512, name="ple")
    return h3.reshape(bsz, t, d), new_wkv, new_shift, new_conv


def _prep_params(i, g_mix, w_in, mu_shift, w0, w2, a0, a2, g2, k_k, k_a, r_k, lnx_w, lnx_b, conv_w, conv_b,
                 conv_ln_w, conv_ln_b, w_out, g_ffn, w_gate_up, w_down, g_ple, w_ple_gate, w_ple_proj):
    dr = w0.shape[-1]
    dc = conv_w.shape[-1]
    lw, la, lg = w2.shape[1], a2.shape[1], g2.shape[1]
    lwp, lap, lgp = (_round_up(x, LANES) for x in (lw, la, lg))

    def rwkv_cols(x):
        c = 3 * dr
        pad = lambda y, to: jnp.pad(y, [(0, 0)] * (y.ndim - 1) + [(0, to - y.shape[-1])])
        return jnp.concatenate([x[..., :c], pad(x[..., c:c + lw], lwp), pad(x[..., c + lw:c + lw + la], lap),
                                pad(x[..., c + lw + la:c + lw + la + lg], lgp)], axis=-1)

    rwkv_cols_n = 3 * dr + lw + la + lg
    wi = w_in[i]
    w2h, w2l = _split_hi_lo(w2[i], lwp)
    a2h, a2l = _split_hi_lo(a2[i], lap)
    g2h, g2l = _split_hi_lo(g2[i], lgp)
    assert (3 * dr + lwp + lap) % lgp == 0
    return dict(
        dr=dr, dc=dc,
        g_mix=g_mix[i], w_in_r=rwkv_cols(wi[:, :rwkv_cols_n]).astype(BF16), w_in_c=wi[:, rwkv_cols_n:].astype(BF16),
        mu=rwkv_cols(mu_shift[i][None, :]), zg_block=(3 * dr + lwp + lap) // lgp,
        w0=w0[i][None, :], w2h=w2h, w2l=w2l, a0=a0[i][None, :], a2h=a2h, a2l=a2l, g2h=g2h, g2l=g2l,
        k_k=k_k[i], k_a=k_a[i], r_k=r_k[i], lnx_w=lnx_w[i], lnx_b=lnx_b[i],
        conv_w=conv_w[i], conv_b=conv_b[i], conv_ln_w=conv_ln_w[i], conv_ln_b=conv_ln_b[i],
        w_out_a=w_out[i][:dr].astype(BF16), w_out_b=w_out[i][dr:].astype(BF16),
        g_ffn=g_ffn[i], w_gate_up=w_gate_up[i].astype(BF16), w_down=w_down[i].astype(BF16),
        g_ple=g_ple[i], w_ple_gate=w_ple_gate[i].astype(BF16), w_ple_proj=w_ple_proj[i].astype(BF16),
    )


def kernel(x_prompt, x_sample, state_wkv, state_shift, state_conv, p_prompt, p_sample, g_mix, w_in, mu_shift, w0, w2, a0, a2, g2, k_k, k_a, r_k, lnx_w, lnx_b, conv_w, conv_b, conv_ln_w, conv_ln_b, w_out, g_ffn, w_gate_up, w_down, g_ple, w_ple_gate, w_ple_proj, g_final):
    depth = w_in.shape[0]
    layer_params = (g_mix, w_in, mu_shift, w0, w2, a0, a2, g2, k_k, k_a, r_k, lnx_w, lnx_b, conv_w, conv_b,
                    conv_ln_w, conv_ln_b, w_out, g_ffn, w_gate_up, w_down, g_ple, w_ple_gate, w_ple_proj)
    hp, hs = x_prompt, x_sample
    outs = [[] for _ in range(6)]
    for i in range(depth):
        p = _prep_params(i, *layer_params)
        hp, s_w, s_sh, s_c = _layer(hp, p_prompt[i], None, None, None, True, p)
        outs[0].append(s_w); outs[1].append(s_sh); outs[2].append(s_c)
        hs, s_w, s_sh, s_c = _layer(hs, p_sample[i], state_wkv[i], state_shift[i], state_conv[i], False, p)
        outs[3].append(s_w); outs[4].append(s_sh); outs[5].append(s_c)
    d = hp.shape[-1]
    y_prompt = _rmsnorm(hp.reshape(-1, d), g_final, name="final_norm").reshape(hp.shape)
    y_sample = _rmsnorm(hs.reshape(-1, d), g_final, name="final_norm").reshape(hs.shape)
    return (y_prompt, y_sample) + tuple(jnp.stack(o) for o in outs)
```

```python
import functools

import jax
import jax.numpy as jnp
from jax import lax
from jax.experimental import pallas as pl
from jax.experimental.pallas import tpu as pltpu

HEAD_SIZE = 64
RMS_EPS = 1e-6
LN_EPS = 1e-5
GN_EPS = 64e-5
KK_EPS = 1e-12
LANES = 128
VMEM_LIMIT_BYTES = 56 * 2 ** 20

F32 = jnp.float32
BF16 = jnp.bfloat16


TILES = dict(in_proj_rwkv=(1024, 512), in_proj_glu=(1024, 512), shift_proj=(128, 512), out_proj=(1024, 512),
             ffn_gate_up=(1024, 256), ffn_down=(512, 512), ple=(1024, 512))


def _tiles(name):
    return dict(tm=TILES[name][0], tn=TILES[name][1], name=name)


def _cparams(*sem):
    return pltpu.CompilerParams(dimension_semantics=sem, vmem_limit_bytes=VMEM_LIMIT_BYTES)


def _round_up(n, m):
    return (n + m - 1) // m * m


def _pick_tile(n, pref):
    if n <= pref:
        return n
    t = pref - pref % LANES
    while t > LANES and n % t:
        t -= LANES
    assert n % t == 0, (n, pref)
    return t


def _mm_body(*refs, n_x, n_w, epi):
    refs = list(refs)
    x_refs = [refs.pop(0) for _ in range(n_x)]
    w_refs = [refs.pop(0) for _ in range(n_w)]
    res_ref = refs.pop(0) if epi in ("res", "ple") else None
    out_ref = refs.pop(0)

    def dot(x_ref, w_ref):
        return jnp.dot(x_ref[...].astype(BF16), w_ref[...], preferred_element_type=F32)

    if epi in ("store", "res"):
        acc = dot(x_refs[0], w_refs[0])
        for xr, wr in zip(x_refs[1:], w_refs[1:]):
            acc = acc + dot(xr, wr)
        if epi == "res":
            acc = res_ref[...] + acc
        out_ref[...] = acc.astype(out_ref.dtype)
    elif epi == "glu":
        out_ref[...] = (dot(x_refs[0], w_refs[0]) * jax.nn.sigmoid(dot(x_refs[0], w_refs[1]))).astype(out_ref.dtype)
    elif epi == "swiglu":
        gate = dot(x_refs[0], w_refs[0])
        out_ref[...] = (gate * jax.nn.sigmoid(gate) * dot(x_refs[0], w_refs[1])).astype(out_ref.dtype)
    elif epi == "ple":
        pg = jax.nn.sigmoid(dot(x_refs[0], w_refs[0]))
        out_ref[...] = (res_ref[...] + pg * dot(x_refs[1], w_refs[1])).astype(out_ref.dtype)
    else:
        raise ValueError(epi)


def _matmul(xs, ws, *, res=None, epi="store", out_dtype=F32, n_out=None, w_col_blocks=None, tm, tn, name):
    m = xs[0].shape[0]
    n_out = ws[0].shape[1] if n_out is None else n_out
    tm = _pick_tile(m, tm)
    tn = _pick_tile(n_out, tn)
    w_col_blocks = [0] * len(ws) if w_col_blocks is None else w_col_blocks
    in_specs = [pl.BlockSpec((tm, x.shape[1]), lambda i, j: (i, 0)) for x in xs]
    for w, off in zip(ws, w_col_blocks):
        in_specs.append(pl.BlockSpec((w.shape[0], tn), functools.partial(lambda i, j, off: (0, j + off), off=off)))
    if res is not None:
        in_specs.append(pl.BlockSpec((tm, tn), lambda i, j: (i, j)))
    return pl.pallas_call(
        functools.partial(_mm_body, n_x=len(xs), n_w=len(ws), epi=epi),
        grid=(m // tm, n_out // tn),
        in_specs=in_specs,
        out_specs=pl.BlockSpec((tm, tn), lambda i, j: (i, j)),
        out_shape=jax.ShapeDtypeStruct((m, n_out), out_dtype),
        compiler_params=_cparams("parallel", "arbitrary"),
        name=name,
    )(*xs, *ws, *([] if res is None else [res]))


def _rms_body(x_ref, g_ref, o_ref):
    x = x_ref[...].astype(F32)
    ms = jnp.mean(x * x, axis=-1, keepdims=True)
    o_ref[...] = (x * lax.rsqrt(ms + RMS_EPS) * g_ref[...]).astype(o_ref.dtype)


def _rmsnorm(x, gain, *, out_dtype=F32, tm=512, name="rmsnorm"):
    m, d = x.shape
    tm = _pick_tile(m, tm)
    return pl.pallas_call(
        _rms_body,
        grid=(m // tm,),
        in_specs=[pl.BlockSpec((tm, d), lambda i: (i, 0)), pl.BlockSpec((1, d), lambda i: (0, 0))],
        out_specs=pl.BlockSpec((tm, d), lambda i: (i, 0)),
        out_shape=jax.ShapeDtypeStruct((m, d), out_dtype),
        compiler_params=_cparams("parallel"),
        name=name,
    )(x, gain.reshape(1, d).astype(F32))


def _split3(x):
    hi = x.astype(BF16)
    r1 = x - hi.astype(F32)
    mid = r1.astype(BF16)
    lo = (r1 - mid.astype(F32)).astype(BF16)
    return hi, mid, lo


def _dot_f32(a, b_hi, b_lo):
    a_hi, a_mid, _ = _split3(a)
    dot = lambda p, q: jnp.dot(p, q, preferred_element_type=F32)
    return dot(a_hi, b_hi) + (dot(a_mid, b_hi) + dot(a_hi, b_lo))


def _prep_body(z_ref, first_ref, mu_ref, w0_ref, w2h_ref, w2l_ref, a0_ref, a2h_ref, a2l_ref,
               r_ref, k_ref, v_ref, d_ref, a_ref, *, dr, lw, la):
    nb, tt, _ = z_ref.shape
    tpos = lax.broadcasted_iota(jnp.int32, (nb, tt, 1), 1)

    def mix(lo, width):
        z = z_ref[:, :, lo:lo + width]
        zp = jnp.where(tpos == 0, first_ref[:, :, lo:lo + width], pltpu.roll(z, 1, axis=1))
        return z + (zp - z) * mu_ref[:, lo:lo + width]

    r_ref[...] = mix(0, dr)
    k_ref[...] = mix(dr, dr)
    v_ref[...] = mix(2 * dr, dr)
    zw = mix(3 * dr, lw).reshape(nb * tt, lw)
    za = mix(3 * dr + lw, la).reshape(nb * tt, la)
    xw = w0_ref[...] + _dot_f32(jnp.tanh(zw), w2h_ref[...], w2l_ref[...])
    w_log = jnp.minimum(xw, 0.0) - jnp.log1p(jnp.exp(-jnp.abs(xw))) - 0.5
    d_ref[...] = jnp.exp(-jnp.exp(w_log)).reshape(nb, tt, dr)
    xa = a0_ref[...] + _dot_f32(za, a2h_ref[...], a2l_ref[...])
    a_ref[...] = jax.nn.sigmoid(xa).reshape(nb, tt, dr)


def _rwkv_prep(z3, first, mu, w0, w2h, w2l, a0, a2h, a2l, *, dr, nb, name):
    n_seq, tt, cr = z3.shape
    lw, la = w2h.shape[0], a2h.shape[0]
    row = lambda c: pl.BlockSpec((1, c), lambda i: (0, 0))
    full = lambda a: pl.BlockSpec(a.shape, lambda i: (0, 0))
    out_spec = pl.BlockSpec((nb, tt, dr), lambda i: (i, 0, 0))
    out_sd = jax.ShapeDtypeStruct((n_seq, tt, dr), F32)
    return pl.pallas_call(
        functools.partial(_prep_body, dr=dr, lw=lw, la=la),
        grid=(n_seq // nb,),
        in_specs=[pl.BlockSpec((nb, tt, cr), lambda i: (i, 0, 0)),
                  pl.BlockSpec((nb, 1, cr), lambda i: (i, 0, 0)),
                  row(cr), row(dr), full(w2h), full(w2l), row(dr), full(a2h), full(a2l)],
        out_specs=[out_spec] * 5,
        out_shape=[out_sd] * 5,
        compiler_params=_cparams("parallel"),
        name=name,
    )(z3, first, mu, w0, w2h, w2l, a0, a2h, a2l)


def _scan_body(r_ref, k_ref, v_ref, d_ref, a_ref, s0_ref, kk_ref, ka_ref, rk_ref, lw_ref, lb_ref,
               y_ref, s_ref, yrow_ref):
    n = HEAD_SIZE
    tc = r_ref.shape[0]

    @pl.when(pl.program_id(1) == 0)
    def _():
        s_ref[...] = s0_ref[...]

    def colsum(x):
        return jnp.sum(x, axis=0, keepdims=True)

    def step(t, carry):
        r, k, v, w, a = r_ref[t], k_ref[t], v_ref[t], d_ref[t], a_ref[t]
        kk = k * kk_ref[...]
        kk = kk / jnp.maximum(jnp.sqrt(colsum(kk * kk)), KK_EPS)
        k2 = k * (1.0 + (a - 1.0) * ka_ref[...])
        b = kk * a
        for i in range(n):
            s_i = s_ref[i]
            s_kk = colsum(s_i * kk)
            s_i = s_i * w - s_kk * b + v_ref[t, i:i + 1, :] * k2
            s_ref[i] = s_i
            yrow_ref[i:i + 1, :] = colsum(s_i * r)
        y = yrow_ref[...]
        ym = jnp.mean(y, axis=0, keepdims=True)
        yc = y - ym
        yv = jnp.mean(yc * yc, axis=0, keepdims=True)
        yn = yc * lax.rsqrt(yv + GN_EPS) * lw_ref[...] + lb_ref[...]
        bonus = colsum(r * k2 * rk_ref[...]) * v
        y_ref[t] = yn + bonus
        return carry

    lax.fori_loop(0, tc, step, 0)


def _wkv_scan(r, k, v, d, a, s0, kkp, kap, rkp, lnw, lnb, *, tc, name):
    g, t, n, _ = r.shape
    tc = min(tc, t)
    seq = pl.BlockSpec((None, tc, n, LANES), lambda gi, ci: (gi, ci, 0, 0))
    par = pl.BlockSpec((None, n, LANES), lambda gi, ci: (gi, 0, 0))
    st = pl.BlockSpec((None, n, n, LANES), lambda gi, ci: (gi, 0, 0, 0))
    return pl.pallas_call(
        _scan_body,
        grid=(g, t // tc),
        in_specs=[seq] * 5 + [st] + [par] * 5,
        out_specs=[seq, st],
        out_shape=[jax.ShapeDtypeStruct((g, t, n, LANES), F32),
                   jax.ShapeDtypeStruct((g, n, n, LANES), F32)],
        scratch_shapes=[pltpu.VMEM((n, LANES), F32)],
        compiler_params=_cparams("parallel", "arbitrary"),
        name=name,
    )(r, k, v, d, a, s0, kkp, kap, rkp, lnw, lnb)


def _gate_body(y_ref, zg_ref, first_ref, mu_ref, g2h_ref, g2l_ref, o_ref):
    nb, tt, lg = zg_ref.shape
    tpos = lax.broadcasted_iota(jnp.int32, (nb, tt, 1), 1)
    z = zg_ref[...]
    zp = jnp.where(tpos == 0, first_ref[...], pltpu.roll(z, 1, axis=1))
    zg = (z + (zp - z) * mu_ref[...]).reshape(nb * tt, lg)
    g = _dot_f32(jax.nn.sigmoid(zg), g2h_ref[...], g2l_ref[...])
    o_ref[...] = (y_ref[...].reshape(nb * tt, -1) * g).reshape(o_ref.shape).astype(o_ref.dtype)


def _rwkv_gate(y3, z3, first, mu, g2h, g2l, *, zg_block, nb, name):
    n_seq, tt, dr = y3.shape
    lg = g2h.shape[0]
    return pl.pallas_call(
        _gate_body,
        grid=(n_seq // nb,),
        in_specs=[pl.BlockSpec((nb, tt, dr), lambda i: (i, 0, 0)),
                  pl.BlockSpec((nb, tt, lg), lambda i: (i, 0, zg_block)),
                  pl.BlockSpec((nb, 1, lg), lambda i: (i, 0, zg_block)),
                  pl.BlockSpec((1, lg), lambda i: (0, zg_block)),
                  pl.BlockSpec(g2h.shape, lambda i: (0, 0)),
                  pl.BlockSpec(g2l.shape, lambda i: (0, 0))],
        out_specs=pl.BlockSpec((nb, tt, dr), lambda i: (i, 0, 0)),
        out_shape=jax.ShapeDtypeStruct((n_seq, tt, dr), BF16),
        compiler_params=_cparams("parallel"),
        name=name,
    )(y3, z3, first, mu, g2h, g2l)


CONV_ROW_CHUNK = 32
CONV_LANE_CHUNK = 256
SUBLANES = 8


def _conv_body(u_ref, hist_ref, cw_ref, cb_ref, lw_ref, lb_ref, o_ref, full_ref, fs_ref, c_ref, *,
               hist_rows, taps, fresh):
    nb, tt, dc = u_ref.shape
    hin = hist_ref.shape[1]
    hist = hist_ref[...]
    if fresh:
        hist = jnp.where(pl.program_id(1) == 0, 0.0, hist)
    if hin < hist_rows:
        full_ref[:, :hist_rows - hin, :] = jnp.zeros((nb, hist_rows - hin, dc), F32)
    full_ref[:, hist_rows - hin:hist_rows, :] = hist
    full_ref[:, hist_rows:, :] = u_ref[...]
    lead = hist_rows - (taps - 1)
    span = hist_rows + tt - SUBLANES
    for s in range(1, SUBLANES):
        fs_ref[s - 1] = full_ref[:, s:s + span, :]
    rc = min(CONV_ROW_CHUNK, tt)
    lc = min(CONV_LANE_CHUNK, dc)

    def row_chunk(ri, carry):
        r0 = pl.multiple_of(ri * rc, rc)
        for b in range(nb):
            for c0 in range(0, dc, lc):
                acc = jnp.broadcast_to(cb_ref[:, c0:c0 + lc], (rc, lc))
                for j in range(taps):
                    s = (lead + j) % SUBLANES
                    rows = pl.ds(pl.multiple_of(r0 + (lead + j - s), SUBLANES), rc)
                    win = full_ref[b, rows, c0:c0 + lc] if s == 0 else fs_ref[s - 1, b, rows, c0:c0 + lc]
                    acc = acc + win * cw_ref[j:j + 1, c0:c0 + lc]
                c_ref[b, pl.ds(r0, rc), c0:c0 + lc] = acc
        return carry

    lax.fori_loop(0, tt // rc, row_chunk, 0)
    acc = c_ref[...]
    mu = jnp.mean(acc, axis=-1, keepdims=True)
    xc = acc - mu
    var = jnp.mean(xc * xc, axis=-1, keepdims=True)
    y = xc * lax.rsqrt(var + LN_EPS) * lw_ref[...] + lb_ref[...]
    o_ref[...] = (y * jax.nn.sigmoid(y)).astype(o_ref.dtype)


def _conv_module(u3, conv0, cw, cb, lnw, lnb, *, nb, tt, name):
    bsz, t, dc = u3.shape
    taps = cw.shape[0]
    hist_rows = _round_up(taps - 1, SUBLANES)
    fresh = conv0 is None
    if fresh:
        assert nb == 1 and tt % hist_rows == 0
        hist, hin = u3, hist_rows
        hist_map = lambda i, j: (i, jnp.maximum(j * (tt // hist_rows) - 1, 0), 0)
    else:
        assert tt == t
        hist, hin = conv0, conv0.shape[1]
        hist_map = lambda i, j: (i, 0, 0)
    row = pl.BlockSpec((1, dc), lambda i, j: (0, 0))
    return pl.pallas_call(
        functools.partial(_conv_body, hist_rows=hist_rows, taps=taps, fresh=fresh),
        grid=(bsz // nb, t // tt),
        in_specs=[pl.BlockSpec((nb, tt, dc), lambda i, j: (i, j, 0)),
                  pl.BlockSpec((nb, hin, dc), hist_map),
                  pl.BlockSpec((taps, dc), lambda i, j: (0, 0)), row, row, row],
        out_specs=pl.BlockSpec((nb, tt, dc), lambda i, j: (i, j, 0)),
        out_shape=jax.ShapeDtypeStruct((bsz, t, dc), BF16),
        scratch_shapes=[pltpu.VMEM((nb, hist_rows + tt, dc), F32),
                        pltpu.VMEM((SUBLANES - 1, nb, hist_rows + tt - SUBLANES, dc), F32),
                        pltpu.VMEM((nb, tt, dc), F32)],
        compiler_params=_cparams("parallel", "parallel"),
        name=name,
    )(u3, hist, cw, cb.reshape(1, dc), lnw.reshape(1, dc), lnb.reshape(1, dc))


def _split_hi_lo(w, rows):
    w = jnp.pad(w.astype(F32), ((0, rows - w.shape[0]), (0, 0)))
    hi = w.astype(BF16)
    return hi, (w - hi.astype(F32)).astype(BF16)


def _layer(h, pe, wkv0, shift0, conv0, lanes_are_batch_head, p):
    bsz, t, d = h.shape
    m = bsz * t
    dr, dc, n = p["dr"], p["dc"], HEAD_SIZE
    nh = dr // n
    cr = p["w_in_r"].shape[1]
    hbuf = conv0.shape[1] if conv0 is not None else p["conv_w"].shape[0] - 1
    x2 = h.reshape(m, d)

    xn = _rmsnorm(x2, p["g_mix"], out_dtype=BF16, name="mix_norm")
    z = _matmul([xn], [p["w_in_r"]], **_tiles("in_proj_rwkv"))
    tg = _pick_tile(dc, TILES["in_proj_glu"][1])
    u = _matmul([xn], [p["w_in_c"], p["w_in_c"]], epi="glu", n_out=dc, w_col_blocks=[0, dc // tg],
                **_tiles("in_proj_glu"))
    new_shift = _rmsnorm(h[:, -1, :], p["g_mix"], name="shift_norm")

    if t > 256:
        tt = 256
        s_per = t // tt
        z3 = z.reshape(bsz * s_per, tt, cr)
        first = jnp.concatenate(
            [jnp.zeros((bsz, 1, cr), F32), z.reshape(bsz, t, cr)[:, tt - 1:t - 1:tt, :]], axis=1
        ).reshape(bsz * s_per, 1, cr)
        nb = 1
    else:
        tt, s_per, nb = t, 1, 32
        z3 = z.reshape(bsz, t, cr)
        z_first = _matmul([shift0.astype(BF16)], [p["w_in_r"]], **_tiles("shift_proj"))
        first = z_first.reshape(bsz, 1, cr)
    r, k, v, dec, a = _rwkv_prep(z3, first, p["mu"], p["w0"], p["w2h"], p["w2l"], p["a0"], p["a2h"], p["a2l"],
                                 dr=dr, nb=nb, name="rwkv_prep")

    if lanes_are_batch_head:
        to_scan = lambda x: jnp.transpose(x.reshape(bsz, t, nh, n), (1, 3, 0, 2)).reshape(1, t, n, bsz * nh)
        par = lambda q: jnp.tile(q.reshape(nh, n).T, (1, bsz))[None]
        s0 = jnp.zeros((1, n, n, LANES), F32)
    else:
        to_scan = lambda x: jnp.transpose(x.reshape(bsz, t, nh, n), (2, 1, 3, 0))
        par = lambda q: jnp.broadcast_to(q.reshape(nh, n)[:, :, None], (nh, n, LANES))
        s0 = jnp.transpose(wkv0, (1, 2, 3, 0))
    y_s, s_new = _wkv_scan(to_scan(r), to_scan(k), to_scan(v), to_scan(dec), to_scan(a), s0,
                           par(p["k_k"]), par(p["k_a"]), par(p["r_k"]), par(p["lnx_w"]), par(p["lnx_b"]),
                           tc=32, name="wkv_scan")
    if lanes_are_batch_head:
        y_tok = jnp.transpose(y_s.reshape(t, n, bsz, nh), (2, 0, 3, 1))
        new_wkv = jnp.transpose(s_new.reshape(n, n, bsz, nh), (2, 3, 0, 1))
    else:
        y_tok = jnp.transpose(y_s, (3, 1, 0, 2))
        new_wkv = jnp.transpose(s_new, (3, 0, 1, 2))
    y3 = y_tok.reshape(bsz * s_per, tt, dr)
    y_rwkv = _rwkv_gate(y3, z3, first, p["mu"], p["g2h"], p["g2l"], zg_block=p["zg_block"], nb=nb,
                        name="rwkv_gate").reshape(m, dr)

    u3 = u.reshape(bsz, t, dc)
    if conv0 is None:
        new_conv = u3[:, t - hbuf:, :]
        cnb = 1
    else:
        new_conv = jnp.concatenate([conv0, u3], axis=1)[:, -hbuf:, :]
        cnb = 4
    y_conv = _conv_module(u3, conv0, p["conv_w"], p["conv_b"], p["conv_ln_w"], p["conv_ln_b"], nb=cnb, tt=tt,
                          name="conv_module").reshape(m, dc)

    h1 = _matmul([y_rwkv, y_conv], [p["w_out_a"], p["w_out_b"]], res=x2, epi="res", **_tiles("out_proj"))
    dff = p["w_down"].shape[0]
    tf = _pick_tile(dff, TILES["ffn_gate_up"][1])
    h1n = _rmsnorm(h1, p["g_ffn"], out_dtype=BF16, name="ffn_norm")
    act = _matmul([h1n], [p["w_gate_up"], p["w_gate_up"]], epi="swiglu", n_out=dff, out_dtype=BF16,
                  w_col_blocks=[0, dff // tf], **_tiles("ffn_gate_up"))
    h2 = _matmul([act], [p["w_down"]], res=h1, epi="res", **_tiles("ffn_down"))
    h2n = _rmsnorm(h2, p["g_ple"], out_dtype=BF16, name="ple_norm")
    h3 = _matmul([h2n, pe.reshape(m, -1)], [p["w_ple_gate"], p["w_ple_proj"]], res=h2, epi="ple", **_tiles("ple"))
    return h3.reshape(bsz, t, d), new_wkv, new_shift, new_conv


def _prep_params(i, g_mix, w_in, mu_shift, w0, w2, a0, a2, g2, k_k, k_a, r_k, lnx_w, lnx_b, conv_w, conv_b,
                 conv_ln_w, conv_ln_b, w_out, g_ffn, w_gate_up, w_down, g_ple, w_ple_gate, w_ple_proj):
    dr = w0.shape[-1]
    dc = conv_w.shape[-1]
    lw, la, lg = w2.shape[1], a2.shape[1], g2.shape[1]
    lwp, lap, lgp = (_round_up(x, LANES) for x in (lw, la, lg))

    def rwkv_cols(x):
        c = 3 * dr
        pad = lambda y, to: jnp.pad(y, [(0, 0)] * (y.ndim - 1) + [(0, to - y.shape[-1])])
        return jnp.concatenate([x[..., :c], pad(x[..., c:c + lw], lwp), pad(x[..., c + lw:c + lw + la], lap),
                                pad(x[..., c + lw + la:c + lw + la + lg], lgp)], axis=-1)

    rwkv_cols_n = 3 * dr + lw + la + lg
    wi = w_in[i]
    w2h, w2l = _split_hi_lo(w2[i], lwp)
    a2h, a2l = _split_hi_lo(a2[i], lap)
    g2h, g2l = _split_hi_lo(g2[i], lgp)
    assert (3 * dr + lwp + lap) % lgp == 0
    return dict(
        dr=dr, dc=dc,
        g_mix=g_mix[i], w_in_r=rwkv_cols(wi[:, :rwkv_cols_n]).astype(BF16), w_in_c=wi[:, rwkv_cols_n:].astype(BF16),
        mu=rwkv_cols(mu_shift[i][None, :]), zg_block=(3 * dr + lwp + lap) // lgp,
        w0=w0[i][None, :], w2h=w2h, w2l=w2l, a0=a0[i][None, :], a2h=a2h, a2l=a2l, g2h=g2h, g2l=g2l,
        k_k=k_k[i], k_a=k_a[i], r_k=r_k[i], lnx_w=lnx_w[i], lnx_b=lnx_b[i],
        conv_w=conv_w[i], conv_b=conv_b[i], conv_ln_w=conv_ln_w[i], conv_ln_b=conv_ln_b[i],
        w_out_a=w_out[i][:dr].astype(BF16), w_out_b=w_out[i][dr:].astype(BF16),
        g_ffn=g_ffn[i], w_gate_up=w_gate_up[i].astype(BF16), w_down=w_down[i].astype(BF16),
        g_ple=g_ple[i], w_ple_gate=w_ple_gate[i].astype(BF16), w_ple_proj=w_ple_proj[i].astype(BF16),
    )


def kernel(x_prompt, x_sample, state_wkv, state_shift, state_conv, p_prompt, p_sample, g_mix, w_in, mu_shift, w0, w2, a0, a2, g2, k_k, k_a, r_k, lnx_w, lnx_b, conv_w, conv_b, conv_ln_w, conv_ln_b, w_out, g_ffn, w_gate_up, w_down, g_ple, w_ple_gate, w_ple_proj, g_final):
    depth = w_in.shape[0]
    layer_params = (g_mix, w_in, mu_shift, w0, w2, a0, a2, g2, k_k, k_a, r_k, lnx_w, lnx_b, conv_w, conv_b,
                    conv_ln_w, conv_ln_b, w_out, g_ffn, w_gate_up, w_down, g_ple, w_ple_gate, w_ple_proj)
    hp, hs = x_prompt, x_sample
    outs = [[] for _ in range(6)]
    for i in range(depth):
        p = _prep_params(i, *layer_params)
        hp, s_w, s_sh, s_c = _layer(hp, p_prompt[i], None, None, None, True, p)
        outs[0].append(s_w); outs[1].append(s_sh); outs[2].append(s_c)
        hs, s_w, s_sh, s_c = _layer(hs, p_sample[i], state_wkv[i], state_shift[i], state_conv[i], False, p)
        outs[3].append(s_w); outs[4].append(s_sh); outs[5].append(s_c)
    d = hp.shape[-1]
    y_prompt = _rmsnorm(hp.reshape(-1, d), g_final, name="final_norm").reshape(hp.shape)
    y_sample = _rmsnorm(hs.reshape(-1, d), g_final, name="final_norm").reshape(hs.shape)
    return (y_prompt, y_sample) + tuple(jnp.stack(o) for o in outs)
```

```python
import functools

import jax
import jax.numpy as jnp
from jax import lax
from jax.experimental import pallas as pl
from jax.experimental.pallas import tpu as pltpu

HEAD_SIZE = 64
RMS_EPS = 1e-6
LN_EPS = 1e-5
GN_EPS = 64e-5
KK_EPS = 1e-12
LANES = 128
VMEM_LIMIT_BYTES = 56 * 2 ** 20

F32 = jnp.float32
BF16 = jnp.bfloat16


TILES = dict(in_proj_rkv=(2048, 512), in_proj_rkv_shift=(128, 512), in_proj_glu=(1024, 512), out_proj=(1024, 512),
             ffn_gate_up=(2048, 256), ffn_down=(512, 512), ple=(1024, 512))


def _tiles(name):
    return dict(tm=TILES[name][0], tn=TILES[name][1], name=name)


def _cparams(*sem):
    return pltpu.CompilerParams(dimension_semantics=sem, vmem_limit_bytes=VMEM_LIMIT_BYTES)


def _round_up(n, m):
    return (n + m - 1) // m * m


def _pick_tile(n, pref):
    if n <= pref:
        return n
    t = pref - pref % LANES
    while t > LANES and n % t:
        t -= LANES
    assert n % t == 0, (n, pref)
    return t


def _mm_body(*refs, n_x, n_w, epi):
    refs = list(refs)
    x_refs = [refs.pop(0) for _ in range(n_x)]
    w_refs = [refs.pop(0) for _ in range(n_w)]
    res_ref = refs.pop(0) if epi in ("res", "ple") else None
    out_ref = refs.pop(-1)

    def dot(x_ref, w_ref):
        return jnp.dot(x_ref[...].astype(BF16), w_ref[...].astype(BF16), preferred_element_type=F32)

    if epi in ("store", "res"):
        acc = dot(x_refs[0], w_refs[0])
        for xr, wr in zip(x_refs[1:], w_refs[1:]):
            acc = acc + dot(xr, wr)
        if epi == "res":
            acc = res_ref[...] + acc
        out_ref[...] = acc.astype(out_ref.dtype)
    elif epi == "glu":
        out_ref[...] = (dot(x_refs[0], w_refs[0]) * jax.nn.sigmoid(dot(x_refs[0], w_refs[1]))).astype(out_ref.dtype)
    elif epi == "swiglu":
        gate = dot(x_refs[0], w_refs[0])
        out_ref[...] = (gate * jax.nn.sigmoid(gate) * dot(x_refs[0], w_refs[1])).astype(out_ref.dtype)
    elif epi == "ple":
        pg = jax.nn.sigmoid(dot(x_refs[0], w_refs[0]))
        out_ref[...] = (res_ref[...] + pg * dot(x_refs[1], w_refs[1])).astype(out_ref.dtype)
    else:
        raise ValueError(epi)


def _matmul(xs, ws, *, res=None, epi="store", out_dtype=F32, n_out=None, out_cols=None, into=None, w_blocks=None,
            tm, tn, name):
    m = xs[0].shape[0]
    n_out = ws[0].shape[1] if n_out is None else n_out
    tm = _pick_tile(m, tm)
    tn = _pick_tile(n_out, tn)
    w_blocks = [(0, 0)] * len(ws) if w_blocks is None else w_blocks
    in_specs = [pl.BlockSpec((tm, x.shape[1]), lambda i, j: (i, 0), pipeline_mode=pl.Buffered(1)) for x in xs]
    for idx, (rb, cb) in enumerate(w_blocks):
        k = xs[min(idx, len(xs) - 1)].shape[1]
        in_specs.append(pl.BlockSpec((k, tn), functools.partial(lambda i, j, rb, cb: (rb, j + cb), rb=rb, cb=cb)))
    args = [*xs, *ws]
    if res is not None:
        in_specs.append(pl.BlockSpec((tm, tn), lambda i, j: (i, j)))
        args.append(res)
    aliases, out_cb = {}, 0
    if into is not None:
        dest, out_cb = into
        in_specs.append(pl.BlockSpec(memory_space=pl.ANY))
        aliases = {len(args): 0}
        args.append(dest)
        out_sd = jax.ShapeDtypeStruct(dest.shape, dest.dtype)
    else:
        out_sd = jax.ShapeDtypeStruct((m, n_out if out_cols is None else out_cols), out_dtype)
    return pl.pallas_call(
        functools.partial(_mm_body, n_x=len(xs), n_w=len(ws), epi=epi),
        grid=(m // tm, n_out // tn),
        in_specs=in_specs,
        out_specs=pl.BlockSpec((tm, tn), functools.partial(lambda i, j, cb: (i, j + cb), cb=out_cb)),
        out_shape=out_sd,
        input_output_aliases=aliases,
        compiler_params=_cparams("parallel", "arbitrary"),
        name=name,
    )(*args)


def _rms_body(x_ref, g_ref, o_ref):
    x = x_ref[...].astype(F32)
    ms = jnp.mean(x * x, axis=-1, keepdims=True)
    o_ref[...] = (x * lax.rsqrt(ms + RMS_EPS) * g_ref[...]).astype(o_ref.dtype)


def _rmsnorm(x, gain, *, out_dtype=F32, tm=512, name="rmsnorm"):
    m, d = x.shape
    tm = _pick_tile(m, tm)
    return pl.pallas_call(
        _rms_body,
        grid=(m // tm,),
        in_specs=[pl.BlockSpec((tm, d), lambda i: (i, 0)), pl.BlockSpec((1, d), lambda i: (0, 0))],
        out_specs=pl.BlockSpec((tm, d), lambda i: (i, 0)),
        out_shape=jax.ShapeDtypeStruct((m, d), out_dtype),
        compiler_params=_cparams("parallel"),
        name=name,
    )(x, gain.reshape(1, d).astype(F32))


def _split3(x):
    hi = x.astype(BF16)
    r1 = x - hi.astype(F32)
    mid = r1.astype(BF16)
    lo = (r1 - mid.astype(F32)).astype(BF16)
    return hi, mid, lo


def _dot_f32(a, b_hi, b_lo):
    a_hi, a_mid, _ = _split3(a)
    dot = lambda p, q: jnp.dot(p, q, preferred_element_type=F32)
    return dot(a_hi, b_hi) + (dot(a_mid, b_hi) + dot(a_hi, b_lo))


def _prep_body(z_ref, first_ref, mu_ref, w0_ref, w2h_ref, w2l_ref, a0_ref, a2h_ref, a2l_ref,
               r_ref, k_ref, v_ref, d_ref, a_ref, *, dr, lw, la):
    nb, tt, _ = z_ref.shape
    tpos = lax.broadcasted_iota(jnp.int32, (nb, tt, 1), 1)

    def mix(lo, width):
        z = z_ref[:, :, lo:lo + width]
        zp = jnp.where(tpos == 0, first_ref[:, :, lo:lo + width], pltpu.roll(z, 1, axis=1))
        return z + (zp - z) * mu_ref[:, lo:lo + width]

    r_ref[...] = mix(0, dr)
    k_ref[...] = mix(dr, dr)
    v_ref[...] = mix(2 * dr, dr)
    zw = mix(3 * dr, lw).reshape(nb * tt, lw)
    za = mix(3 * dr + lw, la).reshape(nb * tt, la)
    xw = w0_ref[...] + _dot_f32(jnp.tanh(zw), w2h_ref[...], w2l_ref[...])
    w_log = jnp.minimum(xw, 0.0) - jnp.log1p(jnp.exp(-jnp.abs(xw))) - 0.5
    d_ref[...] = jnp.exp(-jnp.exp(w_log)).reshape(nb, tt, dr)
    xa = a0_ref[...] + _dot_f32(za, a2h_ref[...], a2l_ref[...])
    a_ref[...] = jax.nn.sigmoid(xa).reshape(nb, tt, dr)


def _rwkv_prep(z3, first, mu, w0, w2h, w2l, a0, a2h, a2l, *, dr, nb, name):
    n_seq, tt, cr = z3.shape
    lw, la = w2h.shape[0], a2h.shape[0]
    row = lambda c: pl.BlockSpec((1, c), lambda i: (0, 0))
    full = lambda a: pl.BlockSpec(a.shape, lambda i: (0, 0))
    out_spec = pl.BlockSpec((nb, tt, dr), lambda i: (i, 0, 0))
    out_sd = jax.ShapeDtypeStruct((n_seq, tt, dr), F32)
    return pl.pallas_call(
        functools.partial(_prep_body, dr=dr, lw=lw, la=la),
        grid=(n_seq // nb,),
        in_specs=[pl.BlockSpec((nb, tt, cr), lambda i: (i, 0, 0)),
                  pl.BlockSpec((nb, 1, cr), lambda i: (i, 0, 0)),
                  row(cr), row(dr), full(w2h), full(w2l), row(dr), full(a2h), full(a2l)],
        out_specs=[out_spec] * 5,
        out_shape=[out_sd] * 5,
        compiler_params=_cparams("parallel"),
        name=name,
    )(z3, first, mu, w0, w2h, w2l, a0, a2h, a2l)


def _scan_body(r_ref, k_ref, v_ref, d_ref, a_ref, s0_ref, kk_ref, ka_ref, rk_ref, lw_ref, lb_ref,
               y_ref, s_ref, yrow_ref):
    n = HEAD_SIZE
    tc = r_ref.shape[0]

    @pl.when(pl.program_id(1) == 0)
    def _():
        s_ref[...] = s0_ref[...]

    def colsum(x):
        return jnp.sum(x, axis=0, keepdims=True)

    def step(t, gamma):
        r, k, v, w, a = r_ref[t], k_ref[t], v_ref[t], d_ref[t], a_ref[t]
        kk = k * kk_ref[...]
        kk = kk / jnp.maximum(jnp.sqrt(colsum(kk * kk)), KK_EPS)
        k2 = k * (1.0 + (a - 1.0) * ka_ref[...])
        kk_n = kk * gamma
        gamma = gamma * w
        inv = 1.0 / gamma
        b_n = kk * a * inv
        k_n = k2 * inv
        r_n = r * gamma
        for i in range(n):
            s_i = s_ref[i]
            s_kk = colsum(s_i * kk_n)
            s_i = s_i - s_kk * b_n + v_ref[t, i:i + 1, :] * k_n
            s_ref[i] = s_i
            yrow_ref[i:i + 1, :] = colsum(s_i * r_n)
        y = yrow_ref[...]
        ym = jnp.mean(y, axis=0, keepdims=True)
        yc = y - ym
        yv = jnp.mean(yc * yc, axis=0, keepdims=True)
        yn = yc * lax.rsqrt(yv + GN_EPS) * lw_ref[...] + lb_ref[...]
        bonus = colsum(r * k2 * rk_ref[...]) * v
        y_ref[t] = yn + bonus
        return gamma

    gamma = lax.fori_loop(0, tc, step, jnp.ones((n, LANES), F32))
    s_ref[...] = s_ref[...] * gamma


def _wkv_scan(r, k, v, d, a, s0, kkp, kap, rkp, lnw, lnb, *, tc, name):
    g, t, n, _ = r.shape
    tc = min(tc, t)
    seq = pl.BlockSpec((None, tc, n, LANES), lambda gi, ci: (gi, ci, 0, 0))
    par = pl.BlockSpec((None, n, LANES), lambda gi, ci: (gi, 0, 0))
    st = pl.BlockSpec((None, n, n, LANES), lambda gi, ci: (gi, 0, 0, 0))
    return pl.pallas_call(
        _scan_body,
        grid=(g, t // tc),
        in_specs=[seq] * 5 + [st] + [par] * 5,
        out_specs=[seq, st],
        out_shape=[jax.ShapeDtypeStruct((g, t, n, LANES), F32),
                   jax.ShapeDtypeStruct((g, n, n, LANES), F32)],
        scratch_shapes=[pltpu.VMEM((n, LANES), F32)],
        compiler_params=_cparams("parallel", "arbitrary"),
        name=name,
    )(r, k, v, d, a, s0, kkp, kap, rkp, lnw, lnb)


def _gate_body(y_ref, zg_ref, first_ref, mu_ref, g2h_ref, g2l_ref, o_ref):
    nb, tt, lg = zg_ref.shape
    tpos = lax.broadcasted_iota(jnp.int32, (nb, tt, 1), 1)
    z = zg_ref[...]
    zp = jnp.where(tpos == 0, first_ref[...], pltpu.roll(z, 1, axis=1))
    zg = (z + (zp - z) * mu_ref[...]).reshape(nb * tt, lg)
    g = _dot_f32(jax.nn.sigmoid(zg), g2h_ref[...], g2l_ref[...])
    o_ref[...] = (y_ref[...].reshape(nb * tt, -1) * g).reshape(o_ref.shape).astype(o_ref.dtype)


def _rwkv_gate(y3, z3, first, mu, g2h, g2l, *, zg_block, nb, name):
    n_seq, tt, dr = y3.shape
    lg = g2h.shape[0]
    return pl.pallas_call(
        _gate_body,
        grid=(n_seq // nb,),
        in_specs=[pl.BlockSpec((nb, tt, dr), lambda i: (i, 0, 0)),
                  pl.BlockSpec((nb, tt, lg), lambda i: (i, 0, zg_block)),
                  pl.BlockSpec((nb, 1, lg), lambda i: (i, 0, zg_block)),
                  pl.BlockSpec((1, lg), lambda i: (0, zg_block)),
                  pl.BlockSpec(g2h.shape, lambda i: (0, 0)),
                  pl.BlockSpec(g2l.shape, lambda i: (0, 0))],
        out_specs=pl.BlockSpec((nb, tt, dr), lambda i: (i, 0, 0)),
        out_shape=jax.ShapeDtypeStruct((n_seq, tt, dr), BF16),
        compiler_params=_cparams("parallel"),
        name=name,
    )(y3, z3, first, mu, g2h, g2l)


CONV_ROW_CHUNK = 32
CONV_LANE_CHUNK = 256
SUBLANES = 8


def _conv_body(u_ref, hist_ref, cw_ref, cb_ref, lw_ref, lb_ref, o_ref, full_ref, fs_ref, c_ref, *,
               hist_rows, taps, fresh):
    nb, tt, dc = u_ref.shape
    hin = hist_ref.shape[1]
    hist = hist_ref[...]
    if fresh:
        hist = jnp.where(pl.program_id(1) == 0, 0.0, hist)
    if hin < hist_rows:
        full_ref[:, :hist_rows - hin, :] = jnp.zeros((nb, hist_rows - hin, dc), F32)
    full_ref[:, hist_rows - hin:hist_rows, :] = hist
    full_ref[:, hist_rows:, :] = u_ref[...]
    lead = hist_rows - (taps - 1)
    span = hist_rows + tt - SUBLANES
    for s in range(1, SUBLANES):
        fs_ref[s - 1] = full_ref[:, s:s + span, :]
    rc = min(CONV_ROW_CHUNK, tt)
    lc = min(CONV_LANE_CHUNK, dc)

    def row_chunk(ri, carry):
        r0 = pl.multiple_of(ri * rc, rc)
        for b in range(nb):
            for c0 in range(0, dc, lc):
                acc = jnp.broadcast_to(cb_ref[:, c0:c0 + lc], (rc, lc))
                for j in range(taps):
                    s = (lead + j) % SUBLANES
                    rows = pl.ds(pl.multiple_of(r0 + (lead + j - s), SUBLANES), rc)
                    win = full_ref[b, rows, c0:c0 + lc] if s == 0 else fs_ref[s - 1, b, rows, c0:c0 + lc]
                    acc = acc + win * cw_ref[j:j + 1, c0:c0 + lc]
                c_ref[b, pl.ds(r0, rc), c0:c0 + lc] = acc
        return carry

    lax.fori_loop(0, tt // rc, row_chunk, 0)
    acc = c_ref[...]
    mu = jnp.mean(acc, axis=-1, keepdims=True)
    xc = acc - mu
    var = jnp.mean(xc * xc, axis=-1, keepdims=True)
    y = xc * lax.rsqrt(var + LN_EPS) * lw_ref[...] + lb_ref[...]
    o_ref[...] = (y * jax.nn.sigmoid(y)).astype(o_ref.dtype)


def _conv_module(u3, conv0, cw, cb, lnw, lnb, *, nb, tt, name):
    bsz, t, dc = u3.shape
    taps = cw.shape[0]
    hist_rows = _round_up(taps - 1, SUBLANES)
    fresh = conv0 is None
    if fresh:
        assert nb == 1 and tt % hist_rows == 0
        hist, hin = u3, hist_rows
        hist_map = lambda i, j: (i, jnp.maximum(j * (tt // hist_rows) - 1, 0), 0)
    else:
        assert tt == t
        hist, hin = conv0, conv0.shape[1]
        hist_map = lambda i, j: (i, 0, 0)
    row = pl.BlockSpec((1, dc), lambda i, j: (0, 0))
    return pl.pallas_call(
        functools.partial(_conv_body, hist_rows=hist_rows, taps=taps, fresh=fresh),
        grid=(bsz // nb, t // tt),
        in_specs=[pl.BlockSpec((nb, tt, dc), lambda i, j: (i, j, 0)),
                  pl.BlockSpec((nb, hin, dc), hist_map),
                  pl.BlockSpec((taps, dc), lambda i, j: (0, 0)), row, row, row],
        out_specs=pl.BlockSpec((nb, tt, dc), lambda i, j: (i, j, 0)),
        out_shape=jax.ShapeDtypeStruct((bsz, t, dc), BF16),
        scratch_shapes=[pltpu.VMEM((nb, hist_rows + tt, dc), F32),
                        pltpu.VMEM((SUBLANES - 1, nb, hist_rows + tt - SUBLANES, dc), F32),
                        pltpu.VMEM((nb, tt, dc), F32)],
        compiler_params=_cparams("parallel", "parallel"),
        name=name,
    )(u3, hist, cw, cb.reshape(1, dc), lnw.reshape(1, dc), lnb.reshape(1, dc))


def _split_hi_lo(w, rows):
    w = jnp.pad(w.astype(F32), ((0, rows - w.shape[0]), (0, 0)))
    hi = w.astype(BF16)
    return hi, (w - hi.astype(F32)).astype(BF16)


def _layer(h, pe, wkv0, shift0, conv0, lanes_are_batch_head, p):
    bsz, t, d = h.shape
    m = bsz * t
    dr, dc, n = p["dr"], p["dc"], HEAD_SIZE
    nh = dr // n
    cr = p["mu"].shape[1]
    hbuf = conv0.shape[1] if conv0 is not None else p["conv_w"].shape[0] - 1
    x2 = h.reshape(m, d)

    xn = _rmsnorm(x2, p["g_mix"], out_dtype=BF16, name="mix_norm")
    lo = p["w_lora"].shape[1]
    assert (3 * dr) % lo == 0 and cr == 3 * dr + lo

    def in_proj_rwkv(rows, tag):
        tm = TILES["in_proj_rkv" + tag][0]
        zz = _matmul([rows], [p["w_in"]], n_out=3 * dr, out_cols=cr, **_tiles("in_proj_rkv" + tag))
        return _matmul([rows], [p["w_lora"]], into=(zz, 3 * dr // lo), tm=tm, tn=lo, name="in_proj_lora" + tag)

    z = in_proj_rwkv(xn, "")
    tg = _pick_tile(dc, TILES["in_proj_glu"][1])
    u = _matmul([xn], [p["w_in_c"], p["w_in_c"]], epi="glu", n_out=dc, w_blocks=[(0, 0), (0, dc // tg)],
                **_tiles("in_proj_glu"))
    new_shift = _rmsnorm(h[:, -1, :], p["g_mix"], name="shift_norm")

    if t > 256:
        tt = 256
        s_per = t // tt
        z3 = z.reshape(bsz * s_per, tt, cr)
        first = jnp.concatenate(
            [jnp.zeros((bsz, 1, cr), F32), z.reshape(bsz, t, cr)[:, tt - 1:t - 1:tt, :]], axis=1
        ).reshape(bsz * s_per, 1, cr)
        nb = 1
    else:
        tt, s_per, nb = t, 1, 32
        z3 = z.reshape(bsz, t, cr)
        z_first = in_proj_rwkv(shift0.astype(BF16), "_shift")
        first = z_first.reshape(bsz, 1, cr)
    r, k, v, dec, a = _rwkv_prep(z3, first, p["mu"], p["w0"], p["w2h"], p["w2l"], p["a0"], p["a2h"], p["a2l"],
                                 dr=dr, nb=nb, name="rwkv_prep")

    if lanes_are_batch_head:
        to_scan = lambda x: jnp.transpose(x.reshape(bsz, t, nh, n), (1, 3, 0, 2)).reshape(1, t, n, bsz * nh)
        par = lambda q: jnp.tile(q.reshape(nh, n).T, (1, bsz))[None]
        s0 = jnp.zeros((1, n, n, LANES), F32)
    else:
        to_scan = lambda x: jnp.transpose(x.reshape(bsz, t, nh, n), (2, 1, 3, 0))
        par = lambda q: jnp.broadcast_to(q.reshape(nh, n)[:, :, None], (nh, n, LANES))
        s0 = jnp.transpose(wkv0, (1, 2, 3, 0))
    y_s, s_new = _wkv_scan(to_scan(r), to_scan(k), to_scan(v), to_scan(dec), to_scan(a), s0,
                           par(p["k_k"]), par(p["k_a"]), par(p["r_k"]), par(p["lnx_w"]), par(p["lnx_b"]),
                           tc=32, name="wkv_scan")
    if lanes_are_batch_head:
        y_tok = jnp.transpose(y_s.reshape(t, n, bsz, nh), (2, 0, 3, 1))
        new_wkv = jnp.transpose(s_new.reshape(n, n, bsz, nh), (2, 3, 0, 1))
    else:
        y_tok = jnp.transpose(y_s, (3, 1, 0, 2))
        new_wkv = jnp.transpose(s_new, (3, 0, 1, 2))
    y3 = y_tok.reshape(bsz * s_per, tt, dr)
    y_rwkv = _rwkv_gate(y3, z3, first, p["mu"], p["g2h"], p["g2l"], zg_block=p["zg_block"], nb=nb,
                        name="rwkv_gate").reshape(m, dr)

    u3 = u.reshape(bsz, t, dc)
    if conv0 is None:
        new_conv = u3[:, t - hbuf:, :]
        cnb = 1
    else:
        new_conv = jnp.concatenate([conv0, u3], axis=1)[:, -hbuf:, :]
        cnb = 4
    y_conv = _conv_module(u3, conv0, p["conv_w"], p["conv_b"], p["conv_ln_w"], p["conv_ln_b"], nb=cnb, tt=tt,
                          name="conv_module").reshape(m, dc)

    assert dr == dc
    h1 = _matmul([y_rwkv, y_conv], [p["w_out"], p["w_out"]], w_blocks=[(0, 0), (1, 0)], res=x2, epi="res",
                 **_tiles("out_proj"))
    dff = p["w_down"].shape[0]
    tf = _pick_tile(dff, TILES["ffn_gate_up"][1])
    h1n = _rmsnorm(h1, p["g_ffn"], out_dtype=BF16, name="ffn_norm")
    act = _matmul([h1n], [p["w_gate_up"], p["w_gate_up"]], epi="swiglu", n_out=dff, out_dtype=BF16,
                  w_blocks=[(0, 0), (0, dff // tf)], **_tiles("ffn_gate_up"))
    h2 = _matmul([act], [p["w_down"]], res=h1, epi="res", **_tiles("ffn_down"))
    h2n = _rmsnorm(h2, p["g_ple"], out_dtype=BF16, name="ple_norm")
    h3 = _matmul([h2n, pe.reshape(m, -1)], [p["w_ple_gate"], p["w_ple_proj"]], res=h2, epi="ple", **_tiles("ple"))
    return h3.reshape(bsz, t, d), new_wkv, new_shift, new_conv


def _prep_params(i, g_mix, w_in, mu_shift, w0, w2, a0, a2, g2, k_k, k_a, r_k, lnx_w, lnx_b, conv_w, conv_b,
                 conv_ln_w, conv_ln_b, w_out, g_ffn, w_gate_up, w_down, g_ple, w_ple_gate, w_ple_proj):
    dr = w0.shape[-1]
    dc = conv_w.shape[-1]
    lw, la, lg = w2.shape[1], a2.shape[1], g2.shape[1]
    lwp, lap, lgp = (_round_up(x, LANES) for x in (lw, la, lg))

    def lora_cols(x):
        c = 3 * dr
        pad = lambda y, to: jnp.pad(y, [(0, 0)] * (y.ndim - 1) + [(0, to - y.shape[-1])])
        return jnp.concatenate([pad(x[..., c:c + lw], lwp), pad(x[..., c + lw:c + lw + la], lap),
                                pad(x[..., c + lw + la:c + lw + la + lg], lgp)], axis=-1)

    rwkv_cols_n = 3 * dr + lw + la + lg
    wi = w_in[i]
    w2h, w2l = _split_hi_lo(w2[i], lwp)
    a2h, a2l = _split_hi_lo(a2[i], lap)
    g2h, g2l = _split_hi_lo(g2[i], lgp)
    assert (3 * dr + lwp + lap) % lgp == 0
    return dict(
        dr=dr, dc=dc,
        g_mix=g_mix[i], w_in=wi, w_lora=lora_cols(wi[:, :rwkv_cols_n]).astype(BF16),
        w_in_c=wi[:, rwkv_cols_n:].astype(BF16),
        mu=jnp.concatenate([mu_shift[i][None, :3 * dr], lora_cols(mu_shift[i][None, :])], axis=-1),
        zg_block=(3 * dr + lwp + lap) // lgp,
        w0=w0[i][None, :], w2h=w2h, w2l=w2l, a0=a0[i][None, :], a2h=a2h, a2l=a2l, g2h=g2h, g2l=g2l,
        k_k=k_k[i], k_a=k_a[i], r_k=r_k[i], lnx_w=lnx_w[i], lnx_b=lnx_b[i],
        conv_w=conv_w[i], conv_b=conv_b[i], conv_ln_w=conv_ln_w[i], conv_ln_b=conv_ln_b[i],
        w_out=w_out[i], g_ffn=g_ffn[i], w_gate_up=w_gate_up[i], w_down=w_down[i].astype(BF16),
        g_ple=g_ple[i], w_ple_gate=w_ple_gate[i], w_ple_proj=w_ple_proj[i],
    )


def kernel(x_prompt, x_sample, state_wkv, state_shift, state_conv, p_prompt, p_sample, g_mix, w_in, mu_shift, w0, w2, a0, a2, g2, k_k, k_a, r_k, lnx_w, lnx_b, conv_w, conv_b, conv_ln_w, conv_ln_b, w_out, g_ffn, w_gate_up, w_down, g_ple, w_ple_gate, w_ple_proj, g_final):
    depth = w_in.shape[0]
    layer_params = (g_mix, w_in, mu_shift, w0, w2, a0, a2, g2, k_k, k_a, r_k, lnx_w, lnx_b, conv_w, conv_b,
                    conv_ln_w, conv_ln_b, w_out, g_ffn, w_gate_up, w_down, g_ple, w_ple_gate, w_ple_proj)
    hp, hs = x_prompt, x_sample
    outs = [[] for _ in range(6)]
    for i in range(depth):
        p = _prep_params(i, *layer_params)
        hp, s_w, s_sh, s_c = _layer(hp, p_prompt[i], None, None, None, True, p)
        outs[0].append(s_w); outs[1].append(s_sh); outs[2].append(s_c)
        hs, s_w, s_sh, s_c = _layer(hs, p_sample[i], state_wkv[i], state_shift[i], state_conv[i], False, p)
        outs[3].append(s_w); outs[4].append(s_sh); outs[5].append(s_c)
    d = hp.shape[-1]
    y_prompt = _rmsnorm(hp.reshape(-1, d), g_final, name="final_norm").reshape(hp.shape)
    y_sample = _rmsnorm(hs.reshape(-1, d), g_final, name="final_norm").reshape(hs.shape)
    return (y_prompt, y_sample) + tuple(jnp.stack(o) for o in outs)
```

```python
import functools

import jax
import jax.numpy as jnp
from jax import lax
from jax.experimental import pallas as pl
from jax.experimental.pallas import tpu as pltpu

HEAD_SIZE = 64
RMS_EPS = 1e-6
LN_EPS = 1e-5
GN_EPS = 64e-5
KK_EPS = 1e-12
LANES = 128
SUBLANES = 8
VMEM_LIMIT_BYTES = 56 * 2 ** 20

F32 = jnp.float32
BF16 = jnp.bfloat16


TILES = dict(in_proj_rwkv=(1024, 512, 2), shift_proj=(128, 512, 2), in_proj_glu=(1024, 512, 2),
             out_proj=(1024, 512, 2), ffn_gate_up=(2048, 256, 1), ffn_down=(512, 512, 2), ple=(1024, 512, 2))


def _tiles(name):
    tm, tn, x_buffers = TILES[name]
    return dict(tm=tm, tn=tn, x_buffers=x_buffers, name=name)


def _cparams(*sem):
    return pltpu.CompilerParams(dimension_semantics=sem, vmem_limit_bytes=VMEM_LIMIT_BYTES)


def _round_up(n, m):
    return (n + m - 1) // m * m


def _pick_tile(n, pref):
    if n <= pref:
        return n
    t = pref - pref % LANES
    while t > LANES and n % t:
        t -= LANES
    assert n % t == 0, (n, pref)
    return t


def _mm_body(*refs, n_x, n_w, epi):
    refs = list(refs)
    x_refs = [refs.pop(0) for _ in range(n_x)]
    w_refs = [refs.pop(0) for _ in range(n_w)]
    res_ref = refs.pop(0) if epi in ("res", "ple") else None
    out_ref = refs.pop(0)

    def dot(x_ref, w_ref):
        return jnp.dot(x_ref[...].astype(BF16), w_ref[...].astype(BF16), preferred_element_type=F32)

    if epi in ("store", "res"):
        acc = dot(x_refs[0], w_refs[0])
        for xr, wr in zip(x_refs[1:], w_refs[1:]):
            acc = acc + dot(xr, wr)
        if epi == "res":
            acc = res_ref[...] + acc
        out_ref[...] = acc.astype(out_ref.dtype)
    elif epi == "glu":
        out_ref[...] = (dot(x_refs[0], w_refs[0]) * jax.nn.sigmoid(dot(x_refs[0], w_refs[1]))).astype(out_ref.dtype)
    elif epi == "swiglu":
        gate = dot(x_refs[0], w_refs[0])
        out_ref[...] = (gate * jax.nn.sigmoid(gate) * dot(x_refs[0], w_refs[1])).astype(out_ref.dtype)
    elif epi == "ple":
        pg = jax.nn.sigmoid(dot(x_refs[0], w_refs[0]))
        out_ref[...] = (res_ref[...] + pg * dot(x_refs[1], w_refs[1])).astype(out_ref.dtype)
    else:
        raise ValueError(epi)


def _matmul(xs, ws, *, res=None, epi="store", out_dtype=F32, n_out=None, w_blocks=None, tm, tn, x_buffers, name):
    m = xs[0].shape[0]
    n_out = ws[0].shape[1] if n_out is None else n_out
    tm = _pick_tile(m, tm)
    tn = _pick_tile(n_out, tn)
    w_blocks = [(0, 0)] * len(ws) if w_blocks is None else w_blocks
    in_specs = [pl.BlockSpec((tm, x.shape[1]), lambda i, j: (i, 0), pipeline_mode=pl.Buffered(x_buffers))
                for x in xs]
    for idx, (rb, cb) in enumerate(w_blocks):
        k = xs[min(idx, len(xs) - 1)].shape[1]
        in_specs.append(pl.BlockSpec((k, tn), functools.partial(lambda i, j, rb, cb: (rb, j + cb), rb=rb, cb=cb)))
    if res is not None:
        in_specs.append(pl.BlockSpec((tm, tn), lambda i, j: (i, j)))
    return pl.pallas_call(
        functools.partial(_mm_body, n_x=len(xs), n_w=len(ws), epi=epi),
        grid=(m // tm, n_out // tn),
        in_specs=in_specs,
        out_specs=pl.BlockSpec((tm, tn), lambda i, j: (i, j)),
        out_shape=jax.ShapeDtypeStruct((m, n_out), out_dtype),
        compiler_params=_cparams("parallel", "arbitrary"),
        name=name,
    )(*xs, *ws, *([] if res is None else [res]))


def _rms_body(x_ref, g_ref, o_ref):
    x = x_ref[...].astype(F32)
    ms = jnp.mean(x * x, axis=-1, keepdims=True)
    o_ref[...] = (x * lax.rsqrt(ms + RMS_EPS) * g_ref[...]).astype(o_ref.dtype)


def _rmsnorm(x, gain, *, out_dtype=F32, tm=512, name="rmsnorm"):
    m, d = x.shape
    tm = _pick_tile(m, tm)
    return pl.pallas_call(
        _rms_body,
        grid=(m // tm,),
        in_specs=[pl.BlockSpec((tm, d), lambda i: (i, 0)), pl.BlockSpec((1, d), lambda i: (0, 0))],
        out_specs=pl.BlockSpec((tm, d), lambda i: (i, 0)),
        out_shape=jax.ShapeDtypeStruct((m, d), out_dtype),
        compiler_params=_cparams("parallel"),
        name=name,
    )(x, gain.reshape(1, d).astype(F32))


def _split3(x):
    hi = x.astype(BF16)
    r1 = x - hi.astype(F32)
    mid = r1.astype(BF16)
    lo = (r1 - mid.astype(F32)).astype(BF16)
    return hi, mid, lo


def _dot_f32(a, b_hi, b_lo):
    a_hi, a_mid, _ = _split3(a)
    dot = lambda p, q: jnp.dot(p, q, preferred_element_type=F32)
    return dot(a_hi, b_hi) + (dot(a_mid, b_hi) + dot(a_hi, b_lo))


def _prep_body(z_ref, first_ref, mu_ref, w0_ref, w2h_ref, w2l_ref, a0_ref, a2h_ref, a2l_ref,
               r_ref, k_ref, v_ref, d_ref, a_ref, *, dr, lw, la):
    nb, tt, _ = z_ref.shape
    tpos = lax.broadcasted_iota(jnp.int32, (nb, tt, 1), 1)

    def mix(lo, width):
        z = z_ref[:, :, lo:lo + width]
        zp = jnp.where(tpos == 0, first_ref[:, :, lo:lo + width], pltpu.roll(z, 1, axis=1))
        return z + (zp - z) * mu_ref[:, lo:lo + width]

    r_ref[...] = mix(0, dr)
    k_ref[...] = mix(dr, dr)
    v_ref[...] = mix(2 * dr, dr)
    zw = mix(3 * dr, lw).reshape(nb * tt, lw)
    za = mix(3 * dr + lw, la).reshape(nb * tt, la)
    xw = w0_ref[...] + _dot_f32(jnp.tanh(zw), w2h_ref[...], w2l_ref[...])
    w_log = jnp.minimum(xw, 0.0) - jnp.log1p(jnp.exp(-jnp.abs(xw))) - 0.5
    d_ref[...] = jnp.exp(-jnp.exp(w_log)).reshape(nb, tt, dr)
    xa = a0_ref[...] + _dot_f32(za, a2h_ref[...], a2l_ref[...])
    a_ref[...] = jax.nn.sigmoid(xa).reshape(nb, tt, dr)


def _rwkv_prep(z3, first, mu, w0, w2h, w2l, a0, a2h, a2l, *, dr, nb, name):
    n_seq, tt, cr = z3.shape
    lw, la = w2h.shape[0], a2h.shape[0]
    row = lambda c: pl.BlockSpec((1, c), lambda i: (0, 0))
    full = lambda a: pl.BlockSpec(a.shape, lambda i: (0, 0))
    out_spec = pl.BlockSpec((nb, tt, dr), lambda i: (i, 0, 0))
    out_sd = jax.ShapeDtypeStruct((n_seq, tt, dr), F32)
    return pl.pallas_call(
        functools.partial(_prep_body, dr=dr, lw=lw, la=la),
        grid=(n_seq // nb,),
        in_specs=[pl.BlockSpec((nb, tt, cr), lambda i: (i, 0, 0)),
                  pl.BlockSpec((nb, 1, cr), lambda i: (i, 0, 0)),
                  row(cr), row(dr), full(w2h), full(w2l), row(dr), full(a2h), full(a2l)],
        out_specs=[out_spec] * 5,
        out_shape=[out_sd] * 5,
        compiler_params=_cparams("parallel"),
        name=name,
    )(z3, first, mu, w0, w2h, w2l, a0, a2h, a2l)


RELAYOUT_ROWS = 128
SCAN_PITCH = HEAD_SIZE + SUBLANES


def _to_scan_body(*refs, nh, bsz):
    x_refs, o_ref, xt_ref = refs[:bsz], refs[bsz], refs[bsz + 1]
    n, tq = HEAD_SIZE, RELAYOUT_ROWS
    for b in range(bsz):
        xt = x_refs[b][0].T
        for h in range(nh):
            xt_ref[b, h * SCAN_PITCH:h * SCAN_PITCH + n, :] = xt[h * n:(h + 1) * n, :]
    for j in range(n):
        m = jnp.concatenate([xt_ref[b, pl.ds(j, nh, stride=SCAN_PITCH), :] for b in range(bsz)], axis=0)
        o_ref[pl.ds(j, tq, stride=SCAN_PITCH), :] = m.T
    for j in range(n, SCAN_PITCH):
        o_ref[pl.ds(j, tq, stride=SCAN_PITCH), :] = jnp.zeros((tq, LANES), F32)


def _to_scan_bh(x3, *, nh, name):
    bsz, t, d = x3.shape
    assert bsz * nh == LANES and d == nh * HEAD_SIZE and t % RELAYOUT_ROWS == 0
    spec = lambda b: pl.BlockSpec((1, RELAYOUT_ROWS, d), functools.partial(lambda i, b: (b, i, 0), b=b))
    out = pl.pallas_call(
        functools.partial(_to_scan_body, nh=nh, bsz=bsz),
        grid=(t // RELAYOUT_ROWS,),
        in_specs=[spec(b) for b in range(bsz)],
        out_specs=pl.BlockSpec((RELAYOUT_ROWS * SCAN_PITCH, LANES), lambda i: (i, 0)),
        out_shape=jax.ShapeDtypeStruct((t * SCAN_PITCH, LANES), F32),
        scratch_shapes=[pltpu.VMEM((bsz, nh * SCAN_PITCH, RELAYOUT_ROWS), F32)],
        compiler_params=_cparams("parallel"),
        name=name,
    )(*([x3] * bsz))
    return out.reshape(1, t, SCAN_PITCH, LANES)


def _from_scan_body(*refs, nh, bsz):
    y_ref, o_ref, yt_ref = refs
    n, tq = HEAD_SIZE, RELAYOUT_ROWS
    for j in range(n):
        mt = y_ref[pl.ds(j, tq, stride=n), :].T
        for b in range(bsz):
            yt_ref[b, pl.ds(j, nh, stride=SCAN_PITCH), :] = mt[b * nh:(b + 1) * nh, :]
    for b in range(bsz):
        yt = jnp.concatenate([yt_ref[b, h * SCAN_PITCH:h * SCAN_PITCH + n, :] for h in range(nh)], axis=0)
        o_ref[b] = yt.T


def _from_scan_bh(y, *, bsz, nh, name):
    t, n, _ = y.shape
    d = nh * n
    assert bsz * nh == LANES and n == HEAD_SIZE and t % RELAYOUT_ROWS == 0
    return pl.pallas_call(
        functools.partial(_from_scan_body, nh=nh, bsz=bsz),
        grid=(t // RELAYOUT_ROWS,),
        in_specs=[pl.BlockSpec((RELAYOUT_ROWS * n, LANES), lambda i: (i, 0))],
        out_specs=pl.BlockSpec((bsz, RELAYOUT_ROWS, d), lambda i: (0, i, 0)),
        out_shape=jax.ShapeDtypeStruct((bsz, t, d), F32),
        scratch_shapes=[pltpu.VMEM((bsz, nh * SCAN_PITCH, RELAYOUT_ROWS), F32)],
        compiler_params=_cparams("parallel"),
        name=name,
    )(y.reshape(t * n, LANES))


def _scan_body(r_ref, k_ref, v_ref, d_ref, a_ref, s0_ref, kk_ref, ka_ref, rk_ref, lw_ref, lb_ref,
               y_ref, s_ref, yrow_ref):
    n = HEAD_SIZE
    tc = r_ref.shape[0]

    @pl.when(pl.program_id(1) == 0)
    def _():
        s_ref[...] = s0_ref[...]

    def colsum(x):
        return jnp.sum(x, axis=0, keepdims=True)

    def step(t, gamma):
        r, k, v, w, a = (x[t, :n, :] for x in (r_ref, k_ref, v_ref, d_ref, a_ref))
        kk = k * kk_ref[...]
        kk = kk / jnp.maximum(jnp.sqrt(colsum(kk * kk)), KK_EPS)
        k2 = k * (1.0 + (a - 1.0) * ka_ref[...])
        kk_n = kk * gamma
        gamma = gamma * w
        inv = 1.0 / gamma
        b_n = kk * a * inv
        k_n = k2 * inv
        r_n = r * gamma
        for i in range(n):
            s_i = s_ref[i]
            s_kk = colsum(s_i * kk_n)
            s_i = s_i - s_kk * b_n + v_ref[t, i:i + 1, :] * k_n
            s_ref[i] = s_i
            yrow_ref[i:i + 1, :] = colsum(s_i * r_n)
        y = yrow_ref[...]
        ym = jnp.mean(y, axis=0, keepdims=True)
        yc = y - ym
        yv = jnp.mean(yc * yc, axis=0, keepdims=True)
        yn = yc * lax.rsqrt(yv + GN_EPS) * lw_ref[...] + lb_ref[...]
        bonus = colsum(r * k2 * rk_ref[...]) * v
        y_ref[t] = yn + bonus
        return gamma

    gamma = lax.fori_loop(0, tc, step, jnp.ones((n, LANES), F32))
    s_ref[...] = s_ref[...] * gamma


def _wkv_scan(r, k, v, d, a, s0, kkp, kap, rkp, lnw, lnb, *, tc, name):
    g, t, pitch, _ = r.shape
    n = HEAD_SIZE
    tc = min(tc, t)
    seq_in = pl.BlockSpec((None, tc, pitch, LANES), lambda gi, ci: (gi, ci, 0, 0))
    seq = pl.BlockSpec((None, tc, n, LANES), lambda gi, ci: (gi, ci, 0, 0))
    par = pl.BlockSpec((None, n, LANES), lambda gi, ci: (gi, 0, 0))
    st = pl.BlockSpec((None, n, n, LANES), lambda gi, ci: (gi, 0, 0, 0))
    return pl.pallas_call(
        _scan_body,
        grid=(g, t // tc),
        in_specs=[seq_in] * 5 + [st] + [par] * 5,
        out_specs=[seq, st],
        out_shape=[jax.ShapeDtypeStruct((g, t, n, LANES), F32),
                   jax.ShapeDtypeStruct((g, n, n, LANES), F32)],
        scratch_shapes=[pltpu.VMEM((n, LANES), F32)],
        compiler_params=_cparams("parallel", "arbitrary"),
        name=name,
    )(r, k, v, d, a, s0, kkp, kap, rkp, lnw, lnb)


def _gate_body(y_ref, zg_ref, first_ref, mu_ref, g2h_ref, g2l_ref, o_ref):
    nb, tt, lg = zg_ref.shape
    tpos = lax.broadcasted_iota(jnp.int32, (nb, tt, 1), 1)
    z = zg_ref[...]
    zp = jnp.where(tpos == 0, first_ref[...], pltpu.roll(z, 1, axis=1))
    zg = (z + (zp - z) * mu_ref[...]).reshape(nb * tt, lg)
    g = _dot_f32(jax.nn.sigmoid(zg), g2h_ref[...], g2l_ref[...])
    o_ref[...] = (y_ref[...].reshape(nb * tt, -1) * g).reshape(o_ref.shape).astype(o_ref.dtype)


def _rwkv_gate(y3, z3, first, mu, g2h, g2l, *, zg_block, nb, name):
    n_seq, tt, dr = y3.shape
    lg = g2h.shape[0]
    return pl.pallas_call(
        _gate_body,
        grid=(n_seq // nb,),
        in_specs=[pl.BlockSpec((nb, tt, dr), lambda i: (i, 0, 0)),
                  pl.BlockSpec((nb, tt, lg), lambda i: (i, 0, zg_block)),
                  pl.BlockSpec((nb, 1, lg), lambda i: (i, 0, zg_block)),
                  pl.BlockSpec((1, lg), lambda i: (0, zg_block)),
                  pl.BlockSpec(g2h.shape, lambda i: (0, 0)),
                  pl.BlockSpec(g2l.shape, lambda i: (0, 0))],
        out_specs=pl.BlockSpec((nb, tt, dr), lambda i: (i, 0, 0)),
        out_shape=jax.ShapeDtypeStruct((n_seq, tt, dr), BF16),
        compiler_params=_cparams("parallel"),
        name=name,
    )(y3, z3, first, mu, g2h, g2l)


CONV_ROW_CHUNK = 32
CONV_LANE_CHUNK = 256


def _conv_body(u_ref, hist_ref, cw_ref, cb_ref, lw_ref, lb_ref, o_ref, full_ref, fs_ref, c_ref, *,
               hist_rows, taps, fresh):
    nb, tt, dc = u_ref.shape
    hin = hist_ref.shape[1]
    hist = hist_ref[...]
    if fresh:
        hist = jnp.where(pl.program_id(1) == 0, 0.0, hist)
    if hin < hist_rows:
        full_ref[:, :hist_rows - hin, :] = jnp.zeros((nb, hist_rows - hin, dc), F32)
    full_ref[:, hist_rows - hin:hist_rows, :] = hist
    full_ref[:, hist_rows:, :] = u_ref[...]
    lead = hist_rows - (taps - 1)
    span = hist_rows + tt - SUBLANES
    for s in range(1, SUBLANES):
        fs_ref[s - 1] = full_ref[:, s:s + span, :]
    rc = min(CONV_ROW_CHUNK, tt)
    lc = min(CONV_LANE_CHUNK, dc)

    def row_chunk(ri, carry):
        r0 = pl.multiple_of(ri * rc, rc)
        for b in range(nb):
            for c0 in range(0, dc, lc):
                acc = jnp.broadcast_to(cb_ref[:, c0:c0 + lc], (rc, lc))
                for j in range(taps):
                    s = (lead + j) % SUBLANES
                    rows = pl.ds(pl.multiple_of(r0 + (lead + j - s), SUBLANES), rc)
                    win = full_ref[b, rows, c0:c0 + lc] if s == 0 else fs_ref[s - 1, b, rows, c0:c0 + lc]
                    acc = acc + win * cw_ref[j:j + 1, c0:c0 + lc]
                c_ref[b, pl.ds(r0, rc), c0:c0 + lc] = acc
        return carry

    lax.fori_loop(0, tt // rc, row_chunk, 0)
    acc = c_ref[...]
    mu = jnp.mean(acc, axis=-1, keepdims=True)
    xc = acc - mu
    var = jnp.mean(xc * xc, axis=-1, keepdims=True)
    y = xc * lax.rsqrt(var + LN_EPS) * lw_ref[...] + lb_ref[...]
    o_ref[...] = (y * jax.nn.sigmoid(y)).astype(o_ref.dtype)


def _conv_module(u3, conv0, cw, cb, lnw, lnb, *, nb, tt, name):
    bsz, t, dc = u3.shape
    taps = cw.shape[0]
    hist_rows = _round_up(taps - 1, SUBLANES)
    fresh = conv0 is None
    if fresh:
        assert nb == 1 and tt % hist_rows == 0
        hist, hin = u3, hist_rows
        hist_map = lambda i, j: (i, jnp.maximum(j * (tt // hist_rows) - 1, 0), 0)
    else:
        assert tt == t
        hist, hin = conv0, conv0.shape[1]
        hist_map = lambda i, j: (i, 0, 0)
    row = pl.BlockSpec((1, dc), lambda i, j: (0, 0))
    return pl.pallas_call(
        functools.partial(_conv_body, hist_rows=hist_rows, taps=taps, fresh=fresh),
        grid=(bsz // nb, t // tt),
        in_specs=[pl.BlockSpec((nb, tt, dc), lambda i, j: (i, j, 0)),
                  pl.BlockSpec((nb, hin, dc), hist_map),
                  pl.BlockSpec((taps, dc), lambda i, j: (0, 0)), row, row, row],
        out_specs=pl.BlockSpec((nb, tt, dc), lambda i, j: (i, j, 0)),
        out_shape=jax.ShapeDtypeStruct((bsz, t, dc), BF16),
        scratch_shapes=[pltpu.VMEM((nb, hist_rows + tt, dc), F32),
                        pltpu.VMEM((SUBLANES - 1, nb, hist_rows + tt - SUBLANES, dc), F32),
                        pltpu.VMEM((nb, tt, dc), F32)],
        compiler_params=_cparams("parallel", "parallel"),
        name=name,
    )(u3, hist, cw, cb.reshape(1, dc), lnw.reshape(1, dc), lnb.reshape(1, dc))


def _split_hi_lo(w, rows):
    w = jnp.pad(w.astype(F32), ((0, rows - w.shape[0]), (0, 0)))
    hi = w.astype(BF16)
    return hi, (w - hi.astype(F32)).astype(BF16)


def _layer(h, pe, wkv0, shift0, conv0, lanes_are_batch_head, p):
    bsz, t, d = h.shape
    m = bsz * t
    dr, dc, n = p["dr"], p["dc"], HEAD_SIZE
    nh = dr // n
    cr = p["mu"].shape[1]
    hbuf = conv0.shape[1] if conv0 is not None else p["conv_w"].shape[0] - 1
    x2 = h.reshape(m, d)

    xn = _rmsnorm(x2, p["g_mix"], out_dtype=BF16, name="mix_norm")
    z = _matmul([xn], [p["w_in_r"]], **_tiles("in_proj_rwkv"))
    tg = _pick_tile(dc, TILES["in_proj_glu"][1])
    u = _matmul([xn], [p["w_in_c"], p["w_in_c"]], epi="glu", n_out=dc, w_blocks=[(0, 0), (0, dc // tg)],
                **_tiles("in_proj_glu"))
    new_shift = _rmsnorm(h[:, -1, :], p["g_mix"], name="shift_norm")

    if t > 256:
        tt = 256
        s_per = t // tt
        z3 = z.reshape(bsz * s_per, tt, cr)
        first = jnp.concatenate(
            [jnp.zeros((bsz, 1, cr), F32), z.reshape(bsz, t, cr)[:, tt - 1:t - 1:tt, :]], axis=1
        ).reshape(bsz * s_per, 1, cr)
        nb = 1
    else:
        tt, s_per, nb = t, 1, 32
        z3 = z.reshape(bsz, t, cr)
        z_first = _matmul([shift0.astype(BF16)], [p["w_in_r"]], **_tiles("shift_proj"))
        first = z_first.reshape(bsz, 1, cr)
    r, k, v, dec, a = _rwkv_prep(z3, first, p["mu"], p["w0"], p["w2h"], p["w2l"], p["a0"], p["a2h"], p["a2l"],
                                 dr=dr, nb=nb, name="rwkv_prep")

    if lanes_are_batch_head:
        to_scan = lambda x: _to_scan_bh(x.reshape(bsz, t, dr), nh=nh, name="to_scan")
        par = lambda q: jnp.tile(q.reshape(nh, n).T, (1, bsz))[None]
        s0 = jnp.zeros((1, n, n, LANES), F32)
    else:
        to_scan = lambda x: jnp.transpose(x.reshape(bsz, t, nh, n), (2, 1, 3, 0))
        par = lambda q: jnp.broadcast_to(q.reshape(nh, n)[:, :, None], (nh, n, LANES))
        s0 = jnp.transpose(wkv0, (1, 2, 3, 0))
    y_s, s_new = _wkv_scan(to_scan(r), to_scan(k), to_scan(v), to_scan(dec), to_scan(a), s0,
                           par(p["k_k"]), par(p["k_a"]), par(p["r_k"]), par(p["lnx_w"]), par(p["lnx_b"]),
                           tc=32, name="wkv_scan")
    if lanes_are_batch_head:
        y_tok = _from_scan_bh(y_s[0], bsz=bsz, nh=nh, name="from_scan")
        new_wkv = jnp.transpose(s_new.reshape(n, n, bsz, nh), (2, 3, 0, 1))
    else:
        y_tok = jnp.transpose(y_s, (3, 1, 0, 2))
        new_wkv = jnp.transpose(s_new, (3, 0, 1, 2))
    y3 = y_tok.reshape(bsz * s_per, tt, dr)
    y_rwkv = _rwkv_gate(y3, z3, first, p["mu"], p["g2h"], p["g2l"], zg_block=p["zg_block"], nb=nb,
                        name="rwkv_gate").reshape(m, dr)

    u3 = u.reshape(bsz, t, dc)
    if conv0 is None:
        new_conv = u3[:, t - hbuf:, :]
        cnb = 1
    else:
        new_conv = jnp.concatenate([conv0, u3], axis=1)[:, -hbuf:, :]
        cnb = 4
    y_conv = _conv_module(u3, conv0, p["conv_w"], p["conv_b"], p["conv_ln_w"], p["conv_ln_b"], nb=cnb, tt=tt,
                          name="conv_module").reshape(m, dc)

    assert dr == dc
    h1 = _matmul([y_rwkv, y_conv], [p["w_out"], p["w_out"]], w_blocks=[(0, 0), (1, 0)], res=x2, epi="res",
                 **_tiles("out_proj"))
    dff = p["w_down"].shape[0]
    tf = _pick_tile(dff, TILES["ffn_gate_up"][1])
    h1n = _rmsnorm(h1, p["g_ffn"], out_dtype=BF16, name="ffn_norm")
    act = _matmul([h1n], [p["w_gate_up"], p["w_gate_up"]], epi="swiglu", n_out=dff, out_dtype=BF16,
                  w_blocks=[(0, 0), (0, dff // tf)], **_tiles("ffn_gate_up"))
    h2 = _matmul([act], [p["w_down"]], res=h1, epi="res", **_tiles("ffn_down"))
    h2n = _rmsnorm(h2, p["g_ple"], out_dtype=BF16, name="ple_norm")
    h3 = _matmul([h2n, pe.reshape(m, -1)], [p["w_ple_gate"], p["w_ple_proj"]], res=h2, epi="ple", **_tiles("ple"))
    return h3.reshape(bsz, t, d), new_wkv, new_shift, new_conv


def _prep_params(i, g_mix, w_in, mu_shift, w0, w2, a0, a2, g2, k_k, k_a, r_k, lnx_w, lnx_b, conv_w, conv_b,
                 conv_ln_w, conv_ln_b, w_out, g_ffn, w_gate_up, w_down, g_ple, w_ple_gate, w_ple_proj):
    dr = w0.shape[-1]
    dc = conv_w.shape[-1]
    lw, la, lg = w2.shape[1], a2.shape[1], g2.shape[1]
    lwp, lap, lgp = (_round_up(x, LANES) for x in (lw, la, lg))

    def rwkv_cols(x):
        c = 3 * dr
        pad = lambda y, to: jnp.pad(y, [(0, 0)] * (y.ndim - 1) + [(0, to - y.shape[-1])])
        return jnp.concatenate([x[..., :c], pad(x[..., c:c + lw], lwp), pad(x[..., c + lw:c + lw + la], lap),
                                pad(x[..., c + lw + la:c + lw + la + lg], lgp)], axis=-1)

    rwkv_cols_n = 3 * dr + lw + la + lg
    wi = w_in[i]
    w2h, w2l = _split_hi_lo(w2[i], lwp)
    a2h, a2l = _split_hi_lo(a2[i], lap)
    g2h, g2l = _split_hi_lo(g2[i], lgp)
    assert (3 * dr + lwp + lap) % lgp == 0
    return dict(
        dr=dr, dc=dc,
        g_mix=g_mix[i], w_in_r=rwkv_cols(wi[:, :rwkv_cols_n]).astype(BF16), w_in_c=wi[:, rwkv_cols_n:].astype(BF16),
        mu=rwkv_cols(mu_shift[i][None, :]), zg_block=(3 * dr + lwp + lap) // lgp,
        w0=w0[i][None, :], w2h=w2h, w2l=w2l, a0=a0[i][None, :], a2h=a2h, a2l=a2l, g2h=g2h, g2l=g2l,
        k_k=k_k[i], k_a=k_a[i], r_k=r_k[i], lnx_w=lnx_w[i], lnx_b=lnx_b[i],
        conv_w=conv_w[i], conv_b=conv_b[i], conv_ln_w=conv_ln_w[i], conv_ln_b=conv_ln_b[i],
        w_out=w_out[i].astype(BF16), g_ffn=g_ffn[i], w_gate_up=w_gate_up[i], w_down=w_down[i].astype(BF16),
        g_ple=g_ple[i], w_ple_gate=w_ple_gate[i].astype(BF16), w_ple_proj=w_ple_proj[i].astype(BF16),
    )


def kernel(x_prompt, x_sample, state_wkv, state_shift, state_conv, p_prompt, p_sample, g_mix, w_in, mu_shift, w0, w2, a0, a2, g2, k_k, k_a, r_k, lnx_w, lnx_b, conv_w, conv_b, conv_ln_w, conv_ln_b, w_out, g_ffn, w_gate_up, w_down, g_ple, w_ple_gate, w_ple_proj, g_final):
    depth = w_in.shape[0]
    layer_params = (g_mix, w_in, mu_shift, w0, w2, a0, a2, g2, k_k, k_a, r_k, lnx_w, lnx_b, conv_w, conv_b,
                    conv_ln_w, conv_ln_b, w_out, g_ffn, w_gate_up, w_down, g_ple, w_ple_gate, w_ple_proj)
    hp, hs = x_prompt, x_sample
    outs = [[] for _ in range(6)]
    for i in range(depth):
        p = _prep_params(i, *layer_params)
        hp, s_w, s_sh, s_c = _layer(hp, p_prompt[i], None, None, None, True, p)
        outs[0].append(s_w); outs[1].append(s_sh); outs[2].append(s_c)
        hs, s_w, s_sh, s_c = _layer(hs, p_sample[i], state_wkv[i], state_shift[i], state_conv[i], False, p)
        outs[3].append(s_w); outs[4].append(s_sh); outs[5].append(s_c)
    d = hp.shape[-1]
    y_prompt = _rmsnorm(hp.reshape(-1, d), g_final, name="final_norm").reshape(hp.shape)
    y_sample = _rmsnorm(hs.reshape(-1, d), g_final, name="final_norm").reshape(hs.shape)
    return (y_prompt, y_sample) + tuple(jnp.stack(o) for o in outs)
```

```python
import functools

import jax
import jax.numpy as jnp
from jax import lax
from jax.experimental import pallas as pl
from jax.experimental.pallas import tpu as pltpu

HEAD_SIZE = 64
RMS_EPS = 1e-6
LN_EPS = 1e-5
GN_EPS = 64e-5
KK_EPS = 1e-12
LANES = 128
SUBLANES = 8
VMEM_LIMIT_BYTES = 56 * 2 ** 20

F32 = jnp.float32
BF16 = jnp.bfloat16


TILES = dict(in_proj_rwkv=(1024, 512, 2), shift_proj=(128, 512, 2), in_proj_glu=(1024, 512, 2),
             out_proj=(1024, 512, 2), ffn_gate_up=(2048, 256, 1), ffn_down=(512, 512, 2), ple=(1024, 512, 2))


def _tiles(name):
    tm, tn, x_buffers = TILES[name]
    return dict(tm=tm, tn=tn, x_buffers=x_buffers, name=name)


def _cparams(*sem):
    return pltpu.CompilerParams(dimension_semantics=sem, vmem_limit_bytes=VMEM_LIMIT_BYTES)


def _round_up(n, m):
    return (n + m - 1) // m * m


def _pick_tile(n, pref):
    if n <= pref:
        return n
    t = pref - pref % LANES
    while t > LANES and n % t:
        t -= LANES
    assert n % t == 0, (n, pref)
    return t


def _mm_body(*refs, n_x, n_w, epi):
    refs = list(refs)
    x_refs = [refs.pop(0) for _ in range(n_x)]
    w_refs = [refs.pop(0) for _ in range(n_w)]
    res_ref = refs.pop(0) if epi in ("res", "ple") else None
    out_ref = refs.pop(0)

    def dot(x_ref, w_ref):
        return jnp.dot(x_ref[...].astype(BF16), w_ref[...].astype(BF16), preferred_element_type=F32)

    if epi in ("store", "res"):
        acc = dot(x_refs[0], w_refs[0])
        for xr, wr in zip(x_refs[1:], w_refs[1:]):
            acc = acc + dot(xr, wr)
        if epi == "res":
            acc = res_ref[...] + acc
        out_ref[...] = acc.astype(out_ref.dtype)
    elif epi == "glu":
        out_ref[...] = (dot(x_refs[0], w_refs[0]) * jax.nn.sigmoid(dot(x_refs[0], w_refs[1]))).astype(out_ref.dtype)
    elif epi == "swiglu":
        gate = dot(x_refs[0], w_refs[0])
        out_ref[...] = (gate * jax.nn.sigmoid(gate) * dot(x_refs[0], w_refs[1])).astype(out_ref.dtype)
    elif epi == "ple":
        pg = jax.nn.sigmoid(dot(x_refs[0], w_refs[0]))
        out_ref[...] = (res_ref[...] + pg * dot(x_refs[1], w_refs[1])).astype(out_ref.dtype)
    else:
        raise ValueError(epi)


def _matmul(xs, ws, *, res=None, epi="store", out_dtype=F32, n_out=None, w_blocks=None, tm, tn, x_buffers, name):
    m = xs[0].shape[0]
    n_out = ws[0].shape[1] if n_out is None else n_out
    tm = _pick_tile(m, tm)
    tn = _pick_tile(n_out, tn)
    w_blocks = [(0, 0)] * len(ws) if w_blocks is None else w_blocks
    in_specs = [pl.BlockSpec((tm, x.shape[1]), lambda i, j: (i, 0), pipeline_mode=pl.Buffered(x_buffers))
                for x in xs]
    for idx, (rb, cb) in enumerate(w_blocks):
        k = xs[min(idx, len(xs) - 1)].shape[1]
        in_specs.append(pl.BlockSpec((k, tn), functools.partial(lambda i, j, rb, cb: (rb, j + cb), rb=rb, cb=cb)))
    if res is not None:
        in_specs.append(pl.BlockSpec((tm, tn), lambda i, j: (i, j)))
    return pl.pallas_call(
        functools.partial(_mm_body, n_x=len(xs), n_w=len(ws), epi=epi),
        grid=(m // tm, n_out // tn),
        in_specs=in_specs,
        out_specs=pl.BlockSpec((tm, tn), lambda i, j: (i, j)),
        out_shape=jax.ShapeDtypeStruct((m, n_out), out_dtype),
        compiler_params=_cparams("parallel", "arbitrary"),
        name=name,
    )(*xs, *ws, *([] if res is None else [res]))


def _rms_body(x_ref, g_ref, o_ref):
    x = x_ref[...].astype(F32)
    ms = jnp.mean(x * x, axis=-1, keepdims=True)
    o_ref[...] = (x * lax.rsqrt(ms + RMS_EPS) * g_ref[...]).astype(o_ref.dtype)


def _rmsnorm(x, gain, *, out_dtype=F32, tm=512, name="rmsnorm"):
    m, d = x.shape
    tm = _pick_tile(m, tm)
    return pl.pallas_call(
        _rms_body,
        grid=(m // tm,),
        in_specs=[pl.BlockSpec((tm, d), lambda i: (i, 0)), pl.BlockSpec((1, d), lambda i: (0, 0))],
        out_specs=pl.BlockSpec((tm, d), lambda i: (i, 0)),
        out_shape=jax.ShapeDtypeStruct((m, d), out_dtype),
        compiler_params=_cparams("parallel"),
        name=name,
    )(x, gain.reshape(1, d).astype(F32))


def _split3(x):
    hi = x.astype(BF16)
    r1 = x - hi.astype(F32)
    mid = r1.astype(BF16)
    lo = (r1 - mid.astype(F32)).astype(BF16)
    return hi, mid, lo


def _dot_f32(a, b_hi, b_lo):
    a_hi, a_mid, _ = _split3(a)
    dot = lambda p, q: jnp.dot(p, q, preferred_element_type=F32)
    return dot(a_hi, b_hi) + (dot(a_mid, b_hi) + dot(a_hi, b_lo))


def _prep_body(z_ref, first_ref, mu_ref, w0_ref, w2h_ref, w2l_ref, a0_ref, a2h_ref, a2l_ref, *out_refs,
               dr, lw, la, with_rkv):
    nb, tt, _ = z_ref.shape
    tpos = lax.broadcasted_iota(jnp.int32, (nb, tt, 1), 1)

    def mix(lo, width):
        z = z_ref[:, :, lo:lo + width]
        zp = jnp.where(tpos == 0, first_ref[:, :, lo:lo + width], pltpu.roll(z, 1, axis=1))
        return z + (zp - z) * mu_ref[:, lo:lo + width]

    base = 0
    if with_rkv:
        for c, o_ref in enumerate(out_refs[:3]):
            o_ref[...] = mix(c * dr, dr)
        base = 3 * dr
    d_ref, a_ref = out_refs[-2:]
    zw = mix(base, lw).reshape(nb * tt, lw)
    za = mix(base + lw, la).reshape(nb * tt, la)
    xw = w0_ref[...] + _dot_f32(jnp.tanh(zw), w2h_ref[...], w2l_ref[...])
    w_log = jnp.minimum(xw, 0.0) - jnp.log1p(jnp.exp(-jnp.abs(xw))) - 0.5
    d_ref[...] = jnp.exp(-jnp.exp(w_log)).reshape(nb, tt, dr)
    xa = a0_ref[...] + _dot_f32(za, a2h_ref[...], a2l_ref[...])
    a_ref[...] = jax.nn.sigmoid(xa).reshape(nb, tt, dr)


def _rwkv_prep(z3, first, mu, w0, w2h, w2l, a0, a2h, a2l, *, dr, nb, with_rkv, name):
    n_seq, tt, cr = z3.shape
    lw, la = w2h.shape[0], a2h.shape[0]
    if with_rkv:
        width, cb = cr, 0
    else:
        width = cr - 3 * dr
        assert (3 * dr) % width == 0
        cb = 3 * dr // width
    row = lambda c: pl.BlockSpec((1, c), lambda i: (0, 0))
    full = lambda a: pl.BlockSpec(a.shape, lambda i: (0, 0))
    n_out = 5 if with_rkv else 2
    return pl.pallas_call(
        functools.partial(_prep_body, dr=dr, lw=lw, la=la, with_rkv=with_rkv),
        grid=(n_seq // nb,),
        in_specs=[pl.BlockSpec((nb, tt, width), lambda i: (i, 0, cb)),
                  pl.BlockSpec((nb, 1, width), lambda i: (i, 0, cb)),
                  pl.BlockSpec((1, width), lambda i: (0, cb)),
                  row(dr), full(w2h), full(w2l), row(dr), full(a2h), full(a2l)],
        out_specs=[pl.BlockSpec((nb, tt, dr), lambda i: (i, 0, 0))] * n_out,
        out_shape=[jax.ShapeDtypeStruct((n_seq, tt, dr), F32)] * n_out,
        compiler_params=_cparams("parallel"),
        name=name,
    )(z3, first, mu, w0, w2h, w2l, a0, a2h, a2l)


RELAYOUT_ROWS = 128
SCAN_PITCH = HEAD_SIZE + SUBLANES


def _to_scan_body(*refs, nh, bsz, mix):
    x_refs, refs = refs[:bsz], refs[bsz:]
    if mix:
        first_refs, mu_ref, refs = refs[:bsz], refs[bsz], refs[bsz + 1:]
    o_ref, xt_ref = refs
    n, tq = HEAD_SIZE, RELAYOUT_ROWS
    for b in range(bsz):
        x = x_refs[b][0]
        if mix:
            row = lax.broadcasted_iota(jnp.int32, (tq, 1), 0)
            xp = jnp.where(row == 0, first_refs[b][0, 0], pltpu.roll(x, 1, axis=0))
            x = x + (xp - x) * mu_ref[...]
        xt = x.T
        for h in range(nh):
            xt_ref[b, h * SCAN_PITCH:h * SCAN_PITCH + n, :] = xt[h * n:(h + 1) * n, :]
    for j in range(n):
        m = jnp.concatenate([xt_ref[b, pl.ds(j, nh, stride=SCAN_PITCH), :] for b in range(bsz)], axis=0)
        o_ref[pl.ds(j, tq, stride=SCAN_PITCH), :] = m.T
    for j in range(n, SCAN_PITCH):
        o_ref[pl.ds(j, tq, stride=SCAN_PITCH), :] = jnp.zeros((tq, LANES), F32)


def _to_scan_bh(x3, *, nh, name, col_block=0, mix=None):
    bsz, t, _ = x3.shape
    d = nh * HEAD_SIZE
    assert bsz * nh == LANES and t % RELAYOUT_ROWS == 0
    bspec = lambda shape, b: pl.BlockSpec(shape, functools.partial(lambda i, b: (b, i) + (0,) * (len(shape) - 3)
                                                                  + (col_block,), b=b))
    in_specs = [bspec((1, RELAYOUT_ROWS, d), b) for b in range(bsz)]
    args = [x3] * bsz
    if mix is not None:
        first, mu = mix
        in_specs += [bspec((1, 1, 1, d), b) for b in range(bsz)] + [pl.BlockSpec((1, d), lambda i: (0, col_block))]
        args += [first] * bsz + [mu]
    out = pl.pallas_call(
        functools.partial(_to_scan_body, nh=nh, bsz=bsz, mix=mix is not None),
        grid=(t // RELAYOUT_ROWS,),
        in_specs=in_specs,
        out_specs=pl.BlockSpec((RELAYOUT_ROWS * SCAN_PITCH, LANES), lambda i: (i, 0)),
        out_shape=jax.ShapeDtypeStruct((t * SCAN_PITCH, LANES), F32),
        scratch_shapes=[pltpu.VMEM((bsz, nh * SCAN_PITCH, RELAYOUT_ROWS), F32)],
        compiler_params=_cparams("parallel"),
        name=name,
    )(*args)
    return out.reshape(1, t, SCAN_PITCH, LANES)


def _from_scan_body(*refs, nh, bsz):
    y_ref, o_ref, yt_ref = refs
    n, tq = HEAD_SIZE, RELAYOUT_ROWS
    for j in range(n):
        mt = y_ref[pl.ds(j, tq, stride=n), :].T
        for b in range(bsz):
            yt_ref[b, pl.ds(j, nh, stride=SCAN_PITCH), :] = mt[b * nh:(b + 1) * nh, :]
    for b in range(bsz):
        yt = jnp.concatenate([yt_ref[b, h * SCAN_PITCH:h * SCAN_PITCH + n, :] for h in range(nh)], axis=0)
        o_ref[b] = yt.T


def _from_scan_bh(y, *, bsz, nh, name):
    t, n, _ = y.shape
    d = nh * n
    assert bsz * nh == LANES and n == HEAD_SIZE and t % RELAYOUT_ROWS == 0
    return pl.pallas_call(
        functools.partial(_from_scan_body, nh=nh, bsz=bsz),
        grid=(t // RELAYOUT_ROWS,),
        in_specs=[pl.BlockSpec((RELAYOUT_ROWS * n, LANES), lambda i: (i, 0))],
        out_specs=pl.BlockSpec((bsz, RELAYOUT_ROWS, d), lambda i: (0, i, 0)),
        out_shape=jax.ShapeDtypeStruct((bsz, t, d), F32),
        scratch_shapes=[pltpu.VMEM((bsz, nh * SCAN_PITCH, RELAYOUT_ROWS), F32)],
        compiler_params=_cparams("parallel"),
        name=name,
    )(y.reshape(t * n, LANES))


def _scan_body(*refs, fresh):
    r_ref, k_ref, v_ref, d_ref, a_ref = refs[:5]
    s0_ref = None if fresh else refs[5]
    kk_ref, ka_ref, rk_ref, lw_ref, lb_ref, y_ref, so_ref, s_ref, yrow_ref = refs[5 if fresh else 6:]
    n = HEAD_SIZE
    tc = r_ref.shape[0]

    @pl.when(pl.program_id(1) == 0)
    def _():
        if fresh:
            s_ref[...] = jnp.zeros(s_ref.shape, F32)
        else:
            s_ref[...] = s0_ref[...].T.reshape(n, n, LANES)

    def colsum(x):
        return jnp.sum(x, axis=0, keepdims=True)

    def step(t, gamma):
        r, k, v, w, a = (x[t, :n, :] for x in (r_ref, k_ref, v_ref, d_ref, a_ref))
        kk = k * kk_ref[...]
        kk = kk / jnp.maximum(jnp.sqrt(colsum(kk * kk)), KK_EPS)
        k2 = k * (1.0 + (a - 1.0) * ka_ref[...])
        kk_n = kk * gamma
        gamma = gamma * w
        inv = 1.0 / gamma
        b_n = kk * a * inv
        k_n = k2 * inv
        r_n = r * gamma
        for i in range(n):
            s_i = s_ref[i]
            s_kk = colsum(s_i * kk_n)
            s_i = s_i - s_kk * b_n + v_ref[t, i:i + 1, :] * k_n
            s_ref[i] = s_i
            yrow_ref[i:i + 1, :] = colsum(s_i * r_n)
        y = yrow_ref[...]
        ym = jnp.mean(y, axis=0, keepdims=True)
        yc = y - ym
        yv = jnp.mean(yc * yc, axis=0, keepdims=True)
        yn = yc * lax.rsqrt(yv + GN_EPS) * lw_ref[...] + lb_ref[...]
        bonus = colsum(r * k2 * rk_ref[...]) * v
        y_ref[t] = yn + bonus
        return gamma

    gamma = lax.fori_loop(0, tc, step, jnp.ones((n, LANES), F32))
    s_ref[...] = s_ref[...] * gamma

    @pl.when(pl.program_id(1) == pl.num_programs(1) - 1)
    def _():
        so_ref[...] = s_ref[...].reshape(n * n, LANES).T


def _wkv_scan(r, k, v, d, a, s0, kkp, kap, rkp, lnw, lnb, *, tc, name):
    g, t, pitch, _ = r.shape
    n = HEAD_SIZE
    tc = min(tc, t)
    seq_in = pl.BlockSpec((None, tc, pitch, LANES), lambda gi, ci: (gi, ci, 0, 0))
    seq = pl.BlockSpec((None, tc, n, LANES), lambda gi, ci: (gi, ci, 0, 0))
    par = pl.BlockSpec((None, n, LANES), lambda gi, ci: (gi, 0, 0))
    st = pl.BlockSpec((LANES, n * n), lambda gi, ci: (0, gi))
    fresh = s0 is None
    return pl.pallas_call(
        functools.partial(_scan_body, fresh=fresh),
        grid=(g, t // tc),
        in_specs=[seq_in] * 5 + ([] if fresh else [st]) + [par] * 5,
        out_specs=[seq, st],
        out_shape=[jax.ShapeDtypeStruct((g, t, n, LANES), F32),
                   jax.ShapeDtypeStruct((LANES, g * n * n), F32)],
        scratch_shapes=[pltpu.VMEM((n, n, LANES), F32), pltpu.VMEM((n, LANES), F32)],
        compiler_params=_cparams("parallel", "arbitrary"),
        name=name,
    )(r, k, v, d, a, *([] if fresh else [s0]), kkp, kap, rkp, lnw, lnb)


def _gate_body(y_ref, zg_ref, first_ref, mu_ref, g2h_ref, g2l_ref, o_ref):
    nb, tt, lg = zg_ref.shape
    tpos = lax.broadcasted_iota(jnp.int32, (nb, tt, 1), 1)
    z = zg_ref[...]
    zp = jnp.where(tpos == 0, first_ref[...], pltpu.roll(z, 1, axis=1))
    zg = (z + (zp - z) * mu_ref[...]).reshape(nb * tt, lg)
    g = _dot_f32(jax.nn.sigmoid(zg), g2h_ref[...], g2l_ref[...])
    o_ref[...] = (y_ref[...].reshape(nb * tt, -1) * g).reshape(o_ref.shape).astype(o_ref.dtype)


def _rwkv_gate(y3, z3, first, mu, g2h, g2l, *, zg_block, nb, name):
    n_seq, tt, dr = y3.shape
    lg = g2h.shape[0]
    return pl.pallas_call(
        _gate_body,
        grid=(n_seq // nb,),
        in_specs=[pl.BlockSpec((nb, tt, dr), lambda i: (i, 0, 0)),
                  pl.BlockSpec((nb, tt, lg), lambda i: (i, 0, zg_block)),
                  pl.BlockSpec((nb, 1, lg), lambda i: (i, 0, zg_block)),
                  pl.BlockSpec((1, lg), lambda i: (0, zg_block)),
                  pl.BlockSpec(g2h.shape, lambda i: (0, 0)),
                  pl.BlockSpec(g2l.shape, lambda i: (0, 0))],
        out_specs=pl.BlockSpec((nb, tt, dr), lambda i: (i, 0, 0)),
        out_shape=jax.ShapeDtypeStruct((n_seq, tt, dr), BF16),
        compiler_params=_cparams("parallel"),
        name=name,
    )(y3, z3, first, mu, g2h, g2l)


CONV_ROW_CHUNK = 32
CONV_LANE_CHUNK = 512


def _conv_body(u_ref, hist_ref, cw_ref, cb_ref, lw_ref, lb_ref, o_ref, full_ref, fs_ref, c_ref, *,
               hist_rows, taps, fresh):
    nb, tt, dc = u_ref.shape
    hin = hist_ref.shape[1]
    hist = hist_ref[...]
    if fresh:
        hist = jnp.where(pl.program_id(1) == 0, 0.0, hist)
    if hin < hist_rows:
        full_ref[:, :hist_rows - hin, :] = jnp.zeros((nb, hist_rows - hin, dc), F32)
    full_ref[:, hist_rows - hin:hist_rows, :] = hist
    full_ref[:, hist_rows:, :] = u_ref[...]
    lead = hist_rows - (taps - 1)
    span = hist_rows + tt - SUBLANES
    for s in range(1, SUBLANES):
        fs_ref[s - 1] = full_ref[:, s:s + span, :]
    rc = min(CONV_ROW_CHUNK, tt)
    lc = min(CONV_LANE_CHUNK, dc)

    def row_chunk(ri, carry, *, b, c0):
        r0 = pl.multiple_of(ri * rc, rc)
        acc = jnp.broadcast_to(cb_ref[:, c0:c0 + lc], (rc, lc))
        for j in range(taps):
            s = (lead + j) % SUBLANES
            rows = pl.ds(pl.multiple_of(r0 + (lead + j - s), SUBLANES), rc)
            win = full_ref[b, rows, c0:c0 + lc] if s == 0 else fs_ref[s - 1, b, rows, c0:c0 + lc]
            acc = acc + win * cw_ref[j:j + 1, c0:c0 + lc]
        c_ref[b, pl.ds(r0, rc), c0:c0 + lc] = acc
        return carry

    for b in range(nb):
        for c0 in range(0, dc, lc):
            lax.fori_loop(0, tt // rc, functools.partial(row_chunk, b=b, c0=c0), 0)
    acc = c_ref[...]
    mu = jnp.mean(acc, axis=-1, keepdims=True)
    xc = acc - mu
    var = jnp.mean(xc * xc, axis=-1, keepdims=True)
    y = xc * lax.rsqrt(var + LN_EPS) * lw_ref[...] + lb_ref[...]
    o_ref[...] = (y * jax.nn.sigmoid(y)).astype(o_ref.dtype)


def _conv_module(u3, conv0, cw, cb, lnw, lnb, *, nb, tt, name):
    bsz, t, dc = u3.shape
    taps = cw.shape[0]
    hist_rows = _round_up(taps - 1, SUBLANES)
    fresh = conv0 is None
    if fresh:
        assert nb == 1 and tt % hist_rows == 0
        hist, hin = u3, hist_rows
        hist_map = lambda i, j: (i, jnp.maximum(j * (tt // hist_rows) - 1, 0), 0)
    else:
        assert tt == t
        hist, hin = conv0, conv0.shape[1]
        hist_map = lambda i, j: (i, 0, 0)
    row = pl.BlockSpec((1, dc), lambda i, j: (0, 0))
    return pl.pallas_call(
        functools.partial(_conv_body, hist_rows=hist_rows, taps=taps, fresh=fresh),
        grid=(bsz // nb, t // tt),
        in_specs=[pl.BlockSpec((nb, tt, dc), lambda i, j: (i, j, 0)),
                  pl.BlockSpec((nb, hin, dc), hist_map),
                  pl.BlockSpec((taps, dc), lambda i, j: (0, 0)), row, row, row],
        out_specs=pl.BlockSpec((nb, tt, dc), lambda i, j: (i, j, 0)),
        out_shape=jax.ShapeDtypeStruct((bsz, t, dc), BF16),
        scratch_shapes=[pltpu.VMEM((nb, hist_rows + tt, dc), F32),
                        pltpu.VMEM((SUBLANES - 1, nb, hist_rows + tt - SUBLANES, dc), F32),
                        pltpu.VMEM((nb, tt, dc), F32)],
        compiler_params=_cparams("parallel", "parallel"),
        name=name,
    )(u3, hist, cw, cb.reshape(1, dc), lnw.reshape(1, dc), lnb.reshape(1, dc))


def _split_hi_lo(w, rows):
    w = jnp.pad(w.astype(F32), ((0, rows - w.shape[0]), (0, 0)))
    hi = w.astype(BF16)
    return hi, (w - hi.astype(F32)).astype(BF16)


def _layer(h, pe, wkv0, shift0, conv0, lanes_are_batch_head, p):
    bsz, t, d = h.shape
    m = bsz * t
    dr, dc, n = p["dr"], p["dc"], HEAD_SIZE
    nh = dr // n
    cr = p["mu"].shape[1]
    hbuf = conv0.shape[1] if conv0 is not None else p["conv_w"].shape[0] - 1
    x2 = h.reshape(m, d)

    xn = _rmsnorm(x2, p["g_mix"], out_dtype=BF16, name="mix_norm")
    z = _matmul([xn], [p["w_in_r"]], **_tiles("in_proj_rwkv"))
    tg = _pick_tile(dc, TILES["in_proj_glu"][1])
    u = _matmul([xn], [p["w_in_c"], p["w_in_c"]], epi="glu", n_out=dc, w_blocks=[(0, 0), (0, dc // tg)],
                **_tiles("in_proj_glu"))
    new_shift = _rmsnorm(h[:, -1, :], p["g_mix"], name="shift_norm")

    if t > 256:
        tt = 256
        s_per = t // tt
        z3 = z.reshape(bsz * s_per, tt, cr)
        first = jnp.concatenate(
            [jnp.zeros((bsz, 1, cr), F32), z.reshape(bsz, t, cr)[:, tt - 1:t - 1:tt, :]], axis=1
        ).reshape(bsz * s_per, 1, cr)
        nb = 1
    else:
        tt, s_per, nb = t, 1, 32
        z3 = z.reshape(bsz, t, cr)
        z_first = _matmul([shift0.astype(BF16)], [p["w_in_r"]], **_tiles("shift_proj"))
        first = z_first.reshape(bsz, 1, cr)
    prep = functools.partial(_rwkv_prep, z3, first, p["mu"], p["w0"], p["w2h"], p["w2l"], p["a0"], p["a2h"], p["a2l"],
                             dr=dr, nb=nb, name="rwkv_prep")

    if lanes_are_batch_head:
        dec, a = prep(with_rkv=False)
        zb = z.reshape(bsz, t, cr)
        first_q = jnp.concatenate([jnp.zeros((bsz, 1, cr), F32), zb[:, RELAYOUT_ROWS - 1:t - 1:RELAYOUT_ROWS, :]],
                                  axis=1)[:, :, None, :]
        scan_in = [_to_scan_bh(zb, nh=nh, col_block=c, mix=(first_q, p["mu"]), name="mix_to_scan") for c in range(3)]
        scan_in += [_to_scan_bh(x.reshape(bsz, t, dr), nh=nh, name="to_scan") for x in (dec, a)]
        par = lambda q: jnp.tile(q.reshape(nh, n).T, (1, bsz))[None]
        s0 = None
    else:
        to_scan = lambda x: jnp.transpose(x.reshape(bsz, t, nh, n), (2, 1, 3, 0))
        scan_in = [to_scan(x) for x in prep(with_rkv=True)]
        par = lambda q: jnp.broadcast_to(q.reshape(nh, n)[:, :, None], (nh, n, LANES))
        s0 = wkv0.reshape(bsz, nh * n * n)
    y_s, s_new = _wkv_scan(*scan_in, s0, par(p["k_k"]), par(p["k_a"]), par(p["r_k"]), par(p["lnx_w"]),
                           par(p["lnx_b"]), tc=32, name="wkv_scan")
    new_wkv = s_new.reshape(-1, nh, n, n)
    if lanes_are_batch_head:
        y_tok = _from_scan_bh(y_s[0], bsz=bsz, nh=nh, name="from_scan")
    else:
        y_tok = jnp.transpose(y_s, (3, 1, 0, 2))
    y3 = y_tok.reshape(bsz * s_per, tt, dr)
    y_rwkv = _rwkv_gate(y3, z3, first, p["mu"], p["g2h"], p["g2l"], zg_block=p["zg_block"], nb=nb,
                        name="rwkv_gate").reshape(m, dr)

    u3 = u.reshape(bsz, t, dc)
    if conv0 is None:
        new_conv = u3[:, t - hbuf:, :]
        cnb = 1
    else:
        new_conv = jnp.concatenate([conv0, u3], axis=1)[:, -hbuf:, :]
        cnb = 4
    y_conv = _conv_module(u3, conv0, p["conv_w"], p["conv_b"], p["conv_ln_w"], p["conv_ln_b"], nb=cnb, tt=tt,
                          name="conv_module").reshape(m, dc)

    assert dr == dc
    h1 = _matmul([y_rwkv, y_conv], [p["w_out"], p["w_out"]], w_blocks=[(0, 0), (1, 0)], res=x2, epi="res",
                 **_tiles("out_proj"))
    dff = p["w_down"].shape[0]
    tf = _pick_tile(dff, TILES["ffn_gate_up"][1])
    h1n = _rmsnorm(h1, p["g_ffn"], out_dtype=BF16, name="ffn_norm")
    act = _matmul([h1n], [p["w_gate_up"], p["w_gate_up"]], epi="swiglu", n_out=dff, out_dtype=BF16,
                  w_blocks=[(0, 0), (0, dff // tf)], **_tiles("ffn_gate_up"))
    h2 = _matmul([act], [p["w_down"]], res=h1, epi="res", **_tiles("ffn_down"))
    h2n = _rmsnorm(h2, p["g_ple"], out_dtype=BF16, name="ple_norm")
    h3 = _matmul([h2n, pe.reshape(m, -1)], [p["w_ple_gate"], p["w_ple_proj"]], res=h2, epi="ple", **_tiles("ple"))
    return h3.reshape(bsz, t, d), new_wkv, new_shift, new_conv


def _prep_params(i, g_mix, w_in, mu_shift, w0, w2, a0, a2, g2, k_k, k_a, r_k, lnx_w, lnx_b, conv_w, conv_b,
                 conv_ln_w, conv_ln_b, w_out, g_ffn, w_gate_up, w_down, g_ple, w_ple_gate, w_ple_proj):
    dr = w0.shape[-1]
    dc = conv_w.shape[-1]
    lw, la, lg = w2.shape[1], a2.shape[1], g2.shape[1]
    lwp, lap, lgp = (_round_up(x, LANES) for x in (lw, la, lg))

    def rwkv_cols(x):
        c = 3 * dr
        pad = lambda y, to: jnp.pad(y, [(0, 0)] * (y.ndim - 1) + [(0, to - y.shape[-1])])
        return jnp.concatenate([x[..., :c], pad(x[..., c:c + lw], lwp), pad(x[..., c + lw:c + lw + la], lap),
                                pad(x[..., c + lw + la:c + lw + la + lg], lgp)], axis=-1)

    rwkv_cols_n = 3 * dr + lw + la + lg
    wi = w_in[i]
    w2h, w2l = _split_hi_lo(w2[i], lwp)
    a2h, a2l = _split_hi_lo(a2[i], lap)
    g2h, g2l = _split_hi_lo(g2[i], lgp)
    assert (3 * dr + lwp + lap) % lgp == 0
    return dict(
        dr=dr, dc=dc,
        g_mix=g_mix[i], w_in_r=rwkv_cols(wi[:, :rwkv_cols_n]).astype(BF16),
        w_in_c=jnp.concatenate([wi[:, rwkv_cols_n:rwkv_cols_n + dc], wi[:, rwkv_cols_n + dc:]], axis=1).astype(BF16),
        mu=rwkv_cols(mu_shift[i][None, :]), zg_block=(3 * dr + lwp + lap) // lgp,
        w0=w0[i][None, :], w2h=w2h, w2l=w2l, a0=a0[i][None, :], a2h=a2h, a2l=a2l, g2h=g2h, g2l=g2l,
        k_k=k_k[i], k_a=k_a[i], r_k=r_k[i], lnx_w=lnx_w[i], lnx_b=lnx_b[i],
        conv_w=conv_w[i], conv_b=conv_b[i], conv_ln_w=conv_ln_w[i], conv_ln_b=conv_ln_b[i],
        w_out=w_out[i].astype(BF16), g_ffn=g_ffn[i], w_gate_up=w_gate_up[i], w_down=w_down[i].astype(BF16),
        g_ple=g_ple[i], w_ple_gate=w_ple_gate[i].astype(BF16), w_ple_proj=w_ple_proj[i].astype(BF16),
    )


def kernel(x_prompt, x_sample, state_wkv, state_shift, state_conv, p_prompt, p_sample, g_mix, w_in, mu_shift, w0, w2, a0, a2, g2, k_k, k_a, r_k, lnx_w, lnx_b, conv_w, conv_b, conv_ln_w, conv_ln_b, w_out, g_ffn, w_gate_up, w_down, g_ple, w_ple_gate, w_ple_proj, g_final):
    depth = w_in.shape[0]
    layer_params = (g_mix, w_in, mu_shift, w0, w2, a0, a2, g2, k_k, k_a, r_k, lnx_w, lnx_b, conv_w, conv_b,
                    conv_ln_w, conv_ln_b, w_out, g_ffn, w_gate_up, w_down, g_ple, w_ple_gate, w_ple_proj)
    hp, hs = x_prompt, x_sample
    outs = [[] for _ in range(6)]
    for i in range(depth):
        p = _prep_params(i, *layer_params)
        hp, s_w, s_sh, s_c = _layer(hp, p_prompt[i], None, None, None, True, p)
        outs[0].append(s_w); outs[1].append(s_sh); outs[2].append(s_c)
        hs, s_w, s_sh, s_c = _layer(hs, p_sample[i], state_wkv[i], state_shift[i], state_conv[i], False, p)
        outs[3].append(s_w); outs[4].append(s_sh); outs[5].append(s_c)
    d = hp.shape[-1]
    y_prompt = _rmsnorm(hp.reshape(-1, d), g_final, name="final_norm").reshape(hp.shape)
    y_sample = _rmsnorm(hs.reshape(-1, d), g_final, name="final_norm").reshape(hs.shape)
    return (y_prompt, y_sample) + tuple(jnp.stack(o) for o in outs)
```

```python
import functools

import jax
import jax.numpy as jnp
from jax import lax
from jax.experimental import pallas as pl
from jax.experimental.pallas import tpu as pltpu

HEAD_SIZE = 64
RMS_EPS = 1e-6
LN_EPS = 1e-5
GN_EPS = 64e-5
KK_EPS = 1e-12
LANES = 128
SUBLANES = 8
VMEM_LIMIT_BYTES = 56 * 2 ** 20

F32 = jnp.float32
BF16 = jnp.bfloat16


TILES = dict(in_proj_rkv=(2048, 512, 1), in_proj_lora=(1024, 512, 2), shift_proj=(128, 512, 2),
             shift_lora=(128, 512, 2), in_proj_glu=(1024, 512, 2), out_proj=(1024, 512, 2),
             ffn_gate_up=(2048, 256, 1), ffn_down=(512, 512, 2), ple=(1024, 512, 2))


def _tiles(name):
    tm, tn, x_buffers = TILES[name]
    return dict(tm=tm, tn=tn, x_buffers=x_buffers, name=name)


def _cparams(*sem):
    return pltpu.CompilerParams(dimension_semantics=sem, vmem_limit_bytes=VMEM_LIMIT_BYTES)


def _round_up(n, m):
    return (n + m - 1) // m * m


def _pick_tile(n, pref):
    if n <= pref:
        return n
    t = pref - pref % LANES
    while t > LANES and n % t:
        t -= LANES
    assert n % t == 0, (n, pref)
    return t


def _mm_body(*refs, n_x, n_w, epi):
    refs = list(refs)
    x_refs = [refs.pop(0) for _ in range(n_x)]
    w_refs = [refs.pop(0) for _ in range(n_w)]
    res_ref = refs.pop(0) if epi in ("res", "ple") else None
    out_ref = refs.pop(0)

    def dot(x_ref, w_ref):
        return jnp.dot(x_ref[...].astype(BF16), w_ref[...].astype(BF16), preferred_element_type=F32)

    if epi in ("store", "res"):
        acc = dot(x_refs[0], w_refs[0])
        for xr, wr in zip(x_refs[1:], w_refs[1:]):
            acc = acc + dot(xr, wr)
        if epi == "res":
            acc = res_ref[...] + acc
        out_ref[...] = acc.astype(out_ref.dtype)
    elif epi == "glu":
        out_ref[...] = (dot(x_refs[0], w_refs[0]) * jax.nn.sigmoid(dot(x_refs[0], w_refs[1]))).astype(out_ref.dtype)
    elif epi == "swiglu":
        gate = dot(x_refs[0], w_refs[0])
        out_ref[...] = (gate * jax.nn.sigmoid(gate) * dot(x_refs[0], w_refs[1])).astype(out_ref.dtype)
    elif epi == "ple":
        pg = jax.nn.sigmoid(dot(x_refs[0], w_refs[0]))
        out_ref[...] = (res_ref[...] + pg * dot(x_refs[1], w_refs[1])).astype(out_ref.dtype)
    else:
        raise ValueError(epi)


def _matmul(xs, ws, *, res=None, epi="store", out_dtype=F32, n_out=None, w_blocks=None, tm, tn, x_buffers, name):
    m = xs[0].shape[0]
    n_out = ws[0].shape[1] if n_out is None else n_out
    tm = _pick_tile(m, tm)
    tn = _pick_tile(n_out, tn)
    w_blocks = [(0, 0)] * len(ws) if w_blocks is None else w_blocks
    in_specs = [pl.BlockSpec((tm, x.shape[1]), lambda i, j: (i, 0), pipeline_mode=pl.Buffered(x_buffers))
                for x in xs]
    for idx, (rb, cb) in enumerate(w_blocks):
        k = xs[min(idx, len(xs) - 1)].shape[1]
        in_specs.append(pl.BlockSpec((k, tn), functools.partial(lambda i, j, rb, cb: (rb, j + cb), rb=rb, cb=cb)))
    if res is not None:
        in_specs.append(pl.BlockSpec((tm, tn), lambda i, j: (i, j)))
    return pl.pallas_call(
        functools.partial(_mm_body, n_x=len(xs), n_w=len(ws), epi=epi),
        grid=(m // tm, n_out // tn),
        in_specs=in_specs,
        out_specs=pl.BlockSpec((tm, tn), lambda i, j: (i, j)),
        out_shape=jax.ShapeDtypeStruct((m, n_out), out_dtype),
        compiler_params=_cparams("parallel", "arbitrary"),
        name=name,
    )(*xs, *ws, *([] if res is None else [res]))


def _rms_body(x_ref, g_ref, o_ref):
    x = x_ref[...].astype(F32)
    ms = jnp.mean(x * x, axis=-1, keepdims=True)
    o_ref[...] = (x * lax.rsqrt(ms + RMS_EPS) * g_ref[...]).astype(o_ref.dtype)


def _rmsnorm(x, gain, *, out_dtype=F32, tm=512, name="rmsnorm"):
    m, d = x.shape
    tm = _pick_tile(m, tm)
    return pl.pallas_call(
        _rms_body,
        grid=(m // tm,),
        in_specs=[pl.BlockSpec((tm, d), lambda i: (i, 0)), pl.BlockSpec((1, d), lambda i: (0, 0))],
        out_specs=pl.BlockSpec((tm, d), lambda i: (i, 0)),
        out_shape=jax.ShapeDtypeStruct((m, d), out_dtype),
        compiler_params=_cparams("parallel"),
        name=name,
    )(x, gain.reshape(1, d).astype(F32))


def _split3(x):
    hi = x.astype(BF16)
    r1 = x - hi.astype(F32)
    mid = r1.astype(BF16)
    lo = (r1 - mid.astype(F32)).astype(BF16)
    return hi, mid, lo


def _dot_f32(a, b_hi, b_lo):
    a_hi, a_mid, _ = _split3(a)
    dot = lambda p, q: jnp.dot(p, q, preferred_element_type=F32)
    return dot(a_hi, b_hi) + (dot(a_mid, b_hi) + dot(a_hi, b_lo))


def _token_shift(z_ref, prev_ref, mu_ref, lo, width, fresh):
    nb, tt, _ = z_ref.shape
    z = z_ref[:, :, lo:lo + width]
    first = prev_ref[:, prev_ref.shape[1] - 1:, lo:lo + width]
    if fresh:
        first = jnp.where(pl.program_id(1) == 0, 0.0, first)
    tpos = lax.broadcasted_iota(jnp.int32, (nb, tt, 1), 1)
    zp = jnp.where(tpos == 0, first, pltpu.roll(z, 1, axis=1))
    return z + (zp - z) * mu_ref[:, lo:lo + width]


def _shift_specs(z, prev, mu, *, nb, tt, width, cb, fresh):
    if fresh:
        assert prev is z and tt % SUBLANES == 0
        prev_spec = pl.BlockSpec((nb, SUBLANES, width),
                                 lambda i, j: (i, jnp.maximum(j * (tt // SUBLANES) - 1, 0), cb))
    else:
        prev_spec = pl.BlockSpec((nb, 1, width), lambda i, j: (i, 0, cb))
    return [pl.BlockSpec((nb, tt, width), lambda i, j: (i, j, cb)), prev_spec,
            pl.BlockSpec((1, width), lambda i, j: (0, cb))]


def _prep_body(*refs, dr, lw, la, with_rkv, fresh):
    if with_rkv:
        zr_ref, prevr_ref, mur_ref = refs[:3]
        refs = refs[3:]
    zl_ref, prevl_ref, mul_ref, w0_ref, w2h_ref, w2l_ref, a0_ref, a2h_ref, a2l_ref = refs[:9]
    out_refs = refs[9:]
    nb, tt, _ = zl_ref.shape
    if with_rkv:
        for c, o_ref in enumerate(out_refs[:3]):
            o_ref[...] = _token_shift(zr_ref, prevr_ref, mur_ref, c * dr, dr, fresh)
    d_ref, a_ref = out_refs[-2:]
    zw = _token_shift(zl_ref, prevl_ref, mul_ref, 0, lw, fresh).reshape(nb * tt, lw)
    za = _token_shift(zl_ref, prevl_ref, mul_ref, lw, la, fresh).reshape(nb * tt, la)
    xw = w0_ref[...] + _dot_f32(jnp.tanh(zw), w2h_ref[...], w2l_ref[...])
    w_log = jnp.minimum(xw, 0.0) - jnp.log1p(jnp.exp(-jnp.abs(xw))) - 0.5
    d_ref[...] = jnp.exp(-jnp.exp(w_log)).reshape(nb, tt, dr)
    xa = a0_ref[...] + _dot_f32(za, a2h_ref[...], a2l_ref[...])
    a_ref[...] = jax.nn.sigmoid(xa).reshape(nb, tt, dr)


def _rwkv_prep(zr, prevr, mur, zl, prevl, mul, w0, w2h, w2l, a0, a2h, a2l, *, nb, tt, with_rkv, fresh, name):
    bsz, t, _ = zl.shape
    dr = w0.shape[-1]
    lw, la = w2h.shape[0], a2h.shape[0]
    row = lambda c: pl.BlockSpec((1, c), lambda i, j: (0, 0))
    full = lambda a: pl.BlockSpec(a.shape, lambda i, j: (0, 0))
    shift = functools.partial(_shift_specs, nb=nb, tt=tt, cb=0, fresh=fresh)
    in_specs, args = [], []
    if with_rkv:
        in_specs += shift(zr, prevr, mur, width=zr.shape[-1])
        args += [zr, prevr, mur]
    in_specs += shift(zl, prevl, mul, width=zl.shape[-1]) + [row(dr), full(w2h), full(w2l), row(dr), full(a2h),
                                                            full(a2l)]
    args += [zl, prevl, mul, w0, w2h, w2l, a0, a2h, a2l]
    n_out = 5 if with_rkv else 2
    return pl.pallas_call(
        functools.partial(_prep_body, dr=dr, lw=lw, la=la, with_rkv=with_rkv, fresh=fresh),
        grid=(bsz // nb, t // tt),
        in_specs=in_specs,
        out_specs=[pl.BlockSpec((nb, tt, dr), lambda i, j: (i, j, 0))] * n_out,
        out_shape=[jax.ShapeDtypeStruct((bsz, t, dr), F32)] * n_out,
        compiler_params=_cparams("parallel", "parallel"),
        name=name,
    )(*args)


RELAYOUT_ROWS = 128
SCAN_PITCH = HEAD_SIZE + SUBLANES


def _to_scan_body(*refs, nh, bsz, mix):
    x_refs, refs = refs[:bsz], refs[bsz:]
    if mix:
        prev_refs, mu_ref, refs = refs[:bsz], refs[bsz], refs[bsz + 1:]
    o_ref, xt_ref = refs
    n, tq = HEAD_SIZE, RELAYOUT_ROWS
    for b in range(bsz):
        x = x_refs[b][0]
        if mix:
            first = jnp.where(pl.program_id(0) == 0, 0.0, prev_refs[b][0, SUBLANES - 1:, :])
            row = lax.broadcasted_iota(jnp.int32, (tq, 1), 0)
            xp = jnp.where(row == 0, first, pltpu.roll(x, 1, axis=0))
            x = x + (xp - x) * mu_ref[...]
        xt = x.T
        for h in range(nh):
            xt_ref[b, h * SCAN_PITCH:h * SCAN_PITCH + n, :] = xt[h * n:(h + 1) * n, :]
    for j in range(n):
        m = jnp.concatenate([xt_ref[b, pl.ds(j, nh, stride=SCAN_PITCH), :] for b in range(bsz)], axis=0)
        o_ref[pl.ds(j, tq, stride=SCAN_PITCH), :] = m.T
    for j in range(n, SCAN_PITCH):
        o_ref[pl.ds(j, tq, stride=SCAN_PITCH), :] = jnp.zeros((tq, LANES), F32)


def _to_scan_bh(x3, *, nh, name, col_block=0, mu=None):
    bsz, t, _ = x3.shape
    d = nh * HEAD_SIZE
    assert bsz * nh == LANES and t % RELAYOUT_ROWS == 0
    in_specs = [pl.BlockSpec((1, RELAYOUT_ROWS, d), functools.partial(lambda i, b: (b, i, col_block), b=b))
                for b in range(bsz)]
    args = [x3] * bsz
    if mu is not None:
        per = RELAYOUT_ROWS // SUBLANES
        in_specs += [pl.BlockSpec((1, SUBLANES, d),
                                  functools.partial(lambda i, b: (b, jnp.maximum(i * per - 1, 0), col_block), b=b))
                     for b in range(bsz)] + [pl.BlockSpec((1, d), lambda i: (0, col_block))]
        args += [x3] * bsz + [mu]
    out = pl.pallas_call(
        functools.partial(_to_scan_body, nh=nh, bsz=bsz, mix=mu is not None),
        grid=(t // RELAYOUT_ROWS,),
        in_specs=in_specs,
        out_specs=pl.BlockSpec((RELAYOUT_ROWS * SCAN_PITCH, LANES), lambda i: (i, 0)),
        out_shape=jax.ShapeDtypeStruct((t * SCAN_PITCH, LANES), F32),
        scratch_shapes=[pltpu.VMEM((bsz, nh * SCAN_PITCH, RELAYOUT_ROWS), F32)],
        compiler_params=_cparams("parallel"),
        name=name,
    )(*args)
    return out.reshape(1, t, SCAN_PITCH, LANES)


def _from_scan_body(*refs, nh, bsz):
    y_ref, o_ref, yt_ref = refs
    n, tq = HEAD_SIZE, RELAYOUT_ROWS
    for j in range(n):
        mt = y_ref[pl.ds(j, tq, stride=n), :].T
        for b in range(bsz):
            yt_ref[b, pl.ds(j, nh, stride=SCAN_PITCH), :] = mt[b * nh:(b + 1) * nh, :]
    for b in range(bsz):
        yt = jnp.concatenate([yt_ref[b, h * SCAN_PITCH:h * SCAN_PITCH + n, :] for h in range(nh)], axis=0)
        o_ref[b] = yt.T


def _from_scan_bh(y, *, bsz, nh, name):
    t, n, _ = y.shape
    d = nh * n
    assert bsz * nh == LANES and n == HEAD_SIZE and t % RELAYOUT_ROWS == 0
    return pl.pallas_call(
        functools.partial(_from_scan_body, nh=nh, bsz=bsz),
        grid=(t // RELAYOUT_ROWS,),
        in_specs=[pl.BlockSpec((RELAYOUT_ROWS * n, LANES), lambda i: (i, 0))],
        out_specs=pl.BlockSpec((bsz, RELAYOUT_ROWS, d), lambda i: (0, i, 0)),
        out_shape=jax.ShapeDtypeStruct((bsz, t, d), F32),
        scratch_shapes=[pltpu.VMEM((bsz, nh * SCAN_PITCH, RELAYOUT_ROWS), F32)],
        compiler_params=_cparams("parallel"),
        name=name,
    )(y.reshape(t * n, LANES))


def _scan_body(r_ref, k_ref, v_ref, d_ref, a_ref, s0_ref, kk_ref, ka_ref, rk_ref, lw_ref, lb_ref,
               y_ref, s_ref, yrow_ref):
    n = HEAD_SIZE
    tc = r_ref.shape[0]

    @pl.when(pl.program_id(1) == 0)
    def _():
        s_ref[...] = s0_ref[...]

    def colsum(x):
        return jnp.sum(x, axis=0, keepdims=True)

    def step(t, gamma):
        r, k, v, w, a = (x[t, :n, :] for x in (r_ref, k_ref, v_ref, d_ref, a_ref))
        kk = k * kk_ref[...]
        kk = kk / jnp.maximum(jnp.sqrt(colsum(kk * kk)), KK_EPS)
        k2 = k * (1.0 + (a - 1.0) * ka_ref[...])
        kk_n = kk * gamma
        gamma = gamma * w
        inv = 1.0 / gamma
        b_n = kk * a * inv
        k_n = k2 * inv
        r_n = r * gamma
        for i in range(n):
            s_i = s_ref[i]
            s_kk = colsum(s_i * kk_n)
            s_i = s_i - s_kk * b_n + v_ref[t, i:i + 1, :] * k_n
            s_ref[i] = s_i
            yrow_ref[i:i + 1, :] = colsum(s_i * r_n)
        y = yrow_ref[...]
        ym = jnp.mean(y, axis=0, keepdims=True)
        yc = y - ym
        yv = jnp.mean(yc * yc, axis=0, keepdims=True)
        yn = yc * lax.rsqrt(yv + GN_EPS) * lw_ref[...] + lb_ref[...]
        bonus = colsum(r * k2 * rk_ref[...]) * v
        y_ref[t] = yn + bonus
        return gamma

    gamma = lax.fori_loop(0, tc, step, jnp.ones((n, LANES), F32))
    s_ref[...] = s_ref[...] * gamma


def _wkv_scan(r, k, v, d, a, s0, kkp, kap, rkp, lnw, lnb, *, tc, name):
    g, t, pitch, _ = r.shape
    n = HEAD_SIZE
    tc = min(tc, t)
    seq_in = pl.BlockSpec((None, tc, pitch, LANES), lambda gi, ci: (gi, ci, 0, 0))
    seq = pl.BlockSpec((None, tc, n, LANES), lambda gi, ci: (gi, ci, 0, 0))
    par = pl.BlockSpec((None, n, LANES), lambda gi, ci: (gi, 0, 0))
    st = pl.BlockSpec((None, n, n, LANES), lambda gi, ci: (gi, 0, 0, 0))
    return pl.pallas_call(
        _scan_body,
        grid=(g, t // tc),
        in_specs=[seq_in] * 5 + [st] + [par] * 5,
        out_specs=[seq, st],
        out_shape=[jax.ShapeDtypeStruct((g, t, n, LANES), F32),
                   jax.ShapeDtypeStruct((g, n, n, LANES), F32)],
        scratch_shapes=[pltpu.VMEM((n, LANES), F32)],
        compiler_params=_cparams("parallel", "arbitrary"),
        name=name,
    )(r, k, v, d, a, s0, kkp, kap, rkp, lnw, lnb)


def _gate_body(y_ref, zg_ref, prev_ref, mu_ref, g2h_ref, g2l_ref, o_ref, *, fresh):
    nb, tt, lg = zg_ref.shape
    zg = _token_shift(zg_ref, prev_ref, mu_ref, 0, lg, fresh).reshape(nb * tt, lg)
    g = _dot_f32(jax.nn.sigmoid(zg), g2h_ref[...], g2l_ref[...])
    o_ref[...] = (y_ref[...].reshape(nb * tt, -1) * g).reshape(o_ref.shape).astype(o_ref.dtype)


def _rwkv_gate(y3, zl, prevl, mul, g2h, g2l, *, zg_block, nb, tt, fresh, name):
    bsz, t, dr = y3.shape
    lg = g2h.shape[0]
    return pl.pallas_call(
        functools.partial(_gate_body, fresh=fresh),
        grid=(bsz // nb, t // tt),
        in_specs=[pl.BlockSpec((nb, tt, dr), lambda i, j: (i, j, 0))]
        + _shift_specs(zl, prevl, mul, nb=nb, tt=tt, width=lg, cb=zg_block, fresh=fresh)
        + [pl.BlockSpec(g2h.shape, lambda i, j: (0, 0)), pl.BlockSpec(g2l.shape, lambda i, j: (0, 0))],
        out_specs=pl.BlockSpec((nb, tt, dr), lambda i, j: (i, j, 0)),
        out_shape=jax.ShapeDtypeStruct((bsz, t, dr), BF16),
        compiler_params=_cparams("parallel", "parallel"),
        name=name,
    )(y3, zl, prevl, mul, g2h, g2l)


CONV_ROW_CHUNK = 32
CONV_LANE_CHUNK = 512


def _conv_body(u_ref, hist_ref, cw_ref, cb_ref, lw_ref, lb_ref, o_ref, full_ref, fs_ref, c_ref, *,
               hist_rows, taps, fresh):
    nb, tt, dc = u_ref.shape
    hin = hist_ref.shape[1]
    hist = hist_ref[...]
    if fresh:
        hist = jnp.where(pl.program_id(1) == 0, 0.0, hist)
    if hin < hist_rows:
        full_ref[:, :hist_rows - hin, :] = jnp.zeros((nb, hist_rows - hin, dc), F32)
    full_ref[:, hist_rows - hin:hist_rows, :] = hist
    full_ref[:, hist_rows:, :] = u_ref[...]
    lead = hist_rows - (taps - 1)
    span = hist_rows + tt - SUBLANES
    for s in range(1, SUBLANES):
        fs_ref[s - 1] = full_ref[:, s:s + span, :]
    rc = min(CONV_ROW_CHUNK, tt)
    lc = min(CONV_LANE_CHUNK, dc)

    def row_chunk(ri, carry, *, b, c0):
        r0 = pl.multiple_of(ri * rc, rc)
        acc = jnp.broadcast_to(cb_ref[:, c0:c0 + lc], (rc, lc))
        for j in range(taps):
            s = (lead + j) % SUBLANES
            rows = pl.ds(pl.multiple_of(r0 + (lead + j - s), SUBLANES), rc)
            win = full_ref[b, rows, c0:c0 + lc] if s == 0 else fs_ref[s - 1, b, rows, c0:c0 + lc]
            acc = acc + win * cw_ref[j:j + 1, c0:c0 + lc]
        c_ref[b, pl.ds(r0, rc), c0:c0 + lc] = acc
        return carry

    for b in range(nb):
        for c0 in range(0, dc, lc):
            lax.fori_loop(0, tt // rc, functools.partial(row_chunk, b=b, c0=c0), 0)
    acc = c_ref[...]
    mu = jnp.mean(acc, axis=-1, keepdims=True)
    xc = acc - mu
    var = jnp.mean(xc * xc, axis=-1, keepdims=True)
    y = xc * lax.rsqrt(var + LN_EPS) * lw_ref[...] + lb_ref[...]
    o_ref[...] = (y * jax.nn.sigmoid(y)).astype(o_ref.dtype)


def _conv_module(u3, conv0, cw, cb, lnw, lnb, *, nb, tt, name):
    bsz, t, dc = u3.shape
    taps = cw.shape[0]
    hist_rows = _round_up(taps - 1, SUBLANES)
    fresh = conv0 is None
    if fresh:
        assert nb == 1 and tt % hist_rows == 0
        hist, hin = u3, hist_rows
        hist_map = lambda i, j: (i, jnp.maximum(j * (tt // hist_rows) - 1, 0), 0)
    else:
        assert tt == t
        hist, hin = conv0, conv0.shape[1]
        hist_map = lambda i, j: (i, 0, 0)
    row = pl.BlockSpec((1, dc), lambda i, j: (0, 0))
    return pl.pallas_call(
        functools.partial(_conv_body, hist_rows=hist_rows, taps=taps, fresh=fresh),
        grid=(bsz // nb, t // tt),
        in_specs=[pl.BlockSpec((nb, tt, dc), lambda i, j: (i, j, 0)),
                  pl.BlockSpec((nb, hin, dc), hist_map),
                  pl.BlockSpec((taps, dc), lambda i, j: (0, 0)), row, row, row],
        out_specs=pl.BlockSpec((nb, tt, dc), lambda i, j: (i, j, 0)),
        out_shape=jax.ShapeDtypeStruct((bsz, t, dc), BF16),
        scratch_shapes=[pltpu.VMEM((nb, hist_rows + tt, dc), F32),
                        pltpu.VMEM((SUBLANES - 1, nb, hist_rows + tt - SUBLANES, dc), F32),
                        pltpu.VMEM((nb, tt, dc), F32)],
        compiler_params=_cparams("parallel", "parallel"),
        name=name,
    )(u3, hist, cw, cb.reshape(1, dc), lnw.reshape(1, dc), lnb.reshape(1, dc))


def _split_hi_lo(w, rows):
    w = jnp.pad(w.astype(F32), ((0, rows - w.shape[0]), (0, 0)))
    hi = w.astype(BF16)
    return hi, (w - hi.astype(F32)).astype(BF16)


def _layer(h, pe, wkv0, shift0, conv0, lanes_are_batch_head, p):
    bsz, t, d = h.shape
    m = bsz * t
    dr, dc, n = p["dr"], p["dc"], HEAD_SIZE
    nh = dr // n
    hbuf = conv0.shape[1] if conv0 is not None else p["conv_w"].shape[0] - 1
    x2 = h.reshape(m, d)
    fresh = shift0 is None

    xn = _rmsnorm(x2, p["g_mix"], out_dtype=BF16, name="mix_norm")
    zr = _matmul([xn], [p["w_in"]], n_out=3 * dr, **_tiles("in_proj_rkv")).reshape(bsz, t, 3 * dr)
    zl = _matmul([xn], [p["w_lora"]], **_tiles("in_proj_lora")).reshape(bsz, t, -1)
    tg = _pick_tile(dc, TILES["in_proj_glu"][1])
    u = _matmul([xn], [p["w_in_c"], p["w_in_c"]], epi="glu", n_out=dc, w_blocks=[(0, 0), (0, dc // tg)],
                **_tiles("in_proj_glu"))
    new_shift = _rmsnorm(h[:, -1, :], p["g_mix"], name="shift_norm")

    if fresh:
        tt, nb = 256, 1
        prevr, prevl = zr, zl
    else:
        tt, nb = t, 32
        s0b = shift0.astype(BF16)
        prevr = _matmul([s0b], [p["w_in"]], n_out=3 * dr, **_tiles("shift_proj")).reshape(bsz, 1, 3 * dr)
        prevl = _matmul([s0b], [p["w_lora"]], **_tiles("shift_lora")).reshape(bsz, 1, -1)
    prep = functools.partial(_rwkv_prep, zr, prevr, p["mu_rkv"], zl, prevl, p["mu_lora"], p["w0"], p["w2h"], p["w2l"],
                             p["a0"], p["a2h"], p["a2l"], nb=nb, tt=tt, fresh=fresh, name="rwkv_prep")

    if lanes_are_batch_head:
        assert fresh
        dec, a = prep(with_rkv=False)
        scan_in = [_to_scan_bh(zr, nh=nh, col_block=c, mu=p["mu_rkv"], name="mix_to_scan") for c in range(3)]
        scan_in += [_to_scan_bh(x, nh=nh, name="to_scan") for x in (dec, a)]
        par = lambda q: jnp.tile(q.reshape(nh, n).T, (1, bsz))[None]
        s0 = jnp.zeros((1, n, n, LANES), F32)
    else:
        to_scan = lambda x: jnp.transpose(x.reshape(bsz, t, nh, n), (2, 1, 3, 0))
        scan_in = [to_scan(x) for x in prep(with_rkv=True)]
        par = lambda q: jnp.broadcast_to(q.reshape(nh, n)[:, :, None], (nh, n, LANES))
        s0 = jnp.transpose(wkv0, (1, 2, 3, 0))
    y_s, s_new = _wkv_scan(*scan_in, s0, par(p["k_k"]), par(p["k_a"]), par(p["r_k"]), par(p["lnx_w"]),
                           par(p["lnx_b"]), tc=32, name="wkv_scan")
    if lanes_are_batch_head:
        y_tok = _from_scan_bh(y_s[0], bsz=bsz, nh=nh, name="from_scan")
        new_wkv = jnp.transpose(s_new.reshape(n, n, bsz, nh), (2, 3, 0, 1))
    else:
        y_tok = jnp.transpose(y_s, (3, 1, 0, 2)).reshape(bsz, t, dr)
        new_wkv = jnp.transpose(s_new, (3, 0, 1, 2))
    y_rwkv = _rwkv_gate(y_tok, zl, prevl, p["mu_lora"], p["g2h"], p["g2l"], zg_block=p["zg_block"], nb=nb, tt=tt,
                        fresh=fresh, name="rwkv_gate").reshape(m, dr)


    u3 = u.reshape(bsz, t, dc)
    if conv0 is None:
        new_conv = u3[:, t - hbuf:, :]
        cnb = 1
    else:
        new_conv = jnp.concatenate([conv0, u3], axis=1)[:, -hbuf:, :]
        cnb = 4
    y_conv = _conv_module(u3, conv0, p["conv_w"], p["conv_b"], p["conv_ln_w"], p["conv_ln_b"], nb=cnb, tt=tt,
                          name="conv_module").reshape(m, dc)

    assert dr == dc
    h1 = _matmul([y_rwkv, y_conv], [p["w_out"], p["w_out"]], w_blocks=[(0, 0), (1, 0)], res=x2, epi="res",
                 **_tiles("out_proj"))
    dff = p["w_down"].shape[0]
    tf = _pick_tile(dff, TILES["ffn_gate_up"][1])
    h1n = _rmsnorm(h1, p["g_ffn"], out_dtype=BF16, name="ffn_norm")
    act = _matmul([h1n], [p["w_gate_up"], p["w_gate_up"]], epi="swiglu", n_out=dff, out_dtype=BF16,
                  w_blocks=[(0, 0), (0, dff // tf)], **_tiles("ffn_gate_up"))
    h2 = _matmul([act], [p["w_down"]], res=h1, epi="res", **_tiles("ffn_down"))
    h2n = _rmsnorm(h2, p["g_ple"], out_dtype=BF16, name="ple_norm")
    h3 = _matmul([h2n, pe.reshape(m, -1)], [p["w_ple_gate"], p["w_ple_proj"]], res=h2, epi="ple", **_tiles("ple"))
    return h3.reshape(bsz, t, d), new_wkv, new_shift, new_conv


def _prep_params(i, g_mix, w_in, mu_shift, w0, w2, a0, a2, g2, k_k, k_a, r_k, lnx_w, lnx_b, conv_w, conv_b,
                 conv_ln_w, conv_ln_b, w_out, g_ffn, w_gate_up, w_down, g_ple, w_ple_gate, w_ple_proj):
    dr = w0.shape[-1]
    dc = conv_w.shape[-1]
    lw, la, lg = w2.shape[1], a2.shape[1], g2.shape[1]
    lwp, lap, lgp = (_round_up(x, LANES) for x in (lw, la, lg))

    def lora_cols(x):
        pad = lambda y, to: jnp.pad(y, [(0, 0)] * (y.ndim - 1) + [(0, to - y.shape[-1])])
        return jnp.concatenate([pad(x[..., :lw], lwp), pad(x[..., lw:lw + la], lap),
                                pad(x[..., lw + la:lw + la + lg], lgp)], axis=-1)

    lora_n = lw + la + lg
    wi = w_in[i]
    w_tail = lax.optimization_barrier(wi[:, 3 * dr:])
    w2h, w2l = _split_hi_lo(w2[i], lwp)
    a2h, a2l = _split_hi_lo(a2[i], lap)
    g2h, g2l = _split_hi_lo(g2[i], lgp)
    assert (lwp + lap) % lgp == 0
    return dict(
        dr=dr, dc=dc,
        g_mix=g_mix[i], w_in=wi, w_lora=lora_cols(w_tail).astype(BF16), w_in_c=w_tail[:, lora_n:].astype(BF16),
        mu_rkv=mu_shift[i][None, :3 * dr], mu_lora=lora_cols(mu_shift[i][None, 3 * dr:]),
        zg_block=(lwp + lap) // lgp,
        w0=w0[i][None, :], w2h=w2h, w2l=w2l, a0=a0[i][None, :], a2h=a2h, a2l=a2l, g2h=g2h, g2l=g2l,
        k_k=k_k[i], k_a=k_a[i], r_k=r_k[i], lnx_w=lnx_w[i], lnx_b=lnx_b[i],
        conv_w=conv_w[i], conv_b=conv_b[i], conv_ln_w=conv_ln_w[i], conv_ln_b=conv_ln_b[i],
        w_out=w_out[i].astype(BF16), g_ffn=g_ffn[i], w_gate_up=w_gate_up[i], w_down=w_down[i].astype(BF16),
        g_ple=g_ple[i], w_ple_gate=w_ple_gate[i].astype(BF16), w_ple_proj=w_ple_proj[i].astype(BF16),
    )


def kernel(x_prompt, x_sample, state_wkv, state_shift, state_conv, p_prompt, p_sample, g_mix, w_in, mu_shift, w0, w2, a0, a2, g2, k_k, k_a, r_k, lnx_w, lnx_b, conv_w, conv_b, conv_ln_w, conv_ln_b, w_out, g_ffn, w_gate_up, w_down, g_ple, w_ple_gate, w_ple_proj, g_final):
    depth = w_in.shape[0]
    layer_params = (g_mix, w_in, mu_shift, w0, w2, a0, a2, g2, k_k, k_a, r_k, lnx_w, lnx_b, conv_w, conv_b,
                    conv_ln_w, conv_ln_b, w_out, g_ffn, w_gate_up, w_down, g_ple, w_ple_gate, w_ple_proj)
    hp, hs = x_prompt, x_sample
    outs = [[] for _ in range(6)]
    for i in range(depth):
        p = _prep_params(i, *layer_params)
        hp, s_w, s_sh, s_c = _layer(hp, p_prompt[i], None, None, None, True, p)
        outs[0].append(s_w); outs[1].append(s_sh); outs[2].append(s_c)
        hs, s_w, s_sh, s_c = _layer(hs, p_sample[i], state_wkv[i], state_shift[i], state_conv[i], False, p)
        outs[3].append(s_w); outs[4].append(s_sh); outs[5].append(s_c)
    d = hp.shape[-1]
    y_prompt = _rmsnorm(hp.reshape(-1, d), g_final, name="final_norm").reshape(hp.shape)
    y_sample = _rmsnorm(hs.reshape(-1, d), g_final, name="final_norm").reshape(hs.shape)
    return (y_prompt, y_sample) + tuple(jnp.stack(o) for o in outs)
```

```python
import functools

import jax
import jax.numpy as jnp
from jax import lax
from jax.experimental import pallas as pl
from jax.experimental.pallas import tpu as pltpu

HEAD_SIZE = 64
RMS_EPS = 1e-6
LN_EPS = 1e-5
GN_EPS = 64e-5
KK_EPS = 1e-12
LANES = 128
SUBLANES = 8
VMEM_LIMIT_BYTES = 56 * 2 ** 20

F32 = jnp.float32
BF16 = jnp.bfloat16


TILES = dict(in_proj_rkv=(2048, 512, 1), in_proj_lora=(1024, 512, 2), shift_proj=(128, 512, 2),
             shift_lora=(128, 512, 2), in_proj_glu=(1024, 512, 2), out_proj=(1024, 512, 2),
             ffn_gate_up=(2048, 256, 1), ffn_down=(512, 512, 2), ple=(1024, 512, 2))


def _tiles(name):
    tm, tn, x_buffers = TILES[name]
    return dict(tm=tm, tn=tn, x_buffers=x_buffers, name=name)


def _cparams(*sem):
    return pltpu.CompilerParams(dimension_semantics=sem, vmem_limit_bytes=VMEM_LIMIT_BYTES)


def _round_up(n, m):
    return (n + m - 1) // m * m


def _pick_tile(n, pref):
    if n <= pref:
        return n
    t = pref - pref % LANES
    while t > LANES and n % t:
        t -= LANES
    assert n % t == 0, (n, pref)
    return t


def _mm_body(*refs, n_x, n_w, epi, w_transposed):
    refs = list(refs)
    x_refs = [refs.pop(0) for _ in range(n_x)]
    w_refs = [refs.pop(0) for _ in range(n_w)]
    res_ref = refs.pop(0) if epi in ("res", "ple") else None
    out_ref = refs.pop(0)

    def dot(x_ref, w_ref):
        contract = (((1,), (1 if w_transposed else 0,)), ((), ()))
        return lax.dot_general(x_ref[...].astype(BF16), w_ref[...].astype(BF16), contract,
                               preferred_element_type=F32)

    if epi in ("store", "res"):
        acc = dot(x_refs[0], w_refs[0])
        for xr, wr in zip(x_refs[1:], w_refs[1:]):
            acc = acc + dot(xr, wr)
        if epi == "res":
            acc = res_ref[...] + acc
        out_ref[...] = acc.astype(out_ref.dtype)
    elif epi == "glu":
        out_ref[...] = (dot(x_refs[0], w_refs[0]) * jax.nn.sigmoid(dot(x_refs[0], w_refs[1]))).astype(out_ref.dtype)
    elif epi == "swiglu":
        gate = dot(x_refs[0], w_refs[0])
        out_ref[...] = (gate * jax.nn.sigmoid(gate) * dot(x_refs[0], w_refs[1])).astype(out_ref.dtype)
    elif epi == "ple":
        pg = jax.nn.sigmoid(dot(x_refs[0], w_refs[0]))
        out_ref[...] = (res_ref[...] + pg * dot(x_refs[1], w_refs[1])).astype(out_ref.dtype)
    else:
        raise ValueError(epi)


def _matmul(xs, ws, *, res=None, epi="store", out_dtype=F32, n_out=None, w_blocks=None, w_transposed=False,
            tm, tn, x_buffers, name):
    m = xs[0].shape[0]
    n_out = ws[0].shape[0 if w_transposed else 1] if n_out is None else n_out
    tm = _pick_tile(m, tm)
    tn = _pick_tile(n_out, tn)
    w_blocks = [(0, 0)] * len(ws) if w_blocks is None else w_blocks
    in_specs = [pl.BlockSpec((tm, x.shape[1]), lambda i, j: (i, 0), pipeline_mode=pl.Buffered(x_buffers))
                for x in xs]
    for idx, (rb, cb) in enumerate(w_blocks):
        k = xs[min(idx, len(xs) - 1)].shape[1]
        if w_transposed:
            in_specs.append(pl.BlockSpec((tn, k), functools.partial(lambda i, j, rb, cb: (j + cb, rb), rb=rb, cb=cb)))
        else:
            in_specs.append(pl.BlockSpec((k, tn), functools.partial(lambda i, j, rb, cb: (rb, j + cb), rb=rb, cb=cb)))
    if res is not None:
        in_specs.append(pl.BlockSpec((tm, tn), lambda i, j: (i, j)))
    return pl.pallas_call(
        functools.partial(_mm_body, n_x=len(xs), n_w=len(ws), epi=epi, w_transposed=w_transposed),
        grid=(m // tm, n_out // tn),
        in_specs=in_specs,
        out_specs=pl.BlockSpec((tm, tn), lambda i, j: (i, j)),
        out_shape=jax.ShapeDtypeStruct((m, n_out), out_dtype),
        compiler_params=_cparams("parallel", "arbitrary"),
        name=name,
    )(*xs, *ws, *([] if res is None else [res]))


def _rms_body(x_ref, g_ref, o_ref):
    x = x_ref[...].astype(F32)
    ms = jnp.mean(x * x, axis=-1, keepdims=True)
    o_ref[...] = (x * lax.rsqrt(ms + RMS_EPS) * g_ref[...]).astype(o_ref.dtype)


def _rmsnorm(x, gain, *, out_dtype=F32, tm=512, name="rmsnorm"):
    m, d = x.shape
    tm = _pick_tile(m, tm)
    return pl.pallas_call(
        _rms_body,
        grid=(m // tm,),
        in_specs=[pl.BlockSpec((tm, d), lambda i: (i, 0)), pl.BlockSpec((1, d), lambda i: (0, 0))],
        out_specs=pl.BlockSpec((tm, d), lambda i: (i, 0)),
        out_shape=jax.ShapeDtypeStruct((m, d), out_dtype),
        compiler_params=_cparams("parallel"),
        name=name,
    )(x, gain.reshape(1, d).astype(F32))


def _split3(x):
    hi = x.astype(BF16)
    r1 = x - hi.astype(F32)
    mid = r1.astype(BF16)
    lo = (r1 - mid.astype(F32)).astype(BF16)
    return hi, mid, lo


def _dot_f32(a, b_hi, b_lo):
    a_hi, a_mid, _ = _split3(a)
    dot = lambda p, q: jnp.dot(p, q, preferred_element_type=F32)
    return dot(a_hi, b_hi) + (dot(a_mid, b_hi) + dot(a_hi, b_lo))


def _token_shift(z_ref, prev_ref, mu_ref, lo, width, fresh):
    nb, tt, _ = z_ref.shape
    z = z_ref[:, :, lo:lo + width]
    first = prev_ref[:, prev_ref.shape[1] - 1:, lo:lo + width]
    if fresh:
        first = jnp.where(pl.program_id(1) == 0, 0.0, first)
    tpos = lax.broadcasted_iota(jnp.int32, (nb, tt, 1), 1)
    zp = jnp.where(tpos == 0, first, pltpu.roll(z, 1, axis=1))
    return z + (zp - z) * mu_ref[:, lo:lo + width]


def _shift_specs(z, prev, mu, *, nb, tt, width, cb, fresh):
    if fresh:
        assert prev is z and tt % SUBLANES == 0
        prev_spec = pl.BlockSpec((nb, SUBLANES, width),
                                 lambda i, j: (i, jnp.maximum(j * (tt // SUBLANES) - 1, 0), cb))
    else:
        prev_spec = pl.BlockSpec((nb, 1, width), lambda i, j: (i, 0, cb))
    return [pl.BlockSpec((nb, tt, width), lambda i, j: (i, j, cb)), prev_spec,
            pl.BlockSpec((1, width), lambda i, j: (0, cb))]


def _prep_body(*refs, dr, lw, la, with_rkv, fresh):
    if with_rkv:
        zr_ref, prevr_ref, mur_ref = refs[:3]
        refs = refs[3:]
    zl_ref, prevl_ref, mul_ref, w0_ref, w2h_ref, w2l_ref, a0_ref, a2h_ref, a2l_ref = refs[:9]
    out_refs = refs[9:]
    nb, tt, _ = zl_ref.shape
    if with_rkv:
        for c, o_ref in enumerate(out_refs[:3]):
            o_ref[...] = _token_shift(zr_ref, prevr_ref, mur_ref, c * dr, dr, fresh)
    d_ref, a_ref = out_refs[-2:]
    zw = _token_shift(zl_ref, prevl_ref, mul_ref, 0, lw, fresh).reshape(nb * tt, lw)
    za = _token_shift(zl_ref, prevl_ref, mul_ref, lw, la, fresh).reshape(nb * tt, la)
    xw = w0_ref[...] + _dot_f32(jnp.tanh(zw), w2h_ref[...], w2l_ref[...])
    w_log = jnp.minimum(xw, 0.0) - jnp.log1p(jnp.exp(-jnp.abs(xw))) - 0.5
    d_ref[...] = jnp.exp(-jnp.exp(w_log)).reshape(nb, tt, dr)
    xa = a0_ref[...] + _dot_f32(za, a2h_ref[...], a2l_ref[...])
    a_ref[...] = jax.nn.sigmoid(xa).reshape(nb, tt, dr)


def _rwkv_prep(zr, prevr, mur, zl, prevl, mul, w0, w2h, w2l, a0, a2h, a2l, *, nb, tt, with_rkv, fresh, name):
    bsz, t, _ = zl.shape
    dr = w0.shape[-1]
    lw, la = w2h.shape[0], a2h.shape[0]
    row = lambda c: pl.BlockSpec((1, c), lambda i, j: (0, 0))
    full = lambda a: pl.BlockSpec(a.shape, lambda i, j: (0, 0))
    shift = functools.partial(_shift_specs, nb=nb, tt=tt, cb=0, fresh=fresh)
    in_specs, args = [], []
    if with_rkv:
        in_specs += shift(zr, prevr, mur, width=zr.shape[-1])
        args += [zr, prevr, mur]
    in_specs += shift(zl, prevl, mul, width=zl.shape[-1]) + [row(dr), full(w2h), full(w2l), row(dr), full(a2h),
                                                            full(a2l)]
    args += [zl, prevl, mul, w0, w2h, w2l, a0, a2h, a2l]
    n_out = 5 if with_rkv else 2
    return pl.pallas_call(
        functools.partial(_prep_body, dr=dr, lw=lw, la=la, with_rkv=with_rkv, fresh=fresh),
        grid=(bsz // nb, t // tt),
        in_specs=in_specs,
        out_specs=[pl.BlockSpec((nb, tt, dr), lambda i, j: (i, j, 0))] * n_out,
        out_shape=[jax.ShapeDtypeStruct((bsz, t, dr), F32)] * n_out,
        compiler_params=_cparams("parallel", "parallel"),
        name=name,
    )(*args)


RELAYOUT_ROWS = 128
SCAN_PITCH = HEAD_SIZE + SUBLANES


def _to_scan_body(*refs, nh, bsz, mix):
    x_refs, refs = refs[:bsz], refs[bsz:]
    if mix:
        prev_refs, mu_ref, refs = refs[:bsz], refs[bsz], refs[bsz + 1:]
    o_ref, xt_ref = refs
    n, tq = HEAD_SIZE, RELAYOUT_ROWS
    for b in range(bsz):
        x = x_refs[b][0]
        if mix:
            first = jnp.where(pl.program_id(0) == 0, 0.0, prev_refs[b][0, SUBLANES - 1:, :])
            row = lax.broadcasted_iota(jnp.int32, (tq, 1), 0)
            xp = jnp.where(row == 0, first, pltpu.roll(x, 1, axis=0))
            x = x + (xp - x) * mu_ref[...]
        xt = x.T
        for h in range(nh):
            xt_ref[b, h * SCAN_PITCH:h * SCAN_PITCH + n, :] = xt[h * n:(h + 1) * n, :]
    for j in range(n):
        m = jnp.concatenate([xt_ref[b, pl.ds(j, nh, stride=SCAN_PITCH), :] for b in range(bsz)], axis=0)
        o_ref[pl.ds(j, tq, stride=SCAN_PITCH), :] = m.T
    for j in range(n, SCAN_PITCH):
        o_ref[pl.ds(j, tq, stride=SCAN_PITCH), :] = jnp.zeros((tq, LANES), F32)


def _to_scan_bh(x3, *, nh, name, col_block=0, mu=None):
    bsz, t, _ = x3.shape
    d = nh * HEAD_SIZE
    assert bsz * nh == LANES and t % RELAYOUT_ROWS == 0
    in_specs = [pl.BlockSpec((1, RELAYOUT_ROWS, d), functools.partial(lambda i, b: (b, i, col_block), b=b))
                for b in range(bsz)]
    args = [x3] * bsz
    if mu is not None:
        per = RELAYOUT_ROWS // SUBLANES
        in_specs += [pl.BlockSpec((1, SUBLANES, d),
                                  functools.partial(lambda i, b: (b, jnp.maximum(i * per - 1, 0), col_block), b=b))
                     for b in range(bsz)] + [pl.BlockSpec((1, d), lambda i: (0, col_block))]
        args += [x3] * bsz + [mu]
    out = pl.pallas_call(
        functools.partial(_to_scan_body, nh=nh, bsz=bsz, mix=mu is not None),
        grid=(t // RELAYOUT_ROWS,),
        in_specs=in_specs,
        out_specs=pl.BlockSpec((RELAYOUT_ROWS * SCAN_PITCH, LANES), lambda i: (i, 0)),
        out_shape=jax.ShapeDtypeStruct((t * SCAN_PITCH, LANES), F32),
        scratch_shapes=[pltpu.VMEM((bsz, nh * SCAN_PITCH, RELAYOUT_ROWS), F32)],
        compiler_params=_cparams("parallel"),
        name=name,
    )(*args)
    return out.reshape(1, t, SCAN_PITCH, LANES)


def _from_scan_body(*refs, nh, bsz):
    y_ref, o_ref, yt_ref = refs
    n, tq = HEAD_SIZE, RELAYOUT_ROWS
    for j in range(n):
        mt = y_ref[pl.ds(j, tq, stride=n), :].T
        for b in range(bsz):
            yt_ref[b, pl.ds(j, nh, stride=SCAN_PITCH), :] = mt[b * nh:(b + 1) * nh, :]
    for b in range(bsz):
        yt = jnp.concatenate([yt_ref[b, h * SCAN_PITCH:h * SCAN_PITCH + n, :] for h in range(nh)], axis=0)
        o_ref[b] = yt.T


def _from_scan_bh(y, *, bsz, nh, name):
    t, n, _ = y.shape
    d = nh * n
    assert bsz * nh == LANES and n == HEAD_SIZE and t % RELAYOUT_ROWS == 0
    return pl.pallas_call(
        functools.partial(_from_scan_body, nh=nh, bsz=bsz),
        grid=(t // RELAYOUT_ROWS,),
        in_specs=[pl.BlockSpec((RELAYOUT_ROWS * n, LANES), lambda i: (i, 0))],
        out_specs=pl.BlockSpec((bsz, RELAYOUT_ROWS, d), lambda i: (0, i, 0)),
        out_shape=jax.ShapeDtypeStruct((bsz, t, d), F32),
        scratch_shapes=[pltpu.VMEM((bsz, nh * SCAN_PITCH, RELAYOUT_ROWS), F32)],
        compiler_params=_cparams("parallel"),
        name=name,
    )(y.reshape(t * n, LANES))


def _scan_body(r_ref, k_ref, v_ref, d_ref, a_ref, s0_ref, kk_ref, ka_ref, rk_ref, lw_ref, lb_ref,
               y_ref, s_ref, yrow_ref):
    n = HEAD_SIZE
    tc = r_ref.shape[0]

    @pl.when(pl.program_id(1) == 0)
    def _():
        s_ref[...] = s0_ref[...]

    def colsum(x):
        return jnp.sum(x, axis=0, keepdims=True)

    def step(t, gamma):
        r, k, v, w, a = (x[t, :n, :] for x in (r_ref, k_ref, v_ref, d_ref, a_ref))
        kk = k * kk_ref[...]
        kk = kk / jnp.maximum(jnp.sqrt(colsum(kk * kk)), KK_EPS)
        k2 = k * (1.0 + (a - 1.0) * ka_ref[...])
        kk_n = kk * gamma
        gamma = gamma * w
        inv = 1.0 / gamma
        b_n = kk * a * inv
        k_n = k2 * inv
        r_n = r * gamma
        for i in range(n):
            s_i = s_ref[i]
            s_kk = colsum(s_i * kk_n)
            s_i = s_i - s_kk * b_n + v_ref[t, i:i + 1, :] * k_n
            s_ref[i] = s_i
            yrow_ref[i:i + 1, :] = colsum(s_i * r_n)
        y = yrow_ref[...]
        ym = jnp.mean(y, axis=0, keepdims=True)
        yc = y - ym
        yv = jnp.mean(yc * yc, axis=0, keepdims=True)
        yn = yc * lax.rsqrt(yv + GN_EPS) * lw_ref[...] + lb_ref[...]
        bonus = colsum(r * k2 * rk_ref[...]) * v
        y_ref[t] = yn + bonus
        return gamma

    gamma = lax.fori_loop(0, tc, step, jnp.ones((n, LANES), F32))
    s_ref[...] = s_ref[...] * gamma


def _wkv_scan(r, k, v, d, a, s0, kkp, kap, rkp, lnw, lnb, *, tc, name):
    g, t, pitch, _ = r.shape
    n = HEAD_SIZE
    tc = min(tc, t)
    seq_in = pl.BlockSpec((None, tc, pitch, LANES), lambda gi, ci: (gi, ci, 0, 0))
    seq = pl.BlockSpec((None, tc, n, LANES), lambda gi, ci: (gi, ci, 0, 0))
    par = pl.BlockSpec((None, n, LANES), lambda gi, ci: (gi, 0, 0))
    st = pl.BlockSpec((None, n, n, LANES), lambda gi, ci: (gi, 0, 0, 0))
    return pl.pallas_call(
        _scan_body,
        grid=(g, t // tc),
        in_specs=[seq_in] * 5 + [st] + [par] * 5,
        out_specs=[seq, st],
        out_shape=[jax.ShapeDtypeStruct((g, t, n, LANES), F32),
                   jax.ShapeDtypeStruct((g, n, n, LANES), F32)],
        scratch_shapes=[pltpu.VMEM((n, LANES), F32)],
        compiler_params=_cparams("parallel", "arbitrary"),
        name=name,
    )(r, k, v, d, a, s0, kkp, kap, rkp, lnw, lnb)


def _gate_body(y_ref, zg_ref, prev_ref, mu_ref, g2h_ref, g2l_ref, o_ref, *, fresh):
    nb, tt, lg = zg_ref.shape
    zg = _token_shift(zg_ref, prev_ref, mu_ref, 0, lg, fresh).reshape(nb * tt, lg)
    g = _dot_f32(jax.nn.sigmoid(zg), g2h_ref[...], g2l_ref[...])
    o_ref[...] = (y_ref[...].reshape(nb * tt, -1) * g).reshape(o_ref.shape).astype(o_ref.dtype)


def _rwkv_gate(y3, zl, prevl, mul, g2h, g2l, *, zg_block, nb, tt, fresh, name):
    bsz, t, dr = y3.shape
    lg = g2h.shape[0]
    return pl.pallas_call(
        functools.partial(_gate_body, fresh=fresh),
        grid=(bsz // nb, t // tt),
        in_specs=[pl.BlockSpec((nb, tt, dr), lambda i, j: (i, j, 0))]
        + _shift_specs(zl, prevl, mul, nb=nb, tt=tt, width=lg, cb=zg_block, fresh=fresh)
        + [pl.BlockSpec(g2h.shape, lambda i, j: (0, 0)), pl.BlockSpec(g2l.shape, lambda i, j: (0, 0))],
        out_specs=pl.BlockSpec((nb, tt, dr), lambda i, j: (i, j, 0)),
        out_shape=jax.ShapeDtypeStruct((bsz, t, dr), BF16),
        compiler_params=_cparams("parallel", "parallel"),
        name=name,
    )(y3, zl, prevl, mul, g2h, g2l)


CONV_ROW_CHUNK = 32
CONV_LANE_CHUNK = 512


def _conv_body(u_ref, hist_ref, cw_ref, cb_ref, lw_ref, lb_ref, o_ref, full_ref, fs_ref, c_ref, *,
               hist_rows, taps, fresh):
    nb, tt, dc = u_ref.shape
    hin = hist_ref.shape[1]
    hist = hist_ref[...]
    if fresh:
        hist = jnp.where(pl.program_id(1) == 0, 0.0, hist)
    if hin < hist_rows:
        full_ref[:, :hist_rows - hin, :] = jnp.zeros((nb, hist_rows - hin, dc), F32)
    full_ref[:, hist_rows - hin:hist_rows, :] = hist
    full_ref[:, hist_rows:, :] = u_ref[...]
    lead = hist_rows - (taps - 1)
    span = hist_rows + tt - SUBLANES
    for s in range(1, SUBLANES):
        fs_ref[s - 1] = full_ref[:, s:s + span, :]
    rc = min(CONV_ROW_CHUNK, tt)
    lc = min(CONV_LANE_CHUNK, dc)

    def row_chunk(ri, carry, *, b, c0):
        r0 = pl.multiple_of(ri * rc, rc)
        acc = jnp.broadcast_to(cb_ref[:, c0:c0 + lc], (rc, lc))
        for j in range(taps):
            s = (lead + j) % SUBLANES
            rows = pl.ds(pl.multiple_of(r0 + (lead + j - s), SUBLANES), rc)
            win = full_ref[b, rows, c0:c0 + lc] if s == 0 else fs_ref[s - 1, b, rows, c0:c0 + lc]
            acc = acc + win * cw_ref[j:j + 1, c0:c0 + lc]
        c_ref[b, pl.ds(r0, rc), c0:c0 + lc] = acc
        return carry

    for b in range(nb):
        for c0 in range(0, dc, lc):
            lax.fori_loop(0, tt // rc, functools.partial(row_chunk, b=b, c0=c0), 0)
    acc = c_ref[...]
    mu = jnp.mean(acc, axis=-1, keepdims=True)
    xc = acc - mu
    var = jnp.mean(xc * xc, axis=-1, keepdims=True)
    y = xc * lax.rsqrt(var + LN_EPS) * lw_ref[...] + lb_ref[...]
    o_ref[...] = (y * jax.nn.sigmoid(y)).astype(o_ref.dtype)


def _conv_module(u3, conv0, cw, cb, lnw, lnb, *, nb, tt, name):
    bsz, t, dc = u3.shape
    taps = cw.shape[0]
    hist_rows = _round_up(taps - 1, SUBLANES)
    fresh = conv0 is None
    if fresh:
        assert nb == 1 and tt % hist_rows == 0
        hist, hin = u3, hist_rows
        hist_map = lambda i, j: (i, jnp.maximum(j * (tt // hist_rows) - 1, 0), 0)
    else:
        assert tt == t
        hist, hin = conv0, conv0.shape[1]
        hist_map = lambda i, j: (i, 0, 0)
    row = pl.BlockSpec((1, dc), lambda i, j: (0, 0))
    return pl.pallas_call(
        functools.partial(_conv_body, hist_rows=hist_rows, taps=taps, fresh=fresh),
        grid=(bsz // nb, t // tt),
        in_specs=[pl.BlockSpec((nb, tt, dc), lambda i, j: (i, j, 0)),
                  pl.BlockSpec((nb, hin, dc), hist_map),
                  pl.BlockSpec((taps, dc), lambda i, j: (0, 0)), row, row, row],
        out_specs=pl.BlockSpec((nb, tt, dc), lambda i, j: (i, j, 0)),
        out_shape=jax.ShapeDtypeStruct((bsz, t, dc), BF16),
        scratch_shapes=[pltpu.VMEM((nb, hist_rows + tt, dc), F32),
                        pltpu.VMEM((SUBLANES - 1, nb, hist_rows + tt - SUBLANES, dc), F32),
                        pltpu.VMEM((nb, tt, dc), F32)],
        compiler_params=_cparams("parallel", "parallel"),
        name=name,
    )(u3, hist, cw, cb.reshape(1, dc), lnw.reshape(1, dc), lnb.reshape(1, dc))


def _split_hi_lo(w, rows):
    w = jnp.pad(w.astype(F32), ((0, rows - w.shape[0]), (0, 0)))
    hi = w.astype(BF16)
    return hi, (w - hi.astype(F32)).astype(BF16)


def _layer(h, pe, wkv0, shift0, conv0, lanes_are_batch_head, p):
    bsz, t, d = h.shape
    m = bsz * t
    dr, dc, n = p["dr"], p["dc"], HEAD_SIZE
    nh = dr // n
    hbuf = conv0.shape[1] if conv0 is not None else p["conv_w"].shape[0] - 1
    x2 = h.reshape(m, d)
    fresh = shift0 is None

    xn = _rmsnorm(x2, p["g_mix"], out_dtype=BF16, name="mix_norm")
    in_proj = functools.partial(_matmul, w_transposed=True)
    zr = in_proj([xn], [p["w_in_t"]], n_out=3 * dr, **_tiles("in_proj_rkv")).reshape(bsz, t, 3 * dr)
    zl = in_proj([xn], [p["w_lora_t"]], **_tiles("in_proj_lora")).reshape(bsz, t, -1)
    tg = _pick_tile(dc, TILES["in_proj_glu"][1])
    u = in_proj([xn], [p["w_in_c_t"], p["w_in_c_t"]], epi="glu", n_out=dc, w_blocks=[(0, 0), (0, dc // tg)],
                **_tiles("in_proj_glu"))
    new_shift = _rmsnorm(h[:, -1, :], p["g_mix"], name="shift_norm")

    if fresh:
        tt, nb = 256, 1
        prevr, prevl = zr, zl
    else:
        tt, nb = t, 32
        s0b = shift0.astype(BF16)
        prevr = in_proj([s0b], [p["w_in_t"]], n_out=3 * dr, **_tiles("shift_proj")).reshape(bsz, 1, 3 * dr)
        prevl = in_proj([s0b], [p["w_lora_t"]], **_tiles("shift_lora")).reshape(bsz, 1, -1)
    prep = functools.partial(_rwkv_prep, zr, prevr, p["mu_rkv"], zl, prevl, p["mu_lora"], p["w0"], p["w2h"], p["w2l"],
                             p["a0"], p["a2h"], p["a2l"], nb=nb, tt=tt, fresh=fresh, name="rwkv_prep")

    if lanes_are_batch_head:
        assert fresh
        dec, a = prep(with_rkv=False)
        scan_in = [_to_scan_bh(zr, nh=nh, col_block=c, mu=p["mu_rkv"], name="mix_to_scan") for c in range(3)]
        scan_in += [_to_scan_bh(x, nh=nh, name="to_scan") for x in (dec, a)]
        par = lambda q: jnp.tile(q.reshape(nh, n).T, (1, bsz))[None]
        s0 = jnp.zeros((1, n, n, LANES), F32)
    else:
        to_scan = lambda x: jnp.transpose(x.reshape(bsz, t, nh, n), (2, 1, 3, 0))
        scan_in = [to_scan(x) for x in prep(with_rkv=True)]
        par = lambda q: jnp.broadcast_to(q.reshape(nh, n)[:, :, None], (nh, n, LANES))
        s0 = jnp.transpose(wkv0, (1, 2, 3, 0))
    y_s, s_new = _wkv_scan(*scan_in, s0, par(p["k_k"]), par(p["k_a"]), par(p["r_k"]), par(p["lnx_w"]),
                           par(p["lnx_b"]), tc=32, name="wkv_scan")
    if lanes_are_batch_head:
        y_tok = _from_scan_bh(y_s[0], bsz=bsz, nh=nh, name="from_scan")
        new_wkv = jnp.transpose(s_new.reshape(n, n, bsz, nh), (2, 3, 0, 1))
    else:
        y_tok = jnp.transpose(y_s, (3, 1, 0, 2)).reshape(bsz, t, dr)
        new_wkv = jnp.transpose(s_new, (3, 0, 1, 2))
    y_rwkv = _rwkv_gate(y_tok, zl, prevl, p["mu_lora"], p["g2h"], p["g2l"], zg_block=p["zg_block"], nb=nb, tt=tt,
                        fresh=fresh, name="rwkv_gate").reshape(m, dr)


    u3 = u.reshape(bsz, t, dc)
    if conv0 is None:
        new_conv = u3[:, t - hbuf:, :]
        cnb = 1
    else:
        new_conv = jnp.concatenate([conv0, u3], axis=1)[:, -hbuf:, :]
        cnb = 4
    y_conv = _conv_module(u3, conv0, p["conv_w"], p["conv_b"], p["conv_ln_w"], p["conv_ln_b"], nb=cnb, tt=tt,
                          name="conv_module").reshape(m, dc)

    assert dr == dc
    h1 = _matmul([y_rwkv, y_conv], [p["w_out"], p["w_out"]], w_blocks=[(0, 0), (1, 0)], res=x2, epi="res",
                 **_tiles("out_proj"))
    dff = p["w_down"].shape[0]
    tf = _pick_tile(dff, TILES["ffn_gate_up"][1])
    h1n = _rmsnorm(h1, p["g_ffn"], out_dtype=BF16, name="ffn_norm")
    act = _matmul([h1n], [p["w_gate_up"], p["w_gate_up"]], epi="swiglu", n_out=dff, out_dtype=BF16,
                  w_blocks=[(0, 0), (0, dff // tf)], **_tiles("ffn_gate_up"))
    h2 = _matmul([act], [p["w_down"]], res=h1, epi="res", **_tiles("ffn_down"))
    h2n = _rmsnorm(h2, p["g_ple"], out_dtype=BF16, name="ple_norm")
    h3 = _matmul([h2n, pe.reshape(m, -1)], [p["w_ple_gate"], p["w_ple_proj"]], res=h2, epi="ple", **_tiles("ple"))
    return h3.reshape(bsz, t, d), new_wkv, new_shift, new_conv


def _prep_params(i, g_mix, w_in, mu_shift, w0, w2, a0, a2, g2, k_k, k_a, r_k, lnx_w, lnx_b, conv_w, conv_b,
                 conv_ln_w, conv_ln_b, w_out, g_ffn, w_gate_up, w_down, g_ple, w_ple_gate, w_ple_proj):
    dr = w0.shape[-1]
    dc = conv_w.shape[-1]
    lw, la, lg = w2.shape[1], a2.shape[1], g2.shape[1]
    lwp, lap, lgp = (_round_up(x, LANES) for x in (lw, la, lg))

    def lora_groups(x):
        pad = lambda y, to: jnp.pad(y, [(0, to - y.shape[0])] + [(0, 0)] * (y.ndim - 1))
        return jnp.concatenate([pad(x[:lw], lwp), pad(x[lw:lw + la], lap), pad(x[lw + la:lw + la + lg], lgp)], axis=0)

    lora_n = lw + la + lg
    wt = jnp.swapaxes(w_in[i], 0, 1)
    wt_tail = wt[3 * dr:]
    w2h, w2l = _split_hi_lo(w2[i], lwp)
    a2h, a2l = _split_hi_lo(a2[i], lap)
    g2h, g2l = _split_hi_lo(g2[i], lgp)
    assert (lwp + lap) % lgp == 0
    return dict(
        dr=dr, dc=dc,
        g_mix=g_mix[i], w_in_t=wt, w_lora_t=lora_groups(wt_tail[:lora_n]).astype(BF16),
        w_in_c_t=wt_tail[lora_n:].astype(BF16),
        mu_rkv=mu_shift[i][None, :3 * dr], mu_lora=lora_groups(mu_shift[i][3 * dr:])[None, :],
        zg_block=(lwp + lap) // lgp,
        w0=w0[i][None, :], w2h=w2h, w2l=w2l, a0=a0[i][None, :], a2h=a2h, a2l=a2l, g2h=g2h, g2l=g2l,
        k_k=k_k[i], k_a=k_a[i], r_k=r_k[i], lnx_w=lnx_w[i], lnx_b=lnx_b[i],
        conv_w=conv_w[i], conv_b=conv_b[i], conv_ln_w=conv_ln_w[i], conv_ln_b=conv_ln_b[i],
        w_out=w_out[i].astype(BF16), g_ffn=g_ffn[i], w_gate_up=w_gate_up[i], w_down=w_down[i].astype(BF16),
        g_ple=g_ple[i], w_ple_gate=w_ple_gate[i].astype(BF16), w_ple_proj=w_ple_proj[i].astype(BF16),
    )


def kernel(x_prompt, x_sample, state_wkv, state_shift, state_conv, p_prompt, p_sample, g_mix, w_in, mu_shift, w0, w2, a0, a2, g2, k_k, k_a, r_k, lnx_w, lnx_b, conv_w, conv_b, conv_ln_w, conv_ln_b, w_out, g_ffn, w_gate_up, w_down, g_ple, w_ple_gate, w_ple_proj, g_final):
    depth = w_in.shape[0]
    layer_params = (g_mix, w_in, mu_shift, w0, w2, a0, a2, g2, k_k, k_a, r_k, lnx_w, lnx_b, conv_w, conv_b,
                    conv_ln_w, conv_ln_b, w_out, g_ffn, w_gate_up, w_down, g_ple, w_ple_gate, w_ple_proj)
    hp, hs = x_prompt, x_sample
    outs = [[] for _ in range(6)]
    for i in range(depth):
        p = _prep_params(i, *layer_params)
        hp, s_w, s_sh, s_c = _layer(hp, p_prompt[i], None, None, None, True, p)
        outs[0].append(s_w); outs[1].append(s_sh); outs[2].append(s_c)
        hs, s_w, s_sh, s_c = _layer(hs, p_sample[i], state_wkv[i], state_shift[i], state_conv[i], False, p)
        outs[3].append(s_w); outs[4].append(s_sh); outs[5].append(s_c)
    d = hp.shape[-1]
    y_prompt = _rmsnorm(hp.reshape(-1, d), g_final, name="final_norm").reshape(hp.shape)
    y_sample = _rmsnorm(hs.reshape(-1, d), g_final, name="final_norm").reshape(hs.shape)
    return (y_prompt, y_sample) + tuple(jnp.stack(o) for o in outs)
```

```python
import functools

import jax
import jax.numpy as jnp
from jax import lax
from jax.experimental import pallas as pl
from jax.experimental.pallas import tpu as pltpu

HEAD_SIZE = 64
RMS_EPS = 1e-6
LN_EPS = 1e-5
GN_EPS = 64e-5
KK_EPS = 1e-12
LANES = 128
SUBLANES = 8
VMEM_LIMIT_BYTES = 56 * 2 ** 20

F32 = jnp.float32
BF16 = jnp.bfloat16


TILES = dict(in_proj_rkv=(2048, 512, 1), in_proj_lora=(1024, 512, 2), shift_proj=(128, 512, 2),
             shift_lora=(128, 512, 2), in_proj_glu=(2048, 256, 1), out_proj=(1024, 512, 2),
             ffn_gate_up=(2048, 256, 1), ffn_down=(512, 512, 2), ple=(1024, 512, 2))


def _tiles(name):
    tm, tn, x_buffers = TILES[name]
    return dict(tm=tm, tn=tn, x_buffers=x_buffers, name=name)


def _cparams(*sem):
    return pltpu.CompilerParams(dimension_semantics=sem, vmem_limit_bytes=VMEM_LIMIT_BYTES)


def _round_up(n, m):
    return (n + m - 1) // m * m


def _pick_tile(n, pref):
    if n <= pref:
        return n
    t = pref - pref % LANES
    while t > LANES and n % t:
        t -= LANES
    assert n % t == 0, (n, pref)
    return t


def _mm_body(*refs, n_x, n_w, epi, w_transposed):
    refs = list(refs)
    x_refs = [refs.pop(0) for _ in range(n_x)]
    w_refs = [refs.pop(0) for _ in range(n_w)]
    res_ref = refs.pop(0) if epi in ("res", "ple") else None
    out_ref = refs.pop(0)

    def dot(x_ref, w_ref):
        contract = (((1,), (1 if w_transposed else 0,)), ((), ()))
        return lax.dot_general(x_ref[...].astype(BF16), w_ref[...].astype(BF16), contract,
                               preferred_element_type=F32)

    if epi in ("store", "res"):
        acc = dot(x_refs[0], w_refs[0])
        for xr, wr in zip(x_refs[1:], w_refs[1:]):
            acc = acc + dot(xr, wr)
        if epi == "res":
            acc = res_ref[...] + acc
        out_ref[...] = acc.astype(out_ref.dtype)
    elif epi == "glu":
        out_ref[...] = (dot(x_refs[0], w_refs[0]) * jax.nn.sigmoid(dot(x_refs[0], w_refs[1]))).astype(out_ref.dtype)
    elif epi == "swiglu":
        gate = dot(x_refs[0], w_refs[0])
        out_ref[...] = (gate * jax.nn.sigmoid(gate) * dot(x_refs[0], w_refs[1])).astype(out_ref.dtype)
    elif epi == "ple":
        pg = jax.nn.sigmoid(dot(x_refs[0], w_refs[0]))
        out_ref[...] = (res_ref[...] + pg * dot(x_refs[1], w_refs[1])).astype(out_ref.dtype)
    else:
        raise ValueError(epi)


def _matmul(xs, ws, *, res=None, epi="store", out_dtype=F32, n_out=None, w_blocks=None, w_transposed=False,
            w_row0=None, tm, tn, x_buffers, name):
    m = xs[0].shape[0]
    n_out = ws[0].shape[0 if w_transposed else 1] if n_out is None else n_out
    tm = _pick_tile(m, tm)
    tn = _pick_tile(n_out, tn)
    w_blocks = [(0, 0)] * len(ws) if w_blocks is None else w_blocks
    in_specs = [pl.BlockSpec((tm, x.shape[1]), lambda i, j: (i, 0), pipeline_mode=pl.Buffered(x_buffers))
                for x in xs]
    for idx, (rb, cb) in enumerate(w_blocks):
        k = xs[min(idx, len(xs) - 1)].shape[1]
        if w_transposed and w_row0 is not None:
            assert rb == 0 and cb == 0 and w_row0[idx] % SUBLANES == 0
            in_specs.append(pl.BlockSpec((pl.Element(tn), pl.Element(k)),
                                         functools.partial(lambda i, j, r0: (pl.multiple_of(r0 + j * tn, SUBLANES), 0),
                                                           r0=w_row0[idx])))
        elif w_transposed:
            in_specs.append(pl.BlockSpec((tn, k), functools.partial(lambda i, j, rb, cb: (j + cb, rb), rb=rb, cb=cb)))
        else:
            in_specs.append(pl.BlockSpec((k, tn), functools.partial(lambda i, j, rb, cb: (rb, j + cb), rb=rb, cb=cb)))
    if res is not None:
        in_specs.append(pl.BlockSpec((tm, tn), lambda i, j: (i, j)))
    return pl.pallas_call(
        functools.partial(_mm_body, n_x=len(xs), n_w=len(ws), epi=epi, w_transposed=w_transposed),
        grid=(m // tm, n_out // tn),
        in_specs=in_specs,
        out_specs=pl.BlockSpec((tm, tn), lambda i, j: (i, j)),
        out_shape=jax.ShapeDtypeStruct((m, n_out), out_dtype),
        compiler_params=_cparams("parallel", "arbitrary"),
        name=name,
    )(*xs, *ws, *([] if res is None else [res]))


def _rms_body(x_ref, g_ref, o_ref):
    x = x_ref[...].astype(F32)
    ms = jnp.mean(x * x, axis=-1, keepdims=True)
    o_ref[...] = (x * lax.rsqrt(ms + RMS_EPS) * g_ref[...]).astype(o_ref.dtype)


def _rmsnorm(x, gain, *, out_dtype=F32, tm=512, name="rmsnorm"):
    m, d = x.shape
    tm = _pick_tile(m, tm)
    return pl.pallas_call(
        _rms_body,
        grid=(m // tm,),
        in_specs=[pl.BlockSpec((tm, d), lambda i: (i, 0)), pl.BlockSpec((1, d), lambda i: (0, 0))],
        out_specs=pl.BlockSpec((tm, d), lambda i: (i, 0)),
        out_shape=jax.ShapeDtypeStruct((m, d), out_dtype),
        compiler_params=_cparams("parallel"),
        name=name,
    )(x, gain.reshape(1, d).astype(F32))


def _split3(x):
    hi = x.astype(BF16)
    r1 = x - hi.astype(F32)
    mid = r1.astype(BF16)
    lo = (r1 - mid.astype(F32)).astype(BF16)
    return hi, mid, lo


def _dot_f32(a, b_hi, b_lo):
    a_hi, a_mid, _ = _split3(a)
    dot = lambda p, q: jnp.dot(p, q, preferred_element_type=F32)
    return dot(a_hi, b_hi) + (dot(a_mid, b_hi) + dot(a_hi, b_lo))


def _token_shift(z_ref, prev_ref, mu_ref, lo, width, fresh):
    nb, tt, _ = z_ref.shape
    z = z_ref[:, :, lo:lo + width]
    first = prev_ref[:, prev_ref.shape[1] - 1:, lo:lo + width]
    if fresh:
        first = jnp.where(pl.program_id(1) == 0, 0.0, first)
    tpos = lax.broadcasted_iota(jnp.int32, (nb, tt, 1), 1)
    zp = jnp.where(tpos == 0, first, pltpu.roll(z, 1, axis=1))
    return z + (zp - z) * mu_ref[:, lo:lo + width]


def _shift_specs(z, prev, mu, *, nb, tt, width, cb, fresh):
    if fresh:
        assert prev is z and tt % SUBLANES == 0
        prev_spec = pl.BlockSpec((nb, SUBLANES, width),
                                 lambda i, j: (i, jnp.maximum(j * (tt // SUBLANES) - 1, 0), cb))
    else:
        prev_spec = pl.BlockSpec((nb, 1, width), lambda i, j: (i, 0, cb))
    return [pl.BlockSpec((nb, tt, width), lambda i, j: (i, j, cb)), prev_spec,
            pl.BlockSpec((1, width), lambda i, j: (0, cb))]


def _prep_body(*refs, dr, lw, la, with_rkv, fresh):
    if with_rkv:
        zr_ref, prevr_ref, mur_ref = refs[:3]
        refs = refs[3:]
    zl_ref, prevl_ref, mul_ref, w0_ref, w2h_ref, w2l_ref, a0_ref, a2h_ref, a2l_ref = refs[:9]
    out_refs = refs[9:]
    nb, tt, _ = zl_ref.shape
    if with_rkv:
        for c, o_ref in enumerate(out_refs[:3]):
            o_ref[...] = _token_shift(zr_ref, prevr_ref, mur_ref, c * dr, dr, fresh)
    d_ref, a_ref = out_refs[-2:]
    zw = _token_shift(zl_ref, prevl_ref, mul_ref, 0, lw, fresh).reshape(nb * tt, lw)
    za = _token_shift(zl_ref, prevl_ref, mul_ref, lw, la, fresh).reshape(nb * tt, la)
    xw = w0_ref[...] + _dot_f32(jnp.tanh(zw), w2h_ref[...], w2l_ref[...])
    w_log = jnp.minimum(xw, 0.0) - jnp.log1p(jnp.exp(-jnp.abs(xw))) - 0.5
    d_ref[...] = jnp.exp(-jnp.exp(w_log)).reshape(nb, tt, dr)
    xa = a0_ref[...] + _dot_f32(za, a2h_ref[...], a2l_ref[...])
    a_ref[...] = jax.nn.sigmoid(xa).reshape(nb, tt, dr)


def _rwkv_prep(zr, prevr, mur, zl, prevl, mul, w0, w2h, w2l, a0, a2h, a2l, *, nb, tt, with_rkv, fresh, name):
    bsz, t, _ = zl.shape
    dr = w0.shape[-1]
    lw, la = w2h.shape[0], a2h.shape[0]
    row = lambda c: pl.BlockSpec((1, c), lambda i, j: (0, 0))
    full = lambda a: pl.BlockSpec(a.shape, lambda i, j: (0, 0))
    shift = functools.partial(_shift_specs, nb=nb, tt=tt, cb=0, fresh=fresh)
    in_specs, args = [], []
    if with_rkv:
        in_specs += shift(zr, prevr, mur, width=zr.shape[-1])
        args += [zr, prevr, mur]
    in_specs += shift(zl, prevl, mul, width=zl.shape[-1]) + [row(dr), full(w2h), full(w2l), row(dr), full(a2h),
                                                            full(a2l)]
    args += [zl, prevl, mul, w0, w2h, w2l, a0, a2h, a2l]
    n_out = 5 if with_rkv else 2
    return pl.pallas_call(
        functools.partial(_prep_body, dr=dr, lw=lw, la=la, with_rkv=with_rkv, fresh=fresh),
        grid=(bsz // nb, t // tt),
        in_specs=in_specs,
        out_specs=[pl.BlockSpec((nb, tt, dr), lambda i, j: (i, j, 0))] * n_out,
        out_shape=[jax.ShapeDtypeStruct((bsz, t, dr), F32)] * n_out,
        compiler_params=_cparams("parallel", "parallel"),
        name=name,
    )(*args)


RELAYOUT_ROWS = 128
SCAN_PITCH = HEAD_SIZE + SUBLANES


def _to_scan_body(*refs, nh, bsz, mix):
    x_refs, refs = refs[:bsz], refs[bsz:]
    if mix:
        prev_refs, mu_ref, refs = refs[:bsz], refs[bsz], refs[bsz + 1:]
    o_ref, xt_ref = refs
    n, tq = HEAD_SIZE, RELAYOUT_ROWS
    for b in range(bsz):
        x = x_refs[b][0]
        if mix:
            first = jnp.where(pl.program_id(0) == 0, 0.0, prev_refs[b][0, SUBLANES - 1:, :])
            row = lax.broadcasted_iota(jnp.int32, (tq, 1), 0)
            xp = jnp.where(row == 0, first, pltpu.roll(x, 1, axis=0))
            x = x + (xp - x) * mu_ref[...]
        xt = x.T
        for h in range(nh):
            xt_ref[b, h * SCAN_PITCH:h * SCAN_PITCH + n, :] = xt[h * n:(h + 1) * n, :]
    for j in range(n):
        m = jnp.concatenate([xt_ref[b, pl.ds(j, nh, stride=SCAN_PITCH), :] for b in range(bsz)], axis=0)
        o_ref[pl.ds(j, tq, stride=SCAN_PITCH), :] = m.T
    for j in range(n, SCAN_PITCH):
        o_ref[pl.ds(j, tq, stride=SCAN_PITCH), :] = jnp.zeros((tq, LANES), F32)


def _to_scan_bh(x3, *, nh, name, col_block=0, mu=None):
    bsz, t, _ = x3.shape
    d = nh * HEAD_SIZE
    assert bsz * nh == LANES and t % RELAYOUT_ROWS == 0
    in_specs = [pl.BlockSpec((1, RELAYOUT_ROWS, d), functools.partial(lambda i, b: (b, i, col_block), b=b))
                for b in range(bsz)]
    args = [x3] * bsz
    if mu is not None:
        per = RELAYOUT_ROWS // SUBLANES
        in_specs += [pl.BlockSpec((1, SUBLANES, d),
                                  functools.partial(lambda i, b: (b, jnp.maximum(i * per - 1, 0), col_block), b=b))
                     for b in range(bsz)] + [pl.BlockSpec((1, d), lambda i: (0, col_block))]
        args += [x3] * bsz + [mu]
    out = pl.pallas_call(
        functools.partial(_to_scan_body, nh=nh, bsz=bsz, mix=mu is not None),
        grid=(t // RELAYOUT_ROWS,),
        in_specs=in_specs,
        out_specs=pl.BlockSpec((RELAYOUT_ROWS * SCAN_PITCH, LANES), lambda i: (i, 0)),
        out_shape=jax.ShapeDtypeStruct((t * SCAN_PITCH, LANES), F32),
        scratch_shapes=[pltpu.VMEM((bsz, nh * SCAN_PITCH, RELAYOUT_ROWS), F32)],
        compiler_params=_cparams("parallel"),
        name=name,
    )(*args)
    return out.reshape(1, t, SCAN_PITCH, LANES)


def _from_scan_body(*refs, nh, bsz):
    y_ref, o_ref, yt_ref = refs
    n, tq = HEAD_SIZE, RELAYOUT_ROWS
    for j in range(n):
        mt = y_ref[pl.ds(j, tq, stride=n), :].T
        for b in range(bsz):
            yt_ref[b, pl.ds(j, nh, stride=SCAN_PITCH), :] = mt[b * nh:(b + 1) * nh, :]
    for b in range(bsz):
        yt = jnp.concatenate([yt_ref[b, h * SCAN_PITCH:h * SCAN_PITCH + n, :] for h in range(nh)], axis=0)
        o_ref[b] = yt.T


def _from_scan_bh(y, *, bsz, nh, name):
    t, n, _ = y.shape
    d = nh * n
    assert bsz * nh == LANES and n == HEAD_SIZE and t % RELAYOUT_ROWS == 0
    return pl.pallas_call(
        functools.partial(_from_scan_body, nh=nh, bsz=bsz),
        grid=(t // RELAYOUT_ROWS,),
        in_specs=[pl.BlockSpec((RELAYOUT_ROWS * n, LANES), lambda i: (i, 0))],
        out_specs=pl.BlockSpec((bsz, RELAYOUT_ROWS, d), lambda i: (0, i, 0)),
        out_shape=jax.ShapeDtypeStruct((bsz, t, d), F32),
        scratch_shapes=[pltpu.VMEM((bsz, nh * SCAN_PITCH, RELAYOUT_ROWS), F32)],
        compiler_params=_cparams("parallel"),
        name=name,
    )(y.reshape(t * n, LANES))


def _scan_body(r_ref, k_ref, v_ref, d_ref, a_ref, s0_ref, kk_ref, ka_ref, rk_ref, lw_ref, lb_ref,
               y_ref, s_ref, yrow_ref):
    n = HEAD_SIZE
    tc = r_ref.shape[0]

    @pl.when(pl.program_id(1) == 0)
    def _():
        s_ref[...] = s0_ref[...]

    def colsum(x):
        return jnp.sum(x, axis=0, keepdims=True)

    def step(t, gamma):
        r, k, v, w, a = (x[t, :n, :] for x in (r_ref, k_ref, v_ref, d_ref, a_ref))
        kk = k * kk_ref[...]
        kk = kk / jnp.maximum(jnp.sqrt(colsum(kk * kk)), KK_EPS)
        k2 = k * (1.0 + (a - 1.0) * ka_ref[...])
        kk_n = kk * gamma
        gamma = gamma * w
        inv = 1.0 / gamma
        b_n = kk * a * inv
        k_n = k2 * inv
        r_n = r * gamma
        for i in range(n):
            s_i = s_ref[i]
            s_kk = colsum(s_i * kk_n)
            s_i = s_i - s_kk * b_n + v_ref[t, i:i + 1, :] * k_n
            s_ref[i] = s_i
            yrow_ref[i:i + 1, :] = colsum(s_i * r_n)
        y = yrow_ref[...]
        ym = jnp.mean(y, axis=0, keepdims=True)
        yc = y - ym
        yv = jnp.mean(yc * yc, axis=0, keepdims=True)
        yn = yc * lax.rsqrt(yv + GN_EPS) * lw_ref[...] + lb_ref[...]
        bonus = colsum(r * k2 * rk_ref[...]) * v
        y_ref[t] = yn + bonus
        return gamma

    gamma = lax.fori_loop(0, tc, step, jnp.ones((n, LANES), F32))
    s_ref[...] = s_ref[...] * gamma


def _wkv_scan(r, k, v, d, a, s0, kkp, kap, rkp, lnw, lnb, *, tc, name):
    g, t, pitch, _ = r.shape
    n = HEAD_SIZE
    tc = min(tc, t)
    seq_in = pl.BlockSpec((None, tc, pitch, LANES), lambda gi, ci: (gi, ci, 0, 0))
    seq = pl.BlockSpec((None, tc, n, LANES), lambda gi, ci: (gi, ci, 0, 0))
    par = pl.BlockSpec((None, n, LANES), lambda gi, ci: (gi, 0, 0))
    st = pl.BlockSpec((None, n, n, LANES), lambda gi, ci: (gi, 0, 0, 0))
    return pl.pallas_call(
        _scan_body,
        grid=(g, t // tc),
        in_specs=[seq_in] * 5 + [st] + [par] * 5,
        out_specs=[seq, st],
        out_shape=[jax.ShapeDtypeStruct((g, t, n, LANES), F32),
                   jax.ShapeDtypeStruct((g, n, n, LANES), F32)],
        scratch_shapes=[pltpu.VMEM((n, LANES), F32)],
        compiler_params=_cparams("parallel", "arbitrary"),
        name=name,
    )(r, k, v, d, a, s0, kkp, kap, rkp, lnw, lnb)


def _gate_body(y_ref, zg_ref, prev_ref, mu_ref, g2h_ref, g2l_ref, o_ref, *, fresh):
    nb, tt, lg = zg_ref.shape
    zg = _token_shift(zg_ref, prev_ref, mu_ref, 0, lg, fresh).reshape(nb * tt, lg)
    g = _dot_f32(jax.nn.sigmoid(zg), g2h_ref[...], g2l_ref[...])
    o_ref[...] = (y_ref[...].reshape(nb * tt, -1) * g).reshape(o_ref.shape).astype(o_ref.dtype)


def _rwkv_gate(y3, zl, prevl, mul, g2h, g2l, *, zg_block, nb, tt, fresh, name):
    bsz, t, dr = y3.shape
    lg = g2h.shape[0]
    return pl.pallas_call(
        functools.partial(_gate_body, fresh=fresh),
        grid=(bsz // nb, t // tt),
        in_specs=[pl.BlockSpec((nb, tt, dr), lambda i, j: (i, j, 0))]
        + _shift_specs(zl, prevl, mul, nb=nb, tt=tt, width=lg, cb=zg_block, fresh=fresh)
        + [pl.BlockSpec(g2h.shape, lambda i, j: (0, 0)), pl.BlockSpec(g2l.shape, lambda i, j: (0, 0))],
        out_specs=pl.BlockSpec((nb, tt, dr), lambda i, j: (i, j, 0)),
        out_shape=jax.ShapeDtypeStruct((bsz, t, dr), BF16),
        compiler_params=_cparams("parallel", "parallel"),
        name=name,
    )(y3, zl, prevl, mul, g2h, g2l)


CONV_ROW_CHUNK = 32
CONV_LANE_CHUNK = 512


def _conv_body(u_ref, hist_ref, cw_ref, cb_ref, lw_ref, lb_ref, o_ref, full_ref, fs_ref, c_ref, *,
               hist_rows, taps, fresh):
    nb, tt, dc = u_ref.shape
    hin = hist_ref.shape[1]
    hist = hist_ref[...]
    if fresh:
        hist = jnp.where(pl.program_id(1) == 0, 0.0, hist)
    if hin < hist_rows:
        full_ref[:, :hist_rows - hin, :] = jnp.zeros((nb, hist_rows - hin, dc), F32)
    full_ref[:, hist_rows - hin:hist_rows, :] = hist
    full_ref[:, hist_rows:, :] = u_ref[...]
    lead = hist_rows - (taps - 1)
    span = hist_rows + tt - SUBLANES
    for s in range(1, SUBLANES):
        fs_ref[s - 1] = full_ref[:, s:s + span, :]
    rc = min(CONV_ROW_CHUNK, tt)
    lc = min(CONV_LANE_CHUNK, dc)

    def row_chunk(ri, carry, *, b, c0):
        r0 = pl.multiple_of(ri * rc, rc)
        acc = jnp.broadcast_to(cb_ref[:, c0:c0 + lc], (rc, lc))
        for j in range(taps):
            s = (lead + j) % SUBLANES
            rows = pl.ds(pl.multiple_of(r0 + (lead + j - s), SUBLANES), rc)
            win = full_ref[b, rows, c0:c0 + lc] if s == 0 else fs_ref[s - 1, b, rows, c0:c0 + lc]
            acc = acc + win * cw_ref[j:j + 1, c0:c0 + lc]
        c_ref[b, pl.ds(r0, rc), c0:c0 + lc] = acc
        return carry

    for b in range(nb):
        for c0 in range(0, dc, lc):
            lax.fori_loop(0, tt // rc, functools.partial(row_chunk, b=b, c0=c0), 0)
    acc = c_ref[...]
    mu = jnp.mean(acc, axis=-1, keepdims=True)
    xc = acc - mu
    var = jnp.mean(xc * xc, axis=-1, keepdims=True)
    y = xc * lax.rsqrt(var + LN_EPS) * lw_ref[...] + lb_ref[...]
    o_ref[...] = (y * jax.nn.sigmoid(y)).astype(o_ref.dtype)


def _conv_module(u3, conv0, cw, cb, lnw, lnb, *, nb, tt, name):
    bsz, t, dc = u3.shape
    taps = cw.shape[0]
    hist_rows = _round_up(taps - 1, SUBLANES)
    fresh = conv0 is None
    if fresh:
        assert nb == 1 and tt % hist_rows == 0
        hist, hin = u3, hist_rows
        hist_map = lambda i, j: (i, jnp.maximum(j * (tt // hist_rows) - 1, 0), 0)
    else:
        assert tt == t
        hist, hin = conv0, conv0.shape[1]
        hist_map = lambda i, j: (i, 0, 0)
    row = pl.BlockSpec((1, dc), lambda i, j: (0, 0))
    return pl.pallas_call(
        functools.partial(_conv_body, hist_rows=hist_rows, taps=taps, fresh=fresh),
        grid=(bsz // nb, t // tt),
        in_specs=[pl.BlockSpec((nb, tt, dc), lambda i, j: (i, j, 0)),
                  pl.BlockSpec((nb, hin, dc), hist_map),
                  pl.BlockSpec((taps, dc), lambda i, j: (0, 0)), row, row, row],
        out_specs=pl.BlockSpec((nb, tt, dc), lambda i, j: (i, j, 0)),
        out_shape=jax.ShapeDtypeStruct((bsz, t, dc), BF16),
        scratch_shapes=[pltpu.VMEM((nb, hist_rows + tt, dc), F32),
                        pltpu.VMEM((SUBLANES - 1, nb, hist_rows + tt - SUBLANES, dc), F32),
                        pltpu.VMEM((nb, tt, dc), F32)],
        compiler_params=_cparams("parallel", "parallel"),
        name=name,
    )(u3, hist, cw, cb.reshape(1, dc), lnw.reshape(1, dc), lnb.reshape(1, dc))


def _split_hi_lo(w, rows):
    w = jnp.pad(w.astype(F32), ((0, rows - w.shape[0]), (0, 0)))
    hi = w.astype(BF16)
    return hi, (w - hi.astype(F32)).astype(BF16)


def _layer(h, pe, wkv0, shift0, conv0, lanes_are_batch_head, p):
    bsz, t, d = h.shape
    m = bsz * t
    dr, dc, n = p["dr"], p["dc"], HEAD_SIZE
    nh = dr // n
    hbuf = conv0.shape[1] if conv0 is not None else p["conv_w"].shape[0] - 1
    x2 = h.reshape(m, d)
    fresh = shift0 is None

    xn = _rmsnorm(x2, p["g_mix"], out_dtype=BF16, name="mix_norm")
    in_proj = functools.partial(_matmul, w_transposed=True)
    zr = in_proj([xn], [p["w_in_t"]], n_out=3 * dr, **_tiles("in_proj_rkv")).reshape(bsz, t, 3 * dr)
    zl = in_proj([xn], [p["w_lora_t"]], **_tiles("in_proj_lora")).reshape(bsz, t, -1)
    c0 = p["conv_col0"]
    u = in_proj([xn], [p["w_in_t"], p["w_in_t"]], epi="glu", n_out=dc, w_row0=[c0, c0 + dc], **_tiles("in_proj_glu"))
    new_shift = _rmsnorm(h[:, -1, :], p["g_mix"], name="shift_norm")

    if fresh:
        tt, nb = 256, 1
        prevr, prevl = zr, zl
    else:
        tt, nb = t, 32
        s0b = shift0.astype(BF16)
        prevr = in_proj([s0b], [p["w_in_t"]], n_out=3 * dr, **_tiles("shift_proj")).reshape(bsz, 1, 3 * dr)
        prevl = in_proj([s0b], [p["w_lora_t"]], **_tiles("shift_lora")).reshape(bsz, 1, -1)
    prep = functools.partial(_rwkv_prep, zr, prevr, p["mu_rkv"], zl, prevl, p["mu_lora"], p["w0"], p["w2h"], p["w2l"],
                             p["a0"], p["a2h"], p["a2l"], nb=nb, tt=tt, fresh=fresh, name="rwkv_prep")

    if lanes_are_batch_head:
        assert fresh
        dec, a = prep(with_rkv=False)
        scan_in = [_to_scan_bh(zr, nh=nh, col_block=c, mu=p["mu_rkv"], name="mix_to_scan") for c in range(3)]
        scan_in += [_to_scan_bh(x, nh=nh, name="to_scan") for x in (dec, a)]
        par = lambda q: jnp.tile(q.reshape(nh, n).T, (1, bsz))[None]
        s0 = jnp.zeros((1, n, n, LANES), F32)
    else:
        to_scan = lambda x: jnp.transpose(x.reshape(bsz, t, nh, n), (2, 1, 3, 0))
        scan_in = [to_scan(x) for x in prep(with_rkv=True)]
        par = lambda q: jnp.broadcast_to(q.reshape(nh, n)[:, :, None], (nh, n, LANES))
        s0 = jnp.transpose(wkv0, (1, 2, 3, 0))
    y_s, s_new = _wkv_scan(*scan_in, s0, par(p["k_k"]), par(p["k_a"]), par(p["r_k"]), par(p["lnx_w"]),
                           par(p["lnx_b"]), tc=32, name="wkv_scan")
    if lanes_are_batch_head:
        y_tok = _from_scan_bh(y_s[0], bsz=bsz, nh=nh, name="from_scan")
        new_wkv = jnp.transpose(s_new.reshape(n, n, bsz, nh), (2, 3, 0, 1))
    else:
        y_tok = jnp.transpose(y_s, (3, 1, 0, 2)).reshape(bsz, t, dr)
        new_wkv = jnp.transpose(s_new, (3, 0, 1, 2))
    y_rwkv = _rwkv_gate(y_tok, zl, prevl, p["mu_lora"], p["g2h"], p["g2l"], zg_block=p["zg_block"], nb=nb, tt=tt,
                        fresh=fresh, name="rwkv_gate").reshape(m, dr)


    u3 = u.reshape(bsz, t, dc)
    if conv0 is None:
        new_conv = u3[:, t - hbuf:, :]
        cnb = 1
    else:
        new_conv = jnp.concatenate([conv0, u3], axis=1)[:, -hbuf:, :]
        cnb = 4
    y_conv = _conv_module(u3, conv0, p["conv_w"], p["conv_b"], p["conv_ln_w"], p["conv_ln_b"], nb=cnb, tt=tt,
                          name="conv_module").reshape(m, dc)

    assert dr == dc
    h1 = _matmul([y_rwkv, y_conv], [p["w_out"], p["w_out"]], w_blocks=[(0, 0), (1, 0)], res=x2, epi="res",
                 **_tiles("out_proj"))
    dff = p["w_down"].shape[0]
    tf = _pick_tile(dff, TILES["ffn_gate_up"][1])
    h1n = _rmsnorm(h1, p["g_ffn"], out_dtype=BF16, name="ffn_norm")
    act = _matmul([h1n], [p["w_gate_up"], p["w_gate_up"]], epi="swiglu", n_out=dff, out_dtype=BF16,
                  w_blocks=[(0, 0), (0, dff // tf)], **_tiles("ffn_gate_up"))
    h2 = _matmul([act], [p["w_down"]], res=h1, epi="res", **_tiles("ffn_down"))
    h2n = _rmsnorm(h2, p["g_ple"], out_dtype=BF16, name="ple_norm")
    h3 = _matmul([h2n, pe.reshape(m, -1)], [p["w_ple_gate"], p["w_ple_proj"]], res=h2, epi="ple", **_tiles("ple"))
    return h3.reshape(bsz, t, d), new_wkv, new_shift, new_conv


def _prep_params(i, g_mix, w_in, mu_shift, w0, w2, a0, a2, g2, k_k, k_a, r_k, lnx_w, lnx_b, conv_w, conv_b,
                 conv_ln_w, conv_ln_b, w_out, g_ffn, w_gate_up, w_down, g_ple, w_ple_gate, w_ple_proj):
    dr = w0.shape[-1]
    dc = conv_w.shape[-1]
    lw, la, lg = w2.shape[1], a2.shape[1], g2.shape[1]
    lwp, lap, lgp = (_round_up(x, LANES) for x in (lw, la, lg))

    def lora_groups(x):
        pad = lambda y, to: jnp.pad(y, [(0, to - y.shape[0])] + [(0, 0)] * (y.ndim - 1))
        return jnp.concatenate([pad(x[:lw], lwp), pad(x[lw:lw + la], lap), pad(x[lw + la:lw + la + lg], lgp)], axis=0)

    lora_n = lw + la + lg
    wt = jnp.swapaxes(w_in[i], 0, 1)
    w2h, w2l = _split_hi_lo(w2[i], lwp)
    a2h, a2l = _split_hi_lo(a2[i], lap)
    g2h, g2l = _split_hi_lo(g2[i], lgp)
    assert (lwp + lap) % lgp == 0
    return dict(
        dr=dr, dc=dc,
        g_mix=g_mix[i], w_in_t=wt,
        w_lora_t=lora_groups(lax.optimization_barrier(wt[3 * dr:3 * dr + lora_n])).astype(BF16),
        conv_col0=3 * dr + lora_n,
        mu_rkv=mu_shift[i][None, :3 * dr], mu_lora=lora_groups(mu_shift[i][3 * dr:])[None, :],
        zg_block=(lwp + lap) // lgp,
        w0=w0[i][None, :], w2h=w2h, w2l=w2l, a0=a0[i][None, :], a2h=a2h, a2l=a2l, g2h=g2h, g2l=g2l,
        k_k=k_k[i], k_a=k_a[i], r_k=r_k[i], lnx_w=lnx_w[i], lnx_b=lnx_b[i],
        conv_w=conv_w[i], conv_b=conv_b[i], conv_ln_w=conv_ln_w[i], conv_ln_b=conv_ln_b[i],
        w_out=w_out[i].astype(BF16), g_ffn=g_ffn[i], w_gate_up=w_gate_up[i], w_down=w_down[i].astype(BF16),
        g_ple=g_ple[i], w_ple_gate=w_ple_gate[i].astype(BF16), w_ple_proj=w_ple_proj[i].astype(BF16),
    )


def kernel(x_prompt, x_sample, state_wkv, state_shift, state_conv, p_prompt, p_sample, g_mix, w_in, mu_shift, w0, w2, a0, a2, g2, k_k, k_a, r_k, lnx_w, lnx_b, conv_w, conv_b, conv_ln_w, conv_ln_b, w_out, g_ffn, w_gate_up, w_down, g_ple, w_ple_gate, w_ple_proj, g_final):
    depth = w_in.shape[0]
    layer_params = (g_mix, w_in, mu_shift, w0, w2, a0, a2, g2, k_k, k_a, r_k, lnx_w, lnx_b, conv_w, conv_b,
                    conv_ln_w, conv_ln_b, w_out, g_ffn, w_gate_up, w_down, g_ple, w_ple_gate, w_ple_proj)
    hp, hs = x_prompt, x_sample
    outs = [[] for _ in range(6)]
    for i in range(depth):
        p = _prep_params(i, *layer_params)
        hp, s_w, s_sh, s_c = _layer(hp, p_prompt[i], None, None, None, True, p)
        outs[0].append(s_w); outs[1].append(s_sh); outs[2].append(s_c)
        hs, s_w, s_sh, s_c = _layer(hs, p_sample[i], state_wkv[i], state_shift[i], state_conv[i], False, p)
        outs[3].append(s_w); outs[4].append(s_sh); outs[5].append(s_c)
    d = hp.shape[-1]
    y_prompt = _rmsnorm(hp.reshape(-1, d), g_final, name="final_norm").reshape(hp.shape)
    y_sample = _rmsnorm(hs.reshape(-1, d), g_final, name="final_norm").reshape(hs.shape)
    return (y_prompt, y_sample) + tuple(jnp.stack(o) for o in outs)
```

```python
import functools

import jax
import jax.numpy as jnp
from jax import lax
from jax.experimental import pallas as pl
from jax.experimental.pallas import tpu as pltpu

HEAD_SIZE = 64
RMS_EPS = 1e-6
LN_EPS = 1e-5
GN_EPS = 64e-5
KK_EPS = 1e-12
LANES = 128
SUBLANES = 8
VMEM_LIMIT_BYTES = 56 * 2 ** 20

F32 = jnp.float32
BF16 = jnp.bfloat16


TILES = dict(in_proj_rkv=(2048, 512, 1), in_proj_lora=(1024, 512, 2), shift_proj=(128, 512, 2),
             shift_lora=(128, 512, 2), in_proj_glu=(2048, 256, 1), out_proj=(1024, 512, 2),
             ffn_gate_up=(2048, 256, 1), ffn_down=(512, 512, 2), ple=(1024, 512, 2))


def _tiles(name):
    tm, tn, x_buffers = TILES[name]
    return dict(tm=tm, tn=tn, x_buffers=x_buffers, name=name)


def _cparams(*sem):
    return pltpu.CompilerParams(dimension_semantics=sem, vmem_limit_bytes=VMEM_LIMIT_BYTES)


def _round_up(n, m):
    return (n + m - 1) // m * m


def _pick_tile(n, pref):
    if n <= pref:
        return n
    t = pref - pref % LANES
    while t > LANES and n % t:
        t -= LANES
    assert n % t == 0, (n, pref)
    return t


def _mm_body(*refs, n_x, n_w, epi, w_transposed):
    refs = list(refs)
    x_refs = [refs.pop(0) for _ in range(n_x)]
    w_refs = [refs.pop(0) for _ in range(n_w)]
    res_ref = refs.pop(0) if epi in ("res", "ple") else None
    out_ref = refs.pop(0)

    def dot(x_ref, w_ref):
        contract = (((1,), (1 if w_transposed else 0,)), ((), ()))
        return lax.dot_general(x_ref[...].astype(BF16), w_ref[...].astype(BF16), contract,
                               preferred_element_type=F32)

    if epi in ("store", "res"):
        acc = dot(x_refs[0], w_refs[0])
        for xr, wr in zip(x_refs[1:], w_refs[1:]):
            acc = acc + dot(xr, wr)
        if epi == "res":
            acc = res_ref[...] + acc
        out_ref[...] = acc.astype(out_ref.dtype)
    elif epi == "glu":
        out_ref[...] = (dot(x_refs[0], w_refs[0]) * jax.nn.sigmoid(dot(x_refs[0], w_refs[1]))).astype(out_ref.dtype)
    elif epi == "swiglu":
        gate = dot(x_refs[0], w_refs[0])
        out_ref[...] = (gate * jax.nn.sigmoid(gate) * dot(x_refs[0], w_refs[1])).astype(out_ref.dtype)
    elif epi == "ple":
        pg = jax.nn.sigmoid(dot(x_refs[0], w_refs[0]))
        out_ref[...] = (res_ref[...] + pg * dot(x_refs[1], w_refs[1])).astype(out_ref.dtype)
    else:
        raise ValueError(epi)


def _matmul(xs, ws, *, res=None, epi="store", out_dtype=F32, n_out=None, w_blocks=None, w_transposed=False,
            w_row0=None, tm, tn, x_buffers, name):
    m = xs[0].shape[0]
    n_out = ws[0].shape[0 if w_transposed else 1] if n_out is None else n_out
    tm = _pick_tile(m, tm)
    tn = _pick_tile(n_out, tn)
    w_blocks = [(0, 0)] * len(ws) if w_blocks is None else w_blocks
    in_specs = [pl.BlockSpec((tm, x.shape[1]), lambda i, j: (i, 0), pipeline_mode=pl.Buffered(x_buffers))
                for x in xs]
    for idx, (rb, cb) in enumerate(w_blocks):
        k = xs[min(idx, len(xs) - 1)].shape[1]
        if w_transposed and w_row0 is not None:
            assert rb == 0 and cb == 0 and w_row0[idx] % SUBLANES == 0
            in_specs.append(pl.BlockSpec((pl.Element(tn), pl.Element(k)),
                                         functools.partial(lambda i, j, r0: (pl.multiple_of(r0 + j * tn, SUBLANES), 0),
                                                           r0=w_row0[idx])))
        elif w_transposed:
            in_specs.append(pl.BlockSpec((tn, k), functools.partial(lambda i, j, rb, cb: (j + cb, rb), rb=rb, cb=cb)))
        else:
            in_specs.append(pl.BlockSpec((k, tn), functools.partial(lambda i, j, rb, cb: (rb, j + cb), rb=rb, cb=cb)))
    if res is not None:
        in_specs.append(pl.BlockSpec((tm, tn), lambda i, j: (i, j)))
    return pl.pallas_call(
        functools.partial(_mm_body, n_x=len(xs), n_w=len(ws), epi=epi, w_transposed=w_transposed),
        grid=(m // tm, n_out // tn),
        in_specs=in_specs,
        out_specs=pl.BlockSpec((tm, tn), lambda i, j: (i, j)),
        out_shape=jax.ShapeDtypeStruct((m, n_out), out_dtype),
        compiler_params=_cparams("parallel", "arbitrary"),
        name=name,
    )(*xs, *ws, *([] if res is None else [res]))


def _rms_body(x_ref, g_ref, o_ref):
    x = x_ref[...].astype(F32)
    ms = jnp.mean(x * x, axis=-1, keepdims=True)
    o_ref[...] = (x * lax.rsqrt(ms + RMS_EPS) * g_ref[...]).astype(o_ref.dtype)


def _rmsnorm(x, gain, *, out_dtype=F32, tm=512, name="rmsnorm"):
    m, d = x.shape
    tm = _pick_tile(m, tm)
    return pl.pallas_call(
        _rms_body,
        grid=(m // tm,),
        in_specs=[pl.BlockSpec((tm, d), lambda i: (i, 0)), pl.BlockSpec((1, d), lambda i: (0, 0))],
        out_specs=pl.BlockSpec((tm, d), lambda i: (i, 0)),
        out_shape=jax.ShapeDtypeStruct((m, d), out_dtype),
        compiler_params=_cparams("parallel"),
        name=name,
    )(x, gain.reshape(1, d).astype(F32))


def _split3(x):
    hi = x.astype(BF16)
    r1 = x - hi.astype(F32)
    mid = r1.astype(BF16)
    lo = (r1 - mid.astype(F32)).astype(BF16)
    return hi, mid, lo


def _dot_f32(a, b_hi, b_lo):
    a_hi, a_mid, _ = _split3(a)
    dot = lambda p, q: jnp.dot(p, q, preferred_element_type=F32)
    return dot(a_hi, b_hi) + (dot(a_mid, b_hi) + dot(a_hi, b_lo))


def _token_shift(z_ref, prev_ref, mu_ref, lo, width, fresh, time_axis=1):
    nb, tt, _ = z_ref.shape
    z = z_ref[:, :, lo:lo + width]
    first = prev_ref[:, prev_ref.shape[1] - 1:, lo:lo + width]
    if fresh:
        first = jnp.where(pl.program_id(time_axis) == 0, 0.0, first)
    tpos = lax.broadcasted_iota(jnp.int32, (nb, tt, 1), 1)
    zp = jnp.where(tpos == 0, first, pltpu.roll(z, 1, axis=1))
    return z + (zp - z) * mu_ref[:, lo:lo + width]


def _shift_specs(z, prev, mu, *, nb, tt, width, cb, fresh):
    if fresh:
        assert prev is z and tt % SUBLANES == 0
        prev_spec = pl.BlockSpec((nb, SUBLANES, width),
                                 lambda i, j: (i, jnp.maximum(j * (tt // SUBLANES) - 1, 0), cb))
    else:
        prev_spec = pl.BlockSpec((nb, 1, width), lambda i, j: (i, 0, cb))
    return [pl.BlockSpec((nb, tt, width), lambda i, j: (i, j, cb)), prev_spec,
            pl.BlockSpec((1, width), lambda i, j: (0, cb))]


def _prep_body(zr_ref, prevr_ref, mur_ref, zl_ref, prevl_ref, mul_ref, w0_ref, w2h_ref, w2l_ref, a0_ref, a2h_ref,
               a2l_ref, r_ref, k_ref, v_ref, d_ref, a_ref, *, dr, lw, la, fresh):
    nb, tt, _ = zl_ref.shape
    for c, o_ref in enumerate((r_ref, k_ref, v_ref)):
        o_ref[...] = _token_shift(zr_ref, prevr_ref, mur_ref, c * dr, dr, fresh)
    zw = _token_shift(zl_ref, prevl_ref, mul_ref, 0, lw, fresh).reshape(nb * tt, lw)
    za = _token_shift(zl_ref, prevl_ref, mul_ref, lw, la, fresh).reshape(nb * tt, la)
    dec, a = _decay_and_rate(zw, za, w0_ref, w2h_ref, w2l_ref, a0_ref, a2h_ref, a2l_ref)
    d_ref[...] = dec.reshape(nb, tt, dr)
    a_ref[...] = a.reshape(nb, tt, dr)


def _decay_and_rate(zw, za, w0_ref, w2h_ref, w2l_ref, a0_ref, a2h_ref, a2l_ref):
    xw = w0_ref[...] + _dot_f32(jnp.tanh(zw), w2h_ref[...], w2l_ref[...])
    w_log = jnp.minimum(xw, 0.0) - jnp.log1p(jnp.exp(-jnp.abs(xw))) - 0.5
    xa = a0_ref[...] + _dot_f32(za, a2h_ref[...], a2l_ref[...])
    return jnp.exp(-jnp.exp(w_log)), jax.nn.sigmoid(xa)


def _rwkv_prep(zr, prevr, mur, zl, prevl, mul, w0, w2h, w2l, a0, a2h, a2l, *, nb, tt, fresh, name):
    bsz, t, _ = zl.shape
    dr = w0.shape[-1]
    lw, la = w2h.shape[0], a2h.shape[0]
    row = lambda c: pl.BlockSpec((1, c), lambda i, j: (0, 0))
    full = lambda a: pl.BlockSpec(a.shape, lambda i, j: (0, 0))
    shift = functools.partial(_shift_specs, nb=nb, tt=tt, cb=0, fresh=fresh)
    return pl.pallas_call(
        functools.partial(_prep_body, dr=dr, lw=lw, la=la, fresh=fresh),
        grid=(bsz // nb, t // tt),
        in_specs=shift(zr, prevr, mur, width=zr.shape[-1]) + shift(zl, prevl, mul, width=zl.shape[-1])
        + [row(dr), full(w2h), full(w2l), row(dr), full(a2h), full(a2l)],
        out_specs=[pl.BlockSpec((nb, tt, dr), lambda i, j: (i, j, 0))] * 5,
        out_shape=[jax.ShapeDtypeStruct((bsz, t, dr), F32)] * 5,
        compiler_params=_cparams("parallel", "parallel"),
        name=name,
    )(zr, prevr, mur, zl, prevl, mul, w0, w2h, w2l, a0, a2h, a2l)


RELAYOUT_ROWS = 128
SCAN_PITCH = HEAD_SIZE + SUBLANES


def _to_scan_body(*refs, nh, bsz, mix):
    x_refs, refs = refs[:bsz], refs[bsz:]
    if mix:
        prev_refs, mu_ref, refs = refs[:bsz], refs[bsz], refs[bsz + 1:]
    o_ref, xt_ref = refs
    n, tq = HEAD_SIZE, RELAYOUT_ROWS
    for b in range(bsz):
        x = x_refs[b][0]
        if mix:
            first = jnp.where(pl.program_id(0) == 0, 0.0, prev_refs[b][0, SUBLANES - 1:, :])
            row = lax.broadcasted_iota(jnp.int32, (tq, 1), 0)
            xp = jnp.where(row == 0, first, pltpu.roll(x, 1, axis=0))
            x = x + (xp - x) * mu_ref[...]
        _stash_transposed(x, xt_ref, b, nh)
    _emit_scan_rows(xt_ref, o_ref, nh, bsz)


def _stash_transposed(x, xt_ref, b, nh):
    n = HEAD_SIZE
    xt = x.T
    for h in range(nh):
        xt_ref[b, h * SCAN_PITCH:h * SCAN_PITCH + n, :] = xt[h * n:(h + 1) * n, :]


def _emit_scan_rows(xt_ref, o_ref, nh, bsz):
    n, tq = HEAD_SIZE, RELAYOUT_ROWS
    for j in range(n):
        m = jnp.concatenate([xt_ref[b, pl.ds(j, nh, stride=SCAN_PITCH), :] for b in range(bsz)], axis=0)
        o_ref[pl.ds(j, tq, stride=SCAN_PITCH), :] = m.T
    for j in range(n, SCAN_PITCH):
        o_ref[pl.ds(j, tq, stride=SCAN_PITCH), :] = jnp.zeros((tq, LANES), F32)


def _prep_to_scan_body(zl_ref, prev_ref, mu_ref, w0_ref, w2h_ref, w2l_ref, a0_ref, a2h_ref, a2l_ref,
                       d_ref, a_ref, xtd_ref, xta_ref, *, nh, lw, la):
    bsz = zl_ref.shape[0]
    zw = _token_shift(zl_ref, prev_ref, mu_ref, 0, lw, True, time_axis=0)
    za = _token_shift(zl_ref, prev_ref, mu_ref, lw, la, True, time_axis=0)
    for b in range(bsz):
        dec, a = _decay_and_rate(zw[b], za[b], w0_ref, w2h_ref, w2l_ref, a0_ref, a2h_ref, a2l_ref)
        _stash_transposed(dec, xtd_ref, b, nh)
        _stash_transposed(a, xta_ref, b, nh)
    _emit_scan_rows(xtd_ref, d_ref, nh, bsz)
    _emit_scan_rows(xta_ref, a_ref, nh, bsz)


def _prep_to_scan(zl, mul, w0, w2h, w2l, a0, a2h, a2l, *, nh, name):
    bsz, t, lo = zl.shape
    lw, la = w2h.shape[0], a2h.shape[0]
    assert bsz * nh == LANES and t % RELAYOUT_ROWS == 0
    per = RELAYOUT_ROWS // SUBLANES
    full = lambda a: pl.BlockSpec(a.shape, lambda i: (0,) * a.ndim)
    out_spec = pl.BlockSpec((RELAYOUT_ROWS * SCAN_PITCH, LANES), lambda i: (i, 0))
    scratch = pltpu.VMEM((bsz, nh * SCAN_PITCH, RELAYOUT_ROWS), F32)
    outs = pl.pallas_call(
        functools.partial(_prep_to_scan_body, nh=nh, lw=lw, la=la),
        grid=(t // RELAYOUT_ROWS,),
        in_specs=[pl.BlockSpec((bsz, RELAYOUT_ROWS, lo), lambda i: (0, i, 0)),
                  pl.BlockSpec((bsz, SUBLANES, lo), lambda i: (0, jnp.maximum(i * per - 1, 0), 0)),
                  full(mul), full(w0), full(w2h), full(w2l), full(a0), full(a2h), full(a2l)],
        out_specs=[out_spec] * 2,
        out_shape=[jax.ShapeDtypeStruct((t * SCAN_PITCH, LANES), F32)] * 2,
        scratch_shapes=[scratch, scratch],
        compiler_params=_cparams("parallel"),
        name=name,
    )(zl, zl, mul, w0, w2h, w2l, a0, a2h, a2l)
    return [o.reshape(1, t, SCAN_PITCH, LANES) for o in outs]


def _to_scan_bh(x3, *, nh, name, col_block=0, mu=None):
    bsz, t, _ = x3.shape
    d = nh * HEAD_SIZE
    assert bsz * nh == LANES and t % RELAYOUT_ROWS == 0
    in_specs = [pl.BlockSpec((1, RELAYOUT_ROWS, d), functools.partial(lambda i, b: (b, i, col_block), b=b))
                for b in range(bsz)]
    args = [x3] * bsz
    if mu is not None:
        per = RELAYOUT_ROWS // SUBLANES
        in_specs += [pl.BlockSpec((1, SUBLANES, d),
                                  functools.partial(lambda i, b: (b, jnp.maximum(i * per - 1, 0), col_block), b=b))
                     for b in range(bsz)] + [pl.BlockSpec((1, d), lambda i: (0, col_block))]
        args += [x3] * bsz + [mu]
    out = pl.pallas_call(
        functools.partial(_to_scan_body, nh=nh, bsz=bsz, mix=mu is not None),
        grid=(t // RELAYOUT_ROWS,),
        in_specs=in_specs,
        out_specs=pl.BlockSpec((RELAYOUT_ROWS * SCAN_PITCH, LANES), lambda i: (i, 0)),
        out_shape=jax.ShapeDtypeStruct((t * SCAN_PITCH, LANES), F32),
        scratch_shapes=[pltpu.VMEM((bsz, nh * SCAN_PITCH, RELAYOUT_ROWS), F32)],
        compiler_params=_cparams("parallel"),
        name=name,
    )(*args)
    return out.reshape(1, t, SCAN_PITCH, LANES)


def _from_scan_body(*refs, nh, bsz):
    y_ref, o_ref, yt_ref = refs
    n, tq = HEAD_SIZE, RELAYOUT_ROWS
    for j in range(n):
        mt = y_ref[pl.ds(j, tq, stride=n), :].T
        for b in range(bsz):
            yt_ref[b, pl.ds(j, nh, stride=SCAN_PITCH), :] = mt[b * nh:(b + 1) * nh, :]
    for b in range(bsz):
        yt = jnp.concatenate([yt_ref[b, h * SCAN_PITCH:h * SCAN_PITCH + n, :] for h in range(nh)], axis=0)
        o_ref[b] = yt.T


def _from_scan_bh(y, *, bsz, nh, name):
    t, n, _ = y.shape
    d = nh * n
    assert bsz * nh == LANES and n == HEAD_SIZE and t % RELAYOUT_ROWS == 0
    return pl.pallas_call(
        functools.partial(_from_scan_body, nh=nh, bsz=bsz),
        grid=(t // RELAYOUT_ROWS,),
        in_specs=[pl.BlockSpec((RELAYOUT_ROWS * n, LANES), lambda i: (i, 0))],
        out_specs=pl.BlockSpec((bsz, RELAYOUT_ROWS, d), lambda i: (0, i, 0)),
        out_shape=jax.ShapeDtypeStruct((bsz, t, d), F32),
        scratch_shapes=[pltpu.VMEM((bsz, nh * SCAN_PITCH, RELAYOUT_ROWS), F32)],
        compiler_params=_cparams("parallel"),
        name=name,
    )(y.reshape(t * n, LANES))


def _scan_body(r_ref, k_ref, v_ref, d_ref, a_ref, s0_ref, kk_ref, ka_ref, rk_ref, lw_ref, lb_ref,
               y_ref, s_ref, yrow_ref):
    n = HEAD_SIZE
    tc = r_ref.shape[0]

    @pl.when(pl.program_id(1) == 0)
    def _():
        s_ref[...] = s0_ref[...]

    def colsum(x):
        return jnp.sum(x, axis=0, keepdims=True)

    def step(t, gamma):
        r, k, v, w, a = (x[t, :n, :] for x in (r_ref, k_ref, v_ref, d_ref, a_ref))
        kk = k * kk_ref[...]
        kk = kk / jnp.maximum(jnp.sqrt(colsum(kk * kk)), KK_EPS)
        k2 = k * (1.0 + (a - 1.0) * ka_ref[...])
        kk_n = kk * gamma
        gamma = gamma * w
        inv = 1.0 / gamma
        b_n = kk * a * inv
        k_n = k2 * inv
        r_n = r * gamma
        for i in range(n):
            s_i = s_ref[i]
            s_kk = colsum(s_i * kk_n)
            s_i = s_i - s_kk * b_n + v_ref[t, i:i + 1, :] * k_n
            s_ref[i] = s_i
            yrow_ref[i:i + 1, :] = colsum(s_i * r_n)
        y = yrow_ref[...]
        ym = jnp.mean(y, axis=0, keepdims=True)
        yc = y - ym
        yv = jnp.mean(yc * yc, axis=0, keepdims=True)
        yn = yc * lax.rsqrt(yv + GN_EPS) * lw_ref[...] + lb_ref[...]
        bonus = colsum(r * k2 * rk_ref[...]) * v
        y_ref[t] = yn + bonus
        return gamma

    gamma = lax.fori_loop(0, tc, step, jnp.ones((n, LANES), F32))
    s_ref[...] = s_ref[...] * gamma


def _wkv_scan(r, k, v, d, a, s0, kkp, kap, rkp, lnw, lnb, *, tc, name):
    g, t, pitch, _ = r.shape
    n = HEAD_SIZE
    tc = min(tc, t)
    seq_in = pl.BlockSpec((None, tc, pitch, LANES), lambda gi, ci: (gi, ci, 0, 0))
    seq = pl.BlockSpec((None, tc, n, LANES), lambda gi, ci: (gi, ci, 0, 0))
    par = pl.BlockSpec((None, n, LANES), lambda gi, ci: (gi, 0, 0))
    st = pl.BlockSpec((None, n, n, LANES), lambda gi, ci: (gi, 0, 0, 0))
    return pl.pallas_call(
        _scan_body,
        grid=(g, t // tc),
        in_specs=[seq_in] * 5 + [st] + [par] * 5,
        out_specs=[seq, st],
        out_shape=[jax.ShapeDtypeStruct((g, t, n, LANES), F32),
                   jax.ShapeDtypeStruct((g, n, n, LANES), F32)],
        scratch_shapes=[pltpu.VMEM((n, LANES), F32)],
        compiler_params=_cparams("parallel", "arbitrary"),
        name=name,
    )(r, k, v, d, a, s0, kkp, kap, rkp, lnw, lnb)


def _gate_body(y_ref, zg_ref, prev_ref, mu_ref, g2h_ref, g2l_ref, o_ref, *, fresh):
    nb, tt, lg = zg_ref.shape
    zg = _token_shift(zg_ref, prev_ref, mu_ref, 0, lg, fresh).reshape(nb * tt, lg)
    g = _dot_f32(jax.nn.sigmoid(zg), g2h_ref[...], g2l_ref[...])
    o_ref[...] = (y_ref[...].reshape(nb * tt, -1) * g).reshape(o_ref.shape).astype(o_ref.dtype)


def _rwkv_gate(y3, zl, prevl, mul, g2h, g2l, *, zg_block, nb, tt, fresh, name):
    bsz, t, dr = y3.shape
    lg = g2h.shape[0]
    return pl.pallas_call(
        functools.partial(_gate_body, fresh=fresh),
        grid=(bsz // nb, t // tt),
        in_specs=[pl.BlockSpec((nb, tt, dr), lambda i, j: (i, j, 0))]
        + _shift_specs(zl, prevl, mul, nb=nb, tt=tt, width=lg, cb=zg_block, fresh=fresh)
        + [pl.BlockSpec(g2h.shape, lambda i, j: (0, 0)), pl.BlockSpec(g2l.shape, lambda i, j: (0, 0))],
        out_specs=pl.BlockSpec((nb, tt, dr), lambda i, j: (i, j, 0)),
        out_shape=jax.ShapeDtypeStruct((bsz, t, dr), BF16),
        compiler_params=_cparams("parallel", "parallel"),
        name=name,
    )(y3, zl, prevl, mul, g2h, g2l)


CONV_ROW_CHUNK = 32
CONV_LANE_CHUNK = 512


def _conv_body(u_ref, hist_ref, cw_ref, cb_ref, lw_ref, lb_ref, o_ref, full_ref, fs_ref, c_ref, *,
               hist_rows, taps, fresh):
    nb, tt, dc = u_ref.shape
    hin = hist_ref.shape[1]
    hist = hist_ref[...]
    if fresh:
        hist = jnp.where(pl.program_id(1) == 0, 0.0, hist)
    if hin < hist_rows:
        full_ref[:, :hist_rows - hin, :] = jnp.zeros((nb, hist_rows - hin, dc), F32)
    full_ref[:, hist_rows - hin:hist_rows, :] = hist
    full_ref[:, hist_rows:, :] = u_ref[...]
    lead = hist_rows - (taps - 1)
    span = hist_rows + tt - SUBLANES
    for s in range(1, SUBLANES):
        fs_ref[s - 1] = full_ref[:, s:s + span, :]
    rc = min(CONV_ROW_CHUNK, tt)
    lc = min(CONV_LANE_CHUNK, dc)

    def row_chunk(ri, carry, *, b, c0):
        r0 = pl.multiple_of(ri * rc, rc)
        acc = jnp.broadcast_to(cb_ref[:, c0:c0 + lc], (rc, lc))
        for j in range(taps):
            s = (lead + j) % SUBLANES
            rows = pl.ds(pl.multiple_of(r0 + (lead + j - s), SUBLANES), rc)
            win = full_ref[b, rows, c0:c0 + lc] if s == 0 else fs_ref[s - 1, b, rows, c0:c0 + lc]
            acc = acc + win * cw_ref[j:j + 1, c0:c0 + lc]
        c_ref[b, pl.ds(r0, rc), c0:c0 + lc] = acc
        return carry

    for b in range(nb):
        for c0 in range(0, dc, lc):
            lax.fori_loop(0, tt // rc, functools.partial(row_chunk, b=b, c0=c0), 0)
    acc = c_ref[...]
    mu = jnp.mean(acc, axis=-1, keepdims=True)
    xc = acc - mu
    var = jnp.mean(xc * xc, axis=-1, keepdims=True)
    y = xc * lax.rsqrt(var + LN_EPS) * lw_ref[...] + lb_ref[...]
    o_ref[...] = (y * jax.nn.sigmoid(y)).astype(o_ref.dtype)


def _conv_module(u3, conv0, cw, cb, lnw, lnb, *, nb, tt, name):
    bsz, t, dc = u3.shape
    taps = cw.shape[0]
    hist_rows = _round_up(taps - 1, SUBLANES)
    fresh = conv0 is None
    if fresh:
        assert nb == 1 and tt % hist_rows == 0
        hist, hin = u3, hist_rows
        hist_map = lambda i, j: (i, jnp.maximum(j * (tt // hist_rows) - 1, 0), 0)
    else:
        assert tt == t
        hist, hin = conv0, conv0.shape[1]
        hist_map = lambda i, j: (i, 0, 0)
    row = pl.BlockSpec((1, dc), lambda i, j: (0, 0))
    return pl.pallas_call(
        functools.partial(_conv_body, hist_rows=hist_rows, taps=taps, fresh=fresh),
        grid=(bsz // nb, t // tt),
        in_specs=[pl.BlockSpec((nb, tt, dc), lambda i, j: (i, j, 0)),
                  pl.BlockSpec((nb, hin, dc), hist_map),
                  pl.BlockSpec((taps, dc), lambda i, j: (0, 0)), row, row, row],
        out_specs=pl.BlockSpec((nb, tt, dc), lambda i, j: (i, j, 0)),
        out_shape=jax.ShapeDtypeStruct((bsz, t, dc), BF16),
        scratch_shapes=[pltpu.VMEM((nb, hist_rows + tt, dc), F32),
                        pltpu.VMEM((SUBLANES - 1, nb, hist_rows + tt - SUBLANES, dc), F32),
                        pltpu.VMEM((nb, tt, dc), F32)],
        compiler_params=_cparams("parallel", "parallel"),
        name=name,
    )(u3, hist, cw, cb.reshape(1, dc), lnw.reshape(1, dc), lnb.reshape(1, dc))


def _split_hi_lo(w, rows):
    w = jnp.pad(w.astype(F32), ((0, rows - w.shape[0]), (0, 0)))
    hi = w.astype(BF16)
    return hi, (w - hi.astype(F32)).astype(BF16)


def _layer(h, pe, wkv0, shift0, conv0, lanes_are_batch_head, p):
    bsz, t, d = h.shape
    m = bsz * t
    dr, dc, n = p["dr"], p["dc"], HEAD_SIZE
    nh = dr // n
    hbuf = conv0.shape[1] if conv0 is not None else p["conv_w"].shape[0] - 1
    x2 = h.reshape(m, d)
    fresh = shift0 is None

    xn = _rmsnorm(x2, p["g_mix"], out_dtype=BF16, name="mix_norm")
    in_proj = functools.partial(_matmul, w_transposed=True)
    zr = in_proj([xn], [p["w_in_t"]], n_out=3 * dr, **_tiles("in_proj_rkv")).reshape(bsz, t, 3 * dr)
    zl = in_proj([xn], [p["w_lora_t"]], **_tiles("in_proj_lora")).reshape(bsz, t, -1)
    c0 = p["conv_col0"]
    u = in_proj([xn], [p["w_in_t"], p["w_in_t"]], epi="glu", n_out=dc, w_row0=[c0, c0 + dc], **_tiles("in_proj_glu"))
    new_shift = _rmsnorm(h[:, -1, :], p["g_mix"], name="shift_norm")

    if fresh:
        tt, nb = 256, 1
        prevr, prevl = zr, zl
    else:
        tt, nb = t, 32
        s0b = shift0.astype(BF16)
        prevr = in_proj([s0b], [p["w_in_t"]], n_out=3 * dr, **_tiles("shift_proj")).reshape(bsz, 1, 3 * dr)
        prevl = in_proj([s0b], [p["w_lora_t"]], **_tiles("shift_lora")).reshape(bsz, 1, -1)
    prep = functools.partial(_rwkv_prep, zr, prevr, p["mu_rkv"], zl, prevl, p["mu_lora"], p["w0"], p["w2h"], p["w2l"],
                             p["a0"], p["a2h"], p["a2l"], nb=nb, tt=tt, fresh=fresh, name="rwkv_prep")

    if lanes_are_batch_head:
        assert fresh
        scan_in = [_to_scan_bh(zr, nh=nh, col_block=c, mu=p["mu_rkv"], name="mix_to_scan") for c in range(3)]
        scan_in += _prep_to_scan(zl, p["mu_lora"], p["w0"], p["w2h"], p["w2l"], p["a0"], p["a2h"], p["a2l"], nh=nh,
                                 name="prep_to_scan")
        par = lambda q: jnp.tile(q.reshape(nh, n).T, (1, bsz))[None]
        s0 = jnp.zeros((1, n, n, LANES), F32)
    else:
        to_scan = lambda x: jnp.transpose(x.reshape(bsz, t, nh, n), (2, 1, 3, 0))
        scan_in = [to_scan(x) for x in prep()]
        par = lambda q: jnp.broadcast_to(q.reshape(nh, n)[:, :, None], (nh, n, LANES))
        s0 = jnp.transpose(wkv0, (1, 2, 3, 0))
    y_s, s_new = _wkv_scan(*scan_in, s0, par(p["k_k"]), par(p["k_a"]), par(p["r_k"]), par(p["lnx_w"]),
                           par(p["lnx_b"]), tc=32, name="wkv_scan")
    if lanes_are_batch_head:
        y_tok = _from_scan_bh(y_s[0], bsz=bsz, nh=nh, name="from_scan")
        new_wkv = jnp.transpose(s_new.reshape(n, n, bsz, nh), (2, 3, 0, 1))
    else:
        y_tok = jnp.transpose(y_s, (3, 1, 0, 2)).reshape(bsz, t, dr)
        new_wkv = jnp.transpose(s_new, (3, 0, 1, 2))
    y_rwkv = _rwkv_gate(y_tok, zl, prevl, p["mu_lora"], p["g2h"], p["g2l"], zg_block=p["zg_block"], nb=nb, tt=tt,
                        fresh=fresh, name="rwkv_gate").reshape(m, dr)


    u3 = u.reshape(bsz, t, dc)
    if conv0 is None:
        new_conv = u3[:, t - hbuf:, :]
        cnb = 1
    else:
        new_conv = jnp.concatenate([conv0, u3], axis=1)[:, -hbuf:, :]
        cnb = 4
    y_conv = _conv_module(u3, conv0, p["conv_w"], p["conv_b"], p["conv_ln_w"], p["conv_ln_b"], nb=cnb, tt=tt,
                          name="conv_module").reshape(m, dc)

    assert dr == dc
    h1 = _matmul([y_rwkv, y_conv], [p["w_out"], p["w_out"]], w_blocks=[(0, 0), (1, 0)], res=x2, epi="res",
                 **_tiles("out_proj"))
    dff = p["w_down"].shape[0]
    tf = _pick_tile(dff, TILES["ffn_gate_up"][1])
    h1n = _rmsnorm(h1, p["g_ffn"], out_dtype=BF16, name="ffn_norm")
    act = _matmul([h1n], [p["w_gate_up"], p["w_gate_up"]], epi="swiglu", n_out=dff, out_dtype=BF16,
                  w_blocks=[(0, 0), (0, dff // tf)], **_tiles("ffn_gate_up"))
    h2 = _matmul([act], [p["w_down"]], res=h1, epi="res", **_tiles("ffn_down"))
    h2n = _rmsnorm(h2, p["g_ple"], out_dtype=BF16, name="ple_norm")
    h3 = _matmul([h2n, pe.reshape(m, -1)], [p["w_ple_gate"], p["w_ple_proj"]], res=h2, epi="ple", **_tiles("ple"))
    return h3.reshape(bsz, t, d), new_wkv, new_shift, new_conv


def _prep_params(i, g_mix, w_in, mu_shift, w0, w2, a0, a2, g2, k_k, k_a, r_k, lnx_w, lnx_b, conv_w, conv_b,
                 conv_ln_w, conv_ln_b, w_out, g_ffn, w_gate_up, w_down, g_ple, w_ple_gate, w_ple_proj):
    dr = w0.shape[-1]
    dc = conv_w.shape[-1]
    lw, la, lg = w2.shape[1], a2.shape[1], g2.shape[1]
    lwp, lap, lgp = (_round_up(x, LANES) for x in (lw, la, lg))

    def lora_groups(x):
        pad = lambda y, to: jnp.pad(y, [(0, to - y.shape[0])] + [(0, 0)] * (y.ndim - 1))
        return jnp.concatenate([pad(x[:lw], lwp), pad(x[lw:lw + la], lap), pad(x[lw + la:lw + la + lg], lgp)], axis=0)

    lora_n = lw + la + lg
    wt = jnp.swapaxes(w_in[i], 0, 1)
    w2h, w2l = _split_hi_lo(w2[i], lwp)
    a2h, a2l = _split_hi_lo(a2[i], lap)
    g2h, g2l = _split_hi_lo(g2[i], lgp)
    assert (lwp + lap) % lgp == 0
    return dict(
        dr=dr, dc=dc,
        g_mix=g_mix[i], w_in_t=wt,
        w_lora_t=lora_groups(lax.optimization_barrier(wt[3 * dr:3 * dr + lora_n])).astype(BF16),
        conv_col0=3 * dr + lora_n,
        mu_rkv=mu_shift[i][None, :3 * dr], mu_lora=lora_groups(mu_shift[i][3 * dr:])[None, :],
        zg_block=(lwp + lap) // lgp,
        w0=w0[i][None, :], w2h=w2h, w2l=w2l, a0=a0[i][None, :], a2h=a2h, a2l=a2l, g2h=g2h, g2l=g2l,
        k_k=k_k[i], k_a=k_a[i], r_k=r_k[i], lnx_w=lnx_w[i], lnx_b=lnx_b[i],
        conv_w=conv_w[i], conv_b=conv_b[i], conv_ln_w=conv_ln_w[i], conv_ln_b=conv_ln_b[i],
        w_out=w_out[i].astype(BF16), g_ffn=g_ffn[i], w_gate_up=w_gate_up[i], w_down=w_down[i].astype(BF16),
        g_ple=g_ple[i], w_ple_gate=w_ple_gate[i].astype(BF16), w_ple_proj=w_ple_proj[i].astype(BF16),
    )


def kernel(x_prompt, x_sample, state_wkv, state_shift, state_conv, p_prompt, p_sample, g_mix, w_in, mu_shift, w0, w2, a0, a2, g2, k_k, k_a, r_k, lnx_w, lnx_b, conv_w, conv_b, conv_ln_w, conv_ln_b, w_out, g_ffn, w_gate_up, w_down, g_ple, w_ple_gate, w_ple_proj, g_final):
    depth = w_in.shape[0]
    layer_params = (g_mix, w_in, mu_shift, w0, w2, a0, a2, g2, k_k, k_a, r_k, lnx_w, lnx_b, conv_w, conv_b,
                    conv_ln_w, conv_ln_b, w_out, g_ffn, w_gate_up, w_down, g_ple, w_ple_gate, w_ple_proj)
    hp, hs = x_prompt, x_sample
    outs = [[] for _ in range(6)]
    for i in range(depth):
        p = _prep_params(i, *layer_params)
        hp, s_w, s_sh, s_c = _layer(hp, p_prompt[i], None, None, None, True, p)
        outs[0].append(s_w); outs[1].append(s_sh); outs[2].append(s_c)
        hs, s_w, s_sh, s_c = _layer(hs, p_sample[i], state_wkv[i], state_shift[i], state_conv[i], False, p)
        outs[3].append(s_w); outs[4].append(s_sh); outs[5].append(s_c)
    d = hp.shape[-1]
    y_prompt = _rmsnorm(hp.reshape(-1, d), g_final, name="final_norm").reshape(hp.shape)
    y_sample = _rmsnorm(hs.reshape(-1, d), g_final, name="final_norm").reshape(hs.shape)
    return (y_prompt, y_sample) + tuple(jnp.stack(o) for o in outs)
```

```python
import functools

import jax
import jax.numpy as jnp
from jax import lax
from jax.experimental import pallas as pl
from jax.experimental.pallas import tpu as pltpu

HEAD_SIZE = 64
RMS_EPS = 1e-6
LN_EPS = 1e-5
GN_EPS = 64e-5
KK_EPS = 1e-12
LANES = 128
SUBLANES = 8
VMEM_LIMIT_BYTES = 56 * 2 ** 20

F32 = jnp.float32
BF16 = jnp.bfloat16


TILES = dict(in_proj_rkv=(2048, 512, 1), in_proj_lora=(1024, 512, 2), shift_proj=(128, 512, 2),
             shift_lora=(128, 512, 2), in_proj_glu=(2048, 256, 1), out_proj=(1024, 512, 2),
             ffn_gate_up=(2048, 256, 1), ffn_down=(512, 512, 2), ple=(1024, 512, 2))


def _tiles(name):
    tm, tn, x_buffers = TILES[name]
    return dict(tm=tm, tn=tn, x_buffers=x_buffers, name=name)


def _cparams(*sem):
    return pltpu.CompilerParams(dimension_semantics=sem, vmem_limit_bytes=VMEM_LIMIT_BYTES)


def _round_up(n, m):
    return (n + m - 1) // m * m


def _pick_tile(n, pref):
    if n <= pref:
        return n
    t = pref - pref % LANES
    while t > LANES and n % t:
        t -= LANES
    assert n % t == 0, (n, pref)
    return t


def _mm_body(*refs, n_x, n_w, epi, w_transposed):
    refs = list(refs)
    x_refs = [refs.pop(0) for _ in range(n_x)]
    w_refs = [refs.pop(0) for _ in range(n_w)]
    res_ref = refs.pop(0) if epi in ("res", "ple") else None
    out_ref = refs.pop(0)

    def dot(x_ref, w_ref):
        contract = (((1,), (1 if w_transposed else 0,)), ((), ()))
        return lax.dot_general(x_ref[...].astype(BF16), w_ref[...].astype(BF16), contract,
                               preferred_element_type=F32)

    if epi in ("store", "res"):
        acc = dot(x_refs[0], w_refs[0])
        for xr, wr in zip(x_refs[1:], w_refs[1:]):
            acc = acc + dot(xr, wr)
        if epi == "res":
            acc = res_ref[...] + acc
        out_ref[...] = acc.astype(out_ref.dtype)
    elif epi == "glu":
        out_ref[...] = (dot(x_refs[0], w_refs[0]) * jax.nn.sigmoid(dot(x_refs[0], w_refs[1]))).astype(out_ref.dtype)
    elif epi == "swiglu":
        gate = dot(x_refs[0], w_refs[0])
        out_ref[...] = (gate * jax.nn.sigmoid(gate) * dot(x_refs[0], w_refs[1])).astype(out_ref.dtype)
    elif epi == "ple":
        pg = jax.nn.sigmoid(dot(x_refs[0], w_refs[0]))
        out_ref[...] = (res_ref[...] + pg * dot(x_refs[1], w_refs[1])).astype(out_ref.dtype)
    else:
        raise ValueError(epi)


def _matmul(xs, ws, *, res=None, epi="store", out_dtype=F32, n_out=None, w_blocks=None, w_transposed=False,
            w_row0=None, tm, tn, x_buffers, name):
    m = xs[0].shape[0]
    n_out = ws[0].shape[0 if w_transposed else 1] if n_out is None else n_out
    tm = _pick_tile(m, tm)
    tn = _pick_tile(n_out, tn)
    w_blocks = [(0, 0)] * len(ws) if w_blocks is None else w_blocks
    in_specs = [pl.BlockSpec((tm, x.shape[1]), lambda i, j: (i, 0), pipeline_mode=pl.Buffered(x_buffers))
                for x in xs]
    for idx, (rb, cb) in enumerate(w_blocks):
        k = xs[min(idx, len(xs) - 1)].shape[1]
        if w_transposed and w_row0 is not None:
            assert rb == 0 and cb == 0 and w_row0[idx] % SUBLANES == 0
            in_specs.append(pl.BlockSpec((pl.Element(tn), pl.Element(k)),
                                         functools.partial(lambda i, j, r0: (pl.multiple_of(r0 + j * tn, SUBLANES), 0),
                                                           r0=w_row0[idx])))
        elif w_transposed:
            in_specs.append(pl.BlockSpec((tn, k), functools.partial(lambda i, j, rb, cb: (j + cb, rb), rb=rb, cb=cb)))
        else:
            in_specs.append(pl.BlockSpec((k, tn), functools.partial(lambda i, j, rb, cb: (rb, j + cb), rb=rb, cb=cb)))
    if res is not None:
        in_specs.append(pl.BlockSpec((tm, tn), lambda i, j: (i, j)))
    return pl.pallas_call(
        functools.partial(_mm_body, n_x=len(xs), n_w=len(ws), epi=epi, w_transposed=w_transposed),
        grid=(m // tm, n_out // tn),
        in_specs=in_specs,
        out_specs=pl.BlockSpec((tm, tn), lambda i, j: (i, j)),
        out_shape=jax.ShapeDtypeStruct((m, n_out), out_dtype),
        compiler_params=_cparams("parallel", "arbitrary"),
        name=name,
    )(*xs, *ws, *([] if res is None else [res]))


def _rms_body(x_ref, g_ref, o_ref):
    x = x_ref[...].astype(F32)
    ms = jnp.mean(x * x, axis=-1, keepdims=True)
    o_ref[...] = (x * lax.rsqrt(ms + RMS_EPS) * g_ref[...]).astype(o_ref.dtype)


def _rmsnorm(x, gain, *, out_dtype=F32, tm=512, name="rmsnorm"):
    m, d = x.shape
    tm = _pick_tile(m, tm)
    return pl.pallas_call(
        _rms_body,
        grid=(m // tm,),
        in_specs=[pl.BlockSpec((tm, d), lambda i: (i, 0)), pl.BlockSpec((1, d), lambda i: (0, 0))],
        out_specs=pl.BlockSpec((tm, d), lambda i: (i, 0)),
        out_shape=jax.ShapeDtypeStruct((m, d), out_dtype),
        compiler_params=_cparams("parallel"),
        name=name,
    )(x, gain.reshape(1, d).astype(F32))


def _split3(x):
    hi = x.astype(BF16)
    r1 = x - hi.astype(F32)
    mid = r1.astype(BF16)
    lo = (r1 - mid.astype(F32)).astype(BF16)
    return hi, mid, lo


def _dot_f32(a, b_hi, b_lo):
    a_hi, a_mid, _ = _split3(a)
    dot = lambda p, q: jnp.dot(p, q, preferred_element_type=F32)
    return dot(a_hi, b_hi) + (dot(a_mid, b_hi) + dot(a_hi, b_lo))


def _token_shift(z_ref, prev_ref, mu_ref, lo, width, fresh, time_axis=1):
    nb, tt, _ = z_ref.shape
    z = z_ref[:, :, lo:lo + width]
    first = prev_ref[:, prev_ref.shape[1] - 1:, lo:lo + width]
    if fresh:
        first = jnp.where(pl.program_id(time_axis) == 0, 0.0, first)
    tpos = lax.broadcasted_iota(jnp.int32, (nb, tt, 1), 1)
    zp = jnp.where(tpos == 0, first, pltpu.roll(z, 1, axis=1))
    return z + (zp - z) * mu_ref[:, lo:lo + width]


def _shift_specs(z, prev, mu, *, nb, tt, width, cb, fresh):
    if fresh:
        assert prev is z and tt % SUBLANES == 0
        prev_spec = pl.BlockSpec((nb, SUBLANES, width),
                                 lambda i, j: (i, jnp.maximum(j * (tt // SUBLANES) - 1, 0), cb))
    else:
        prev_spec = pl.BlockSpec((nb, 1, width), lambda i, j: (i, 0, cb))
    return [pl.BlockSpec((nb, tt, width), lambda i, j: (i, j, cb)), prev_spec,
            pl.BlockSpec((1, width), lambda i, j: (0, cb))]


def _prep_body(zr_ref, prevr_ref, mur_ref, zl_ref, prevl_ref, mul_ref, w0_ref, w2h_ref, w2l_ref, a0_ref, a2h_ref,
               a2l_ref, r_ref, k_ref, v_ref, d_ref, a_ref, *, dr, lw, la, fresh):
    nb, tt, _ = zl_ref.shape
    for c, o_ref in enumerate((r_ref, k_ref, v_ref)):
        o_ref[...] = _token_shift(zr_ref, prevr_ref, mur_ref, c * dr, dr, fresh)
    zw = _token_shift(zl_ref, prevl_ref, mul_ref, 0, lw, fresh).reshape(nb * tt, lw)
    za = _token_shift(zl_ref, prevl_ref, mul_ref, lw, la, fresh).reshape(nb * tt, la)
    dec, a = _decay_and_rate(zw, za, w0_ref, w2h_ref, w2l_ref, a0_ref, a2h_ref, a2l_ref)
    d_ref[...] = dec.reshape(nb, tt, dr)
    a_ref[...] = a.reshape(nb, tt, dr)


def _decay_and_rate(zw, za, w0_ref, w2h_ref, w2l_ref, a0_ref, a2h_ref, a2l_ref):
    xw = w0_ref[...] + _dot_f32(jnp.tanh(zw), w2h_ref[...], w2l_ref[...])
    w_log = jnp.minimum(xw, 0.0) - jnp.log1p(jnp.exp(-jnp.abs(xw))) - 0.5
    xa = a0_ref[...] + _dot_f32(za, a2h_ref[...], a2l_ref[...])
    return jnp.exp(-jnp.exp(w_log)), jax.nn.sigmoid(xa)


def _rwkv_prep(zr, prevr, mur, zl, prevl, mul, w0, w2h, w2l, a0, a2h, a2l, *, nb, tt, fresh, name):
    bsz, t, _ = zl.shape
    dr = w0.shape[-1]
    lw, la = w2h.shape[0], a2h.shape[0]
    row = lambda c: pl.BlockSpec((1, c), lambda i, j: (0, 0))
    full = lambda a: pl.BlockSpec(a.shape, lambda i, j: (0, 0))
    shift = functools.partial(_shift_specs, nb=nb, tt=tt, cb=0, fresh=fresh)
    return pl.pallas_call(
        functools.partial(_prep_body, dr=dr, lw=lw, la=la, fresh=fresh),
        grid=(bsz // nb, t // tt),
        in_specs=shift(zr, prevr, mur, width=zr.shape[-1]) + shift(zl, prevl, mul, width=zl.shape[-1])
        + [row(dr), full(w2h), full(w2l), row(dr), full(a2h), full(a2l)],
        out_specs=[pl.BlockSpec((nb, tt, dr), lambda i, j: (i, j, 0))] * 5,
        out_shape=[jax.ShapeDtypeStruct((bsz, t, dr), F32)] * 5,
        compiler_params=_cparams("parallel", "parallel"),
        name=name,
    )(zr, prevr, mur, zl, prevl, mul, w0, w2h, w2l, a0, a2h, a2l)


RELAYOUT_ROWS = 128
SCAN_PITCH = HEAD_SIZE + SUBLANES


def _to_scan_body(*refs, nh, bsz, mix):
    x_refs, refs = refs[:bsz], refs[bsz:]
    if mix:
        prev_refs, mu_ref, refs = refs[:bsz], refs[bsz], refs[bsz + 1:]
    o_ref, xt_ref = refs
    n, tq = HEAD_SIZE, RELAYOUT_ROWS
    for b in range(bsz):
        x = x_refs[b][0]
        if mix:
            first = jnp.where(pl.program_id(0) == 0, 0.0, prev_refs[b][0, SUBLANES - 1:, :])
            row = lax.broadcasted_iota(jnp.int32, (tq, 1), 0)
            xp = jnp.where(row == 0, first, pltpu.roll(x, 1, axis=0))
            x = x + (xp - x) * mu_ref[...]
        _stash_transposed(x, xt_ref, b, nh)
    _emit_scan_rows(xt_ref, o_ref, nh, bsz)


def _stash_transposed(x, xt_ref, b, nh):
    n = HEAD_SIZE
    xt = x.T
    for h in range(nh):
        xt_ref[b, h * SCAN_PITCH:h * SCAN_PITCH + n, :] = xt[h * n:(h + 1) * n, :]


def _emit_scan_rows(xt_ref, o_ref, nh, bsz):
    n, tq = HEAD_SIZE, RELAYOUT_ROWS
    for j in range(n):
        m = jnp.concatenate([xt_ref[b, pl.ds(j, nh, stride=SCAN_PITCH), :] for b in range(bsz)], axis=0)
        o_ref[pl.ds(j, tq, stride=SCAN_PITCH), :] = m.T
    for j in range(n, SCAN_PITCH):
        o_ref[pl.ds(j, tq, stride=SCAN_PITCH), :] = jnp.zeros((tq, LANES), F32)


def _prep_to_scan_body(zl_ref, prev_ref, mu_ref, w0_ref, w2h_ref, w2l_ref, a0_ref, a2h_ref, a2l_ref,
                       d_ref, a_ref, xtd_ref, xta_ref, *, nh, lw, la):
    bsz = zl_ref.shape[0]
    zw = _token_shift(zl_ref, prev_ref, mu_ref, 0, lw, True, time_axis=0)
    za = _token_shift(zl_ref, prev_ref, mu_ref, lw, la, True, time_axis=0)
    for b in range(bsz):
        dec, a = _decay_and_rate(zw[b], za[b], w0_ref, w2h_ref, w2l_ref, a0_ref, a2h_ref, a2l_ref)
        _stash_transposed(dec, xtd_ref, b, nh)
        _stash_transposed(a, xta_ref, b, nh)
    _emit_scan_rows(xtd_ref, d_ref, nh, bsz)
    _emit_scan_rows(xta_ref, a_ref, nh, bsz)


def _prep_to_scan(zl, mul, w0, w2h, w2l, a0, a2h, a2l, *, nh, name):
    bsz, t, lo = zl.shape
    lw, la = w2h.shape[0], a2h.shape[0]
    assert bsz * nh == LANES and t % RELAYOUT_ROWS == 0
    per = RELAYOUT_ROWS // SUBLANES
    full = lambda a: pl.BlockSpec(a.shape, lambda i: (0,) * a.ndim)
    out_spec = pl.BlockSpec((RELAYOUT_ROWS * SCAN_PITCH, LANES), lambda i: (i, 0))
    scratch = pltpu.VMEM((bsz, nh * SCAN_PITCH, RELAYOUT_ROWS), F32)
    outs = pl.pallas_call(
        functools.partial(_prep_to_scan_body, nh=nh, lw=lw, la=la),
        grid=(t // RELAYOUT_ROWS,),
        in_specs=[pl.BlockSpec((bsz, RELAYOUT_ROWS, lo), lambda i: (0, i, 0)),
                  pl.BlockSpec((bsz, SUBLANES, lo), lambda i: (0, jnp.maximum(i * per - 1, 0), 0)),
                  full(mul), full(w0), full(w2h), full(w2l), full(a0), full(a2h), full(a2l)],
        out_specs=[out_spec] * 2,
        out_shape=[jax.ShapeDtypeStruct((t * SCAN_PITCH, LANES), F32)] * 2,
        scratch_shapes=[scratch, scratch],
        compiler_params=_cparams("parallel"),
        name=name,
    )(zl, zl, mul, w0, w2h, w2l, a0, a2h, a2l)
    return [o.reshape(1, t, SCAN_PITCH, LANES) for o in outs]


def _to_scan_bh(x3, *, nh, name, col_block=0, mu=None):
    bsz, t, _ = x3.shape
    d = nh * HEAD_SIZE
    assert bsz * nh == LANES and t % RELAYOUT_ROWS == 0
    in_specs = [pl.BlockSpec((1, RELAYOUT_ROWS, d), functools.partial(lambda i, b: (b, i, col_block), b=b))
                for b in range(bsz)]
    args = [x3] * bsz
    if mu is not None:
        per = RELAYOUT_ROWS // SUBLANES
        in_specs += [pl.BlockSpec((1, SUBLANES, d),
                                  functools.partial(lambda i, b: (b, jnp.maximum(i * per - 1, 0), col_block), b=b))
                     for b in range(bsz)] + [pl.BlockSpec((1, d), lambda i: (0, col_block))]
        args += [x3] * bsz + [mu]
    out = pl.pallas_call(
        functools.partial(_to_scan_body, nh=nh, bsz=bsz, mix=mu is not None),
        grid=(t // RELAYOUT_ROWS,),
        in_specs=in_specs,
        out_specs=pl.BlockSpec((RELAYOUT_ROWS * SCAN_PITCH, LANES), lambda i: (i, 0)),
        out_shape=jax.ShapeDtypeStruct((t * SCAN_PITCH, LANES), F32),
        scratch_shapes=[pltpu.VMEM((bsz, nh * SCAN_PITCH, RELAYOUT_ROWS), F32)],
        compiler_params=_cparams("parallel"),
        name=name,
    )(*args)
    return out.reshape(1, t, SCAN_PITCH, LANES)


def _gated_from_scan_body(y_ref, zg_ref, prev_ref, mu_ref, g2h_ref, g2l_ref, o_ref, yt_ref, *, nh, bsz):
    n, tq = HEAD_SIZE, RELAYOUT_ROWS
    for j in range(n):
        mt = y_ref[pl.ds(j, tq, stride=n), :].T
        for b in range(bsz):
            yt_ref[b, pl.ds(j, nh, stride=SCAN_PITCH), :] = mt[b * nh:(b + 1) * nh, :]
    zg = _token_shift(zg_ref, prev_ref, mu_ref, 0, zg_ref.shape[-1], True, time_axis=0)
    for b in range(bsz):
        yt = jnp.concatenate([yt_ref[b, h * SCAN_PITCH:h * SCAN_PITCH + n, :] for h in range(nh)], axis=0)
        g = _dot_f32(jax.nn.sigmoid(zg[b]), g2h_ref[...], g2l_ref[...])
        o_ref[b] = (yt.T * g).astype(o_ref.dtype)


def _gated_from_scan_bh(y, zl, mul, g2h, g2l, *, zg_block, nh, name):
    t, n, _ = y.shape
    bsz = zl.shape[0]
    d = nh * n
    lg = g2h.shape[0]
    assert bsz * nh == LANES and n == HEAD_SIZE and t % RELAYOUT_ROWS == 0
    per = RELAYOUT_ROWS // SUBLANES
    return pl.pallas_call(
        functools.partial(_gated_from_scan_body, nh=nh, bsz=bsz),
        grid=(t // RELAYOUT_ROWS,),
        in_specs=[pl.BlockSpec((RELAYOUT_ROWS * n, LANES), lambda i: (i, 0)),
                  pl.BlockSpec((bsz, RELAYOUT_ROWS, lg), lambda i: (0, i, zg_block)),
                  pl.BlockSpec((bsz, SUBLANES, lg), lambda i: (0, jnp.maximum(i * per - 1, 0), zg_block)),
                  pl.BlockSpec((1, lg), lambda i: (0, zg_block)),
                  pl.BlockSpec(g2h.shape, lambda i: (0, 0)), pl.BlockSpec(g2l.shape, lambda i: (0, 0))],
        out_specs=pl.BlockSpec((bsz, RELAYOUT_ROWS, d), lambda i: (0, i, 0)),
        out_shape=jax.ShapeDtypeStruct((bsz, t, d), BF16),
        scratch_shapes=[pltpu.VMEM((bsz, nh * SCAN_PITCH, RELAYOUT_ROWS), F32)],
        compiler_params=_cparams("parallel"),
        name=name,
    )(y.reshape(t * n, LANES), zl, zl, mul, g2h, g2l)


def _scan_body(r_ref, k_ref, v_ref, d_ref, a_ref, s0_ref, kk_ref, ka_ref, rk_ref, lw_ref, lb_ref,
               y_ref, s_ref, yrow_ref):
    n = HEAD_SIZE
    tc = r_ref.shape[0]

    @pl.when(pl.program_id(1) == 0)
    def _():
        s_ref[...] = s0_ref[...]

    def colsum(x):
        return jnp.sum(x, axis=0, keepdims=True)

    def step(t, gamma):
        r, k, v, w, a = (x[t, :n, :] for x in (r_ref, k_ref, v_ref, d_ref, a_ref))
        kk = k * kk_ref[...]
        kk = kk / jnp.maximum(jnp.sqrt(colsum(kk * kk)), KK_EPS)
        k2 = k * (1.0 + (a - 1.0) * ka_ref[...])
        kk_n = kk * gamma
        gamma = gamma * w
        inv = 1.0 / gamma
        b_n = kk * a * inv
        k_n = k2 * inv
        r_n = r * gamma
        for i in range(n):
            s_i = s_ref[i]
            s_kk = colsum(s_i * kk_n)
            s_i = s_i - s_kk * b_n + v_ref[t, i:i + 1, :] * k_n
            s_ref[i] = s_i
            yrow_ref[i:i + 1, :] = colsum(s_i * r_n)
        y = yrow_ref[...]
        ym = jnp.mean(y, axis=0, keepdims=True)
        yc = y - ym
        yv = jnp.mean(yc * yc, axis=0, keepdims=True)
        yn = yc * lax.rsqrt(yv + GN_EPS) * lw_ref[...] + lb_ref[...]
        bonus = colsum(r * k2 * rk_ref[...]) * v
        y_ref[t] = yn + bonus
        return gamma

    gamma = lax.fori_loop(0, tc, step, jnp.ones((n, LANES), F32))
    s_ref[...] = s_ref[...] * gamma


def _wkv_scan(r, k, v, d, a, s0, kkp, kap, rkp, lnw, lnb, *, tc, name):
    g, t, pitch, _ = r.shape
    n = HEAD_SIZE
    tc = min(tc, t)
    seq_in = pl.BlockSpec((None, tc, pitch, LANES), lambda gi, ci: (gi, ci, 0, 0))
    seq = pl.BlockSpec((None, tc, n, LANES), lambda gi, ci: (gi, ci, 0, 0))
    par = pl.BlockSpec((None, n, LANES), lambda gi, ci: (gi, 0, 0))
    st = pl.BlockSpec((None, n, n, LANES), lambda gi, ci: (gi, 0, 0, 0))
    return pl.pallas_call(
        _scan_body,
        grid=(g, t // tc),
        in_specs=[seq_in] * 5 + [st] + [par] * 5,
        out_specs=[seq, st],
        out_shape=[jax.ShapeDtypeStruct((g, t, n, LANES), F32),
                   jax.ShapeDtypeStruct((g, n, n, LANES), F32)],
        scratch_shapes=[pltpu.VMEM((n, LANES), F32)],
        compiler_params=_cparams("parallel", "arbitrary"),
        name=name,
    )(r, k, v, d, a, s0, kkp, kap, rkp, lnw, lnb)


def _gate_body(y_ref, zg_ref, prev_ref, mu_ref, g2h_ref, g2l_ref, o_ref, *, fresh):
    nb, tt, lg = zg_ref.shape
    zg = _token_shift(zg_ref, prev_ref, mu_ref, 0, lg, fresh).reshape(nb * tt, lg)
    g = _dot_f32(jax.nn.sigmoid(zg), g2h_ref[...], g2l_ref[...])
    o_ref[...] = (y_ref[...].reshape(nb * tt, -1) * g).reshape(o_ref.shape).astype(o_ref.dtype)


def _rwkv_gate(y3, zl, prevl, mul, g2h, g2l, *, zg_block, nb, tt, fresh, name):
    bsz, t, dr = y3.shape
    lg = g2h.shape[0]
    return pl.pallas_call(
        functools.partial(_gate_body, fresh=fresh),
        grid=(bsz // nb, t // tt),
        in_specs=[pl.BlockSpec((nb, tt, dr), lambda i, j: (i, j, 0))]
        + _shift_specs(zl, prevl, mul, nb=nb, tt=tt, width=lg, cb=zg_block, fresh=fresh)
        + [pl.BlockSpec(g2h.shape, lambda i, j: (0, 0)), pl.BlockSpec(g2l.shape, lambda i, j: (0, 0))],
        out_specs=pl.BlockSpec((nb, tt, dr), lambda i, j: (i, j, 0)),
        out_shape=jax.ShapeDtypeStruct((bsz, t, dr), BF16),
        compiler_params=_cparams("parallel", "parallel"),
        name=name,
    )(y3, zl, prevl, mul, g2h, g2l)


CONV_ROW_CHUNK = 32
CONV_LANE_CHUNK = 512


def _conv_body(u_ref, hist_ref, cw_ref, cb_ref, lw_ref, lb_ref, o_ref, full_ref, fs_ref, c_ref, *,
               hist_rows, taps, fresh):
    nb, tt, dc = u_ref.shape
    hin = hist_ref.shape[1]
    hist = hist_ref[...]
    if fresh:
        hist = jnp.where(pl.program_id(1) == 0, 0.0, hist)
    if hin < hist_rows:
        full_ref[:, :hist_rows - hin, :] = jnp.zeros((nb, hist_rows - hin, dc), F32)
    full_ref[:, hist_rows - hin:hist_rows, :] = hist
    full_ref[:, hist_rows:, :] = u_ref[...]
    lead = hist_rows - (taps - 1)
    span = hist_rows + tt - SUBLANES
    for s in range(1, SUBLANES):
        fs_ref[s - 1] = full_ref[:, s:s + span, :]
    rc = min(CONV_ROW_CHUNK, tt)
    lc = min(CONV_LANE_CHUNK, dc)

    def row_chunk(ri, carry, *, b, c0):
        r0 = pl.multiple_of(ri * rc, rc)
        acc = jnp.broadcast_to(cb_ref[:, c0:c0 + lc], (rc, lc))
        for j in range(taps):
            s = (lead + j) % SUBLANES
            rows = pl.ds(pl.multiple_of(r0 + (lead + j - s), SUBLANES), rc)
            win = full_ref[b, rows, c0:c0 + lc] if s == 0 else fs_ref[s - 1, b, rows, c0:c0 + lc]
            acc = acc + win * cw_ref[j:j + 1, c0:c0 + lc]
        c_ref[b, pl.ds(r0, rc), c0:c0 + lc] = acc
        return carry

    for b in range(nb):
        for c0 in range(0, dc, lc):
            lax.fori_loop(0, tt // rc, functools.partial(row_chunk, b=b, c0=c0), 0)
    acc = c_ref[...]
    mu = jnp.mean(acc, axis=-1, keepdims=True)
    xc = acc - mu
    var = jnp.mean(xc * xc, axis=-1, keepdims=True)
    y = xc * lax.rsqrt(var + LN_EPS) * lw_ref[...] + lb_ref[...]
    o_ref[...] = (y * jax.nn.sigmoid(y)).astype(o_ref.dtype)


def _conv_module(u3, conv0, cw, cb, lnw, lnb, *, nb, tt, name):
    bsz, t, dc = u3.shape
    taps = cw.shape[0]
    hist_rows = _round_up(taps - 1, SUBLANES)
    fresh = conv0 is None
    if fresh:
        assert nb == 1 and tt % hist_rows == 0
        hist, hin = u3, hist_rows
        hist_map = lambda i, j: (i, jnp.maximum(j * (tt // hist_rows) - 1, 0), 0)
    else:
        assert tt == t
        hist, hin = conv0, conv0.shape[1]
        hist_map = lambda i, j: (i, 0, 0)
    row = pl.BlockSpec((1, dc), lambda i, j: (0, 0))
    return pl.pallas_call(
        functools.partial(_conv_body, hist_rows=hist_rows, taps=taps, fresh=fresh),
        grid=(bsz // nb, t // tt),
        in_specs=[pl.BlockSpec((nb, tt, dc), lambda i, j: (i, j, 0)),
                  pl.BlockSpec((nb, hin, dc), hist_map),
                  pl.BlockSpec((taps, dc), lambda i, j: (0, 0)), row, row, row],
        out_specs=pl.BlockSpec((nb, tt, dc), lambda i, j: (i, j, 0)),
        out_shape=jax.ShapeDtypeStruct((bsz, t, dc), BF16),
        scratch_shapes=[pltpu.VMEM((nb, hist_rows + tt, dc), F32),
                        pltpu.VMEM((SUBLANES - 1, nb, hist_rows + tt - SUBLANES, dc), F32),
                        pltpu.VMEM((nb, tt, dc), F32)],
        compiler_params=_cparams("parallel", "parallel"),
        name=name,
    )(u3, hist, cw, cb.reshape(1, dc), lnw.reshape(1, dc), lnb.reshape(1, dc))


def _split_hi_lo(w, rows):
    w = jnp.pad(w.astype(F32), ((0, rows - w.shape[0]), (0, 0)))
    hi = w.astype(BF16)
    return hi, (w - hi.astype(F32)).astype(BF16)


def _layer(h, pe, wkv0, shift0, conv0, lanes_are_batch_head, p):
    bsz, t, d = h.shape
    m = bsz * t
    dr, dc, n = p["dr"], p["dc"], HEAD_SIZE
    nh = dr // n
    hbuf = conv0.shape[1] if conv0 is not None else p["conv_w"].shape[0] - 1
    x2 = h.reshape(m, d)
    fresh = shift0 is None

    xn = _rmsnorm(x2, p["g_mix"], out_dtype=BF16, name="mix_norm")
    in_proj = functools.partial(_matmul, w_transposed=True)
    zr = in_proj([xn], [p["w_in_t"]], n_out=3 * dr, **_tiles("in_proj_rkv")).reshape(bsz, t, 3 * dr)
    zl = in_proj([xn], [p["w_lora_t"]], **_tiles("in_proj_lora")).reshape(bsz, t, -1)
    c0 = p["conv_col0"]
    u = in_proj([xn], [p["w_in_t"], p["w_in_t"]], epi="glu", n_out=dc, w_row0=[c0, c0 + dc], **_tiles("in_proj_glu"))
    new_shift = _rmsnorm(h[:, -1, :], p["g_mix"], name="shift_norm")

    if fresh:
        tt, nb = 256, 1
        prevr, prevl = zr, zl
    else:
        tt, nb = t, 32
        s0b = shift0.astype(BF16)
        prevr = in_proj([s0b], [p["w_in_t"]], n_out=3 * dr, **_tiles("shift_proj")).reshape(bsz, 1, 3 * dr)
        prevl = in_proj([s0b], [p["w_lora_t"]], **_tiles("shift_lora")).reshape(bsz, 1, -1)
    prep = functools.partial(_rwkv_prep, zr, prevr, p["mu_rkv"], zl, prevl, p["mu_lora"], p["w0"], p["w2h"], p["w2l"],
                             p["a0"], p["a2h"], p["a2l"], nb=nb, tt=tt, fresh=fresh, name="rwkv_prep")

    if lanes_are_batch_head:
        assert fresh
        scan_in = [_to_scan_bh(zr, nh=nh, col_block=c, mu=p["mu_rkv"], name="mix_to_scan") for c in range(3)]
        scan_in += _prep_to_scan(zl, p["mu_lora"], p["w0"], p["w2h"], p["w2l"], p["a0"], p["a2h"], p["a2l"], nh=nh,
                                 name="prep_to_scan")
        par = lambda q: jnp.tile(q.reshape(nh, n).T, (1, bsz))[None]
        s0 = jnp.zeros((1, n, n, LANES), F32)
    else:
        to_scan = lambda x: jnp.transpose(x.reshape(bsz, t, nh, n), (2, 1, 3, 0))
        scan_in = [to_scan(x) for x in prep()]
        par = lambda q: jnp.broadcast_to(q.reshape(nh, n)[:, :, None], (nh, n, LANES))
        s0 = jnp.transpose(wkv0, (1, 2, 3, 0))
    y_s, s_new = _wkv_scan(*scan_in, s0, par(p["k_k"]), par(p["k_a"]), par(p["r_k"]), par(p["lnx_w"]),
                           par(p["lnx_b"]), tc=32, name="wkv_scan")
    if lanes_are_batch_head:
        y_rwkv = _gated_from_scan_bh(y_s[0], zl, p["mu_lora"], p["g2h"], p["g2l"], zg_block=p["zg_block"], nh=nh,
                                     name="gated_from_scan").reshape(m, dr)
        new_wkv = jnp.transpose(s_new.reshape(n, n, bsz, nh), (2, 3, 0, 1))
    else:
        y_tok = jnp.transpose(y_s, (3, 1, 0, 2)).reshape(bsz, t, dr)
        y_rwkv = _rwkv_gate(y_tok, zl, prevl, p["mu_lora"], p["g2h"], p["g2l"], zg_block=p["zg_block"], nb=nb, tt=tt,
                            fresh=fresh, name="rwkv_gate").reshape(m, dr)
        new_wkv = jnp.transpose(s_new, (3, 0, 1, 2))

    u3 = u.reshape(bsz, t, dc)
    if conv0 is None:
        new_conv = u3[:, t - hbuf:, :]
        cnb = 1
    else:
        new_conv = jnp.concatenate([conv0, u3], axis=1)[:, -hbuf:, :]
        cnb = 4
    y_conv = _conv_module(u3, conv0, p["conv_w"], p["conv_b"], p["conv_ln_w"], p["conv_ln_b"], nb=cnb, tt=tt,
                          name="conv_module").reshape(m, dc)

    assert dr == dc
    h1 = _matmul([y_rwkv, y_conv], [p["w_out"], p["w_out"]], w_blocks=[(0, 0), (1, 0)], res=x2, epi="res",
                 **_tiles("out_proj"))
    dff = p["w_down"].shape[0]
    tf = _pick_tile(dff, TILES["ffn_gate_up"][1])
    h1n = _rmsnorm(h1, p["g_ffn"], out_dtype=BF16, name="ffn_norm")
    act = _matmul([h1n], [p["w_gate_up"], p["w_gate_up"]], epi="swiglu", n_out=dff, out_dtype=BF16,
                  w_blocks=[(0, 0), (0, dff // tf)], **_tiles("ffn_gate_up"))
    h2 = _matmul([act], [p["w_down"]], res=h1, epi="res", **_tiles("ffn_down"))
    h2n = _rmsnorm(h2, p["g_ple"], out_dtype=BF16, name="ple_norm")
    h3 = _matmul([h2n, pe.reshape(m, -1)], [p["w_ple_gate"], p["w_ple_proj"]], res=h2, epi="ple", **_tiles("ple"))
    return h3.reshape(bsz, t, d), new_wkv, new_shift, new_conv


def _prep_params(i, g_mix, w_in, mu_shift, w0, w2, a0, a2, g2, k_k, k_a, r_k, lnx_w, lnx_b, conv_w, conv_b,
                 conv_ln_w, conv_ln_b, w_out, g_ffn, w_gate_up, w_down, g_ple, w_ple_gate, w_ple_proj):
    dr = w0.shape[-1]
    dc = conv_w.shape[-1]
    lw, la, lg = w2.shape[1], a2.shape[1], g2.shape[1]
    lwp, lap, lgp = (_round_up(x, LANES) for x in (lw, la, lg))

    def lora_groups(x):
        pad = lambda y, to: jnp.pad(y, [(0, to - y.shape[0])] + [(0, 0)] * (y.ndim - 1))
        return jnp.concatenate([pad(x[:lw], lwp), pad(x[lw:lw + la], lap), pad(x[lw + la:lw + la + lg], lgp)], axis=0)

    lora_n = lw + la + lg
    wt = jnp.swapaxes(w_in[i], 0, 1)
    w2h, w2l = _split_hi_lo(w2[i], lwp)
    a2h, a2l = _split_hi_lo(a2[i], lap)
    g2h, g2l = _split_hi_lo(g2[i], lgp)
    assert (lwp + lap) % lgp == 0
    return dict(
        dr=dr, dc=dc,
        g_mix=g_mix[i], w_in_t=wt,
        w_lora_t=lora_groups(lax.optimization_barrier(wt[3 * dr:3 * dr + lora_n])).astype(BF16),
        conv_col0=3 * dr + lora_n,
        mu_rkv=mu_shift[i][None, :3 * dr], mu_lora=lora_groups(mu_shift[i][3 * dr:])[None, :],
        zg_block=(lwp + lap) // lgp,
        w0=w0[i][None, :], w2h=w2h, w2l=w2l, a0=a0[i][None, :], a2h=a2h, a2l=a2l, g2h=g2h, g2l=g2l,
        k_k=k_k[i], k_a=k_a[i], r_k=r_k[i], lnx_w=lnx_w[i], lnx_b=lnx_b[i],
        conv_w=conv_w[i], conv_b=conv_b[i], conv_ln_w=conv_ln_w[i], conv_ln_b=conv_ln_b[i],
        w_out=w_out[i].astype(BF16), g_ffn=g_ffn[i], w_gate_up=w_gate_up[i], w_down=w_down[i].astype(BF16),
        g_ple=g_ple[i], w_ple_gate=w_ple_gate[i].astype(BF16), w_ple_proj=w_ple_proj[i].astype(BF16),
    )


def kernel(x_prompt, x_sample, state_wkv, state_shift, state_conv, p_prompt, p_sample, g_mix, w_in, mu_shift, w0, w2, a0, a2, g2, k_k, k_a, r_k, lnx_w, lnx_b, conv_w, conv_b, conv_ln_w, conv_ln_b, w_out, g_ffn, w_gate_up, w_down, g_ple, w_ple_gate, w_ple_proj, g_final):
    depth = w_in.shape[0]
    layer_params = (g_mix, w_in, mu_shift, w0, w2, a0, a2, g2, k_k, k_a, r_k, lnx_w, lnx_b, conv_w, conv_b,
                    conv_ln_w, conv_ln_b, w_out, g_ffn, w_gate_up, w_down, g_ple, w_ple_gate, w_ple_proj)
    hp, hs = x_prompt, x_sample
    outs = [[] for _ in range(6)]
    for i in range(depth):
        p = _prep_params(i, *layer_params)
        hp, s_w, s_sh, s_c = _layer(hp, p_prompt[i], None, None, None, True, p)
        outs[0].append(s_w); outs[1].append(s_sh); outs[2].append(s_c)
        hs, s_w, s_sh, s_c = _layer(hs, p_sample[i], state_wkv[i], state_shift[i], state_conv[i], False, p)
        outs[3].append(s_w); outs[4].append(s_sh); outs[5].append(s_c)
    d = hp.shape[-1]
    y_prompt = _rmsnorm(hp.reshape(-1, d), g_final, name="final_norm").reshape(hp.shape)
    y_sample = _rmsnorm(hs.reshape(-1, d), g_final, name="final_norm").reshape(hs.shape)
    return (y_prompt, y_sample) + tuple(jnp.stack(o) for o in outs)
```

```python
import functools

import jax
import jax.numpy as jnp
from jax import lax
from jax.experimental import pallas as pl
from jax.experimental.pallas import tpu as pltpu

HEAD_SIZE = 64
RMS_EPS = 1e-6
LN_EPS = 1e-5
GN_EPS = 64e-5
KK_EPS = 1e-12
LANES = 128
SUBLANES = 8
VMEM_LIMIT_BYTES = 56 * 2 ** 20

F32 = jnp.float32
BF16 = jnp.bfloat16


TILES = dict(in_proj_rkv=(2048, 512, 1), in_proj_lora=(1024, 512, 2), shift_proj=(128, 512, 2),
             shift_lora=(128, 512, 2), in_proj_glu=(2048, 256, 1), out_proj=(2048, 256, 1),
             ffn_gate_up=(2048, 256, 1), ffn_down=(512, 512, 2), ple=(2048, 256, 1))


def _tiles(name):
    tm, tn, x_buffers = TILES[name]
    return dict(tm=tm, tn=tn, x_buffers=x_buffers, name=name)


def _cparams(*sem):
    return pltpu.CompilerParams(dimension_semantics=sem, vmem_limit_bytes=VMEM_LIMIT_BYTES)


def _round_up(n, m):
    return (n + m - 1) // m * m


def _pick_tile(n, pref):
    if n <= pref:
        return n
    t = pref - pref % LANES
    while t > LANES and n % t:
        t -= LANES
    assert n % t == 0, (n, pref)
    return t


def _mm_body(*refs, n_x, n_w, epi, w_transposed):
    refs = list(refs)
    x_refs = [refs.pop(0) for _ in range(n_x)]
    w_refs = [refs.pop(0) for _ in range(n_w)]
    res_ref = refs.pop(0) if epi in ("res", "ple") else None
    out_ref = refs.pop(0)

    def dot(x_ref, w_ref):
        contract = (((1,), (1 if w_transposed else 0,)), ((), ()))
        return lax.dot_general(x_ref[...].astype(BF16), w_ref[...].astype(BF16), contract,
                               preferred_element_type=F32)

    if epi in ("store", "res"):
        acc = dot(x_refs[0], w_refs[0])
        for xr, wr in zip(x_refs[1:], w_refs[1:]):
            acc = acc + dot(xr, wr)
        if epi == "res":
            acc = res_ref[...] + acc
        out_ref[...] = acc.astype(out_ref.dtype)
    elif epi == "glu":
        out_ref[...] = (dot(x_refs[0], w_refs[0]) * jax.nn.sigmoid(dot(x_refs[0], w_refs[1]))).astype(out_ref.dtype)
    elif epi == "swiglu":
        gate = dot(x_refs[0], w_refs[0])
        out_ref[...] = (gate * jax.nn.sigmoid(gate) * dot(x_refs[0], w_refs[1])).astype(out_ref.dtype)
    elif epi == "ple":
        pg = jax.nn.sigmoid(dot(x_refs[0], w_refs[0]))
        out_ref[...] = (res_ref[...] + pg * dot(x_refs[1], w_refs[1])).astype(out_ref.dtype)
    else:
        raise ValueError(epi)


def _matmul(xs, ws, *, res=None, epi="store", out_dtype=F32, n_out=None, w_blocks=None, w_transposed=False,
            w_row0=None, tm, tn, x_buffers, name):
    m = xs[0].shape[0]
    n_out = ws[0].shape[0 if w_transposed else 1] if n_out is None else n_out
    tm = _pick_tile(m, tm)
    tn = _pick_tile(n_out, tn)
    w_blocks = [(0, 0)] * len(ws) if w_blocks is None else w_blocks
    in_specs = [pl.BlockSpec((tm, x.shape[1]), lambda i, j: (i, 0), pipeline_mode=pl.Buffered(x_buffers))
                for x in xs]
    for idx, (rb, cb) in enumerate(w_blocks):
        k = xs[min(idx, len(xs) - 1)].shape[1]
        if w_transposed and w_row0 is not None:
            assert rb == 0 and cb == 0 and w_row0[idx] % SUBLANES == 0
            in_specs.append(pl.BlockSpec((pl.Element(tn), pl.Element(k)),
                                         functools.partial(lambda i, j, r0: (pl.multiple_of(r0 + j * tn, SUBLANES), 0),
                                                           r0=w_row0[idx])))
        elif w_transposed:
            in_specs.append(pl.BlockSpec((tn, k), functools.partial(lambda i, j, rb, cb: (j + cb, rb), rb=rb, cb=cb)))
        else:
            in_specs.append(pl.BlockSpec((k, tn), functools.partial(lambda i, j, rb, cb: (rb, j + cb), rb=rb, cb=cb)))
    if res is not None:
        in_specs.append(pl.BlockSpec((tm, tn), lambda i, j: (i, j)))
    return pl.pallas_call(
        functools.partial(_mm_body, n_x=len(xs), n_w=len(ws), epi=epi, w_transposed=w_transposed),
        grid=(m // tm, n_out // tn),
        in_specs=in_specs,
        out_specs=pl.BlockSpec((tm, tn), lambda i, j: (i, j)),
        out_shape=jax.ShapeDtypeStruct((m, n_out), out_dtype),
        compiler_params=_cparams("parallel", "arbitrary"),
        name=name,
    )(*xs, *ws, *([] if res is None else [res]))


def _rms_body(x_ref, g_ref, o_ref):
    x = x_ref[...].astype(F32)
    ms = jnp.mean(x * x, axis=-1, keepdims=True)
    o_ref[...] = (x * lax.rsqrt(ms + RMS_EPS) * g_ref[...]).astype(o_ref.dtype)


def _rmsnorm(x, gain, *, out_dtype=F32, tm=512, name="rmsnorm"):
    m, d = x.shape
    tm = _pick_tile(m, tm)
    return pl.pallas_call(
        _rms_body,
        grid=(m // tm,),
        in_specs=[pl.BlockSpec((tm, d), lambda i: (i, 0)), pl.BlockSpec((1, d), lambda i: (0, 0))],
        out_specs=pl.BlockSpec((tm, d), lambda i: (i, 0)),
        out_shape=jax.ShapeDtypeStruct((m, d), out_dtype),
        compiler_params=_cparams("parallel"),
        name=name,
    )(x, gain.reshape(1, d).astype(F32))


def _split3(x):
    hi = x.astype(BF16)
    r1 = x - hi.astype(F32)
    mid = r1.astype(BF16)
    lo = (r1 - mid.astype(F32)).astype(BF16)
    return hi, mid, lo


def _dot_f32(a, b_hi, b_lo):
    a_hi, a_mid, _ = _split3(a)
    dot = lambda p, q: jnp.dot(p, q, preferred_element_type=F32)
    return dot(a_hi, b_hi) + (dot(a_mid, b_hi) + dot(a_hi, b_lo))


def _token_shift(z_ref, prev_ref, mu_ref, lo, width, fresh, time_axis=1):
    nb, tt, _ = z_ref.shape
    z = z_ref[:, :, lo:lo + width]
    first = prev_ref[:, prev_ref.shape[1] - 1:, lo:lo + width]
    if fresh:
        first = jnp.where(pl.program_id(time_axis) == 0, 0.0, first)
    tpos = lax.broadcasted_iota(jnp.int32, (nb, tt, 1), 1)
    zp = jnp.where(tpos == 0, first, pltpu.roll(z, 1, axis=1))
    return z + (zp - z) * mu_ref[:, lo:lo + width]


def _shift_specs(z, prev, mu, *, nb, tt, width, cb, fresh):
    if fresh:
        assert prev is z and tt % SUBLANES == 0
        prev_spec = pl.BlockSpec((nb, SUBLANES, width),
                                 lambda i, j: (i, jnp.maximum(j * (tt // SUBLANES) - 1, 0), cb))
    else:
        prev_spec = pl.BlockSpec((nb, 1, width), lambda i, j: (i, 0, cb))
    return [pl.BlockSpec((nb, tt, width), lambda i, j: (i, j, cb)), prev_spec,
            pl.BlockSpec((1, width), lambda i, j: (0, cb))]


def _prep_body(zr_ref, prevr_ref, mur_ref, zl_ref, prevl_ref, mul_ref, w0_ref, w2h_ref, w2l_ref, a0_ref, a2h_ref,
               a2l_ref, r_ref, k_ref, v_ref, d_ref, a_ref, *, dr, lw, la, fresh):
    nb, tt, _ = zl_ref.shape
    for c, o_ref in enumerate((r_ref, k_ref, v_ref)):
        o_ref[...] = _token_shift(zr_ref, prevr_ref, mur_ref, c * dr, dr, fresh)
    zw = _token_shift(zl_ref, prevl_ref, mul_ref, 0, lw, fresh).reshape(nb * tt, lw)
    za = _token_shift(zl_ref, prevl_ref, mul_ref, lw, la, fresh).reshape(nb * tt, la)
    dec, a = _decay_and_rate(zw, za, w0_ref, w2h_ref, w2l_ref, a0_ref, a2h_ref, a2l_ref)
    d_ref[...] = dec.reshape(nb, tt, dr)
    a_ref[...] = a.reshape(nb, tt, dr)


def _decay_and_rate(zw, za, w0_ref, w2h_ref, w2l_ref, a0_ref, a2h_ref, a2l_ref):
    xw = w0_ref[...] + _dot_f32(jnp.tanh(zw), w2h_ref[...], w2l_ref[...])
    w_log = jnp.minimum(xw, 0.0) - jnp.log1p(jnp.exp(-jnp.abs(xw))) - 0.5
    xa = a0_ref[...] + _dot_f32(za, a2h_ref[...], a2l_ref[...])
    return jnp.exp(-jnp.exp(w_log)), jax.nn.sigmoid(xa)


def _rwkv_prep(zr, prevr, mur, zl, prevl, mul, w0, w2h, w2l, a0, a2h, a2l, *, nb, tt, fresh, name):
    bsz, t, _ = zl.shape
    dr = w0.shape[-1]
    lw, la = w2h.shape[0], a2h.shape[0]
    row = lambda c: pl.BlockSpec((1, c), lambda i, j: (0, 0))
    full = lambda a: pl.BlockSpec(a.shape, lambda i, j: (0, 0))
    shift = functools.partial(_shift_specs, nb=nb, tt=tt, cb=0, fresh=fresh)
    return pl.pallas_call(
        functools.partial(_prep_body, dr=dr, lw=lw, la=la, fresh=fresh),
        grid=(bsz // nb, t // tt),
        in_specs=shift(zr, prevr, mur, width=zr.shape[-1]) + shift(zl, prevl, mul, width=zl.shape[-1])
        + [row(dr), full(w2h), full(w2l), row(dr), full(a2h), full(a2l)],
        out_specs=[pl.BlockSpec((nb, tt, dr), lambda i, j: (i, j, 0))] * 5,
        out_shape=[jax.ShapeDtypeStruct((bsz, t, dr), F32)] * 5,
        compiler_params=_cparams("parallel", "parallel"),
        name=name,
    )(zr, prevr, mur, zl, prevl, mul, w0, w2h, w2l, a0, a2h, a2l)


RELAYOUT_ROWS = 128
SCAN_PITCH = HEAD_SIZE + SUBLANES


def _to_scan_body(*refs, nh, bsz, mix):
    x_refs, refs = refs[:bsz], refs[bsz:]
    if mix:
        prev_refs, mu_ref, refs = refs[:bsz], refs[bsz], refs[bsz + 1:]
    o_ref, xt_ref = refs
    n, tq = HEAD_SIZE, RELAYOUT_ROWS
    for b in range(bsz):
        x = x_refs[b][0]
        if mix:
            first = jnp.where(pl.program_id(0) == 0, 0.0, prev_refs[b][0, SUBLANES - 1:, :])
            row = lax.broadcasted_iota(jnp.int32, (tq, 1), 0)
            xp = jnp.where(row == 0, first, pltpu.roll(x, 1, axis=0))
            x = x + (xp - x) * mu_ref[...]
        _stash_transposed(x, xt_ref, b, nh)
    _emit_scan_rows(xt_ref, o_ref, nh, bsz)


def _stash_transposed(x, xt_ref, b, nh):
    n = HEAD_SIZE
    xt = x.T
    for h in range(nh):
        xt_ref[b, h * SCAN_PITCH:h * SCAN_PITCH + n, :] = xt[h * n:(h + 1) * n, :]


def _emit_scan_rows(xt_ref, o_ref, nh, bsz):
    n, tq = HEAD_SIZE, RELAYOUT_ROWS
    for j in range(n):
        m = jnp.concatenate([xt_ref[b, pl.ds(j, nh, stride=SCAN_PITCH), :] for b in range(bsz)], axis=0)
        o_ref[pl.ds(j, tq, stride=SCAN_PITCH), :] = m.T
    for j in range(n, SCAN_PITCH):
        o_ref[pl.ds(j, tq, stride=SCAN_PITCH), :] = jnp.zeros((tq, LANES), F32)


def _prep_to_scan_body(zl_ref, prev_ref, mu_ref, w0_ref, w2h_ref, w2l_ref, a0_ref, a2h_ref, a2l_ref,
                       d_ref, a_ref, xtd_ref, xta_ref, *, nh, lw, la):
    bsz = zl_ref.shape[0]
    zw = _token_shift(zl_ref, prev_ref, mu_ref, 0, lw, True, time_axis=0)
    za = _token_shift(zl_ref, prev_ref, mu_ref, lw, la, True, time_axis=0)
    for b in range(bsz):
        dec, a = _decay_and_rate(zw[b], za[b], w0_ref, w2h_ref, w2l_ref, a0_ref, a2h_ref, a2l_ref)
        _stash_transposed(dec, xtd_ref, b, nh)
        _stash_transposed(a, xta_ref, b, nh)
    _emit_scan_rows(xtd_ref, d_ref, nh, bsz)
    _emit_scan_rows(xta_ref, a_ref, nh, bsz)


def _prep_to_scan(zl, mul, w0, w2h, w2l, a0, a2h, a2l, *, nh, name):
    bsz, t, lo = zl.shape
    lw, la = w2h.shape[0], a2h.shape[0]
    assert bsz * nh == LANES and t % RELAYOUT_ROWS == 0
    per = RELAYOUT_ROWS // SUBLANES
    full = lambda a: pl.BlockSpec(a.shape, lambda i: (0,) * a.ndim)
    out_spec = pl.BlockSpec((RELAYOUT_ROWS * SCAN_PITCH, LANES), lambda i: (i, 0))
    scratch = pltpu.VMEM((bsz, nh * SCAN_PITCH, RELAYOUT_ROWS), F32)
    outs = pl.pallas_call(
        functools.partial(_prep_to_scan_body, nh=nh, lw=lw, la=la),
        grid=(t // RELAYOUT_ROWS,),
        in_specs=[pl.BlockSpec((bsz, RELAYOUT_ROWS, lo), lambda i: (0, i, 0)),
                  pl.BlockSpec((bsz, SUBLANES, lo), lambda i: (0, jnp.maximum(i * per - 1, 0), 0)),
                  full(mul), full(w0), full(w2h), full(w2l), full(a0), full(a2h), full(a2l)],
        out_specs=[out_spec] * 2,
        out_shape=[jax.ShapeDtypeStruct((t * SCAN_PITCH, LANES), F32)] * 2,
        scratch_shapes=[scratch, scratch],
        compiler_params=_cparams("parallel"),
        name=name,
    )(zl, zl, mul, w0, w2h, w2l, a0, a2h, a2l)
    return [o.reshape(1, t, SCAN_PITCH, LANES) for o in outs]


def _to_scan_bh(x3, *, nh, name, col_block=0, mu=None):
    bsz, t, _ = x3.shape
    d = nh * HEAD_SIZE
    assert bsz * nh == LANES and t % RELAYOUT_ROWS == 0
    in_specs = [pl.BlockSpec((1, RELAYOUT_ROWS, d), functools.partial(lambda i, b: (b, i, col_block), b=b))
                for b in range(bsz)]
    args = [x3] * bsz
    if mu is not None:
        per = RELAYOUT_ROWS // SUBLANES
        in_specs += [pl.BlockSpec((1, SUBLANES, d),
                                  functools.partial(lambda i, b: (b, jnp.maximum(i * per - 1, 0), col_block), b=b))
                     for b in range(bsz)] + [pl.BlockSpec((1, d), lambda i: (0, col_block))]
        args += [x3] * bsz + [mu]
    out = pl.pallas_call(
        functools.partial(_to_scan_body, nh=nh, bsz=bsz, mix=mu is not None),
        grid=(t // RELAYOUT_ROWS,),
        in_specs=in_specs,
        out_specs=pl.BlockSpec((RELAYOUT_ROWS * SCAN_PITCH, LANES), lambda i: (i, 0)),
        out_shape=jax.ShapeDtypeStruct((t * SCAN_PITCH, LANES), F32),
        scratch_shapes=[pltpu.VMEM((bsz, nh * SCAN_PITCH, RELAYOUT_ROWS), F32)],
        compiler_params=_cparams("parallel"),
        name=name,
    )(*args)
    return out.reshape(1, t, SCAN_PITCH, LANES)


def _gated_from_scan_body(y_ref, zg_ref, prev_ref, mu_ref, g2h_ref, g2l_ref, o_ref, yt_ref, *, nh, bsz):
    n, tq = HEAD_SIZE, RELAYOUT_ROWS
    for j in range(n):
        mt = y_ref[pl.ds(j, tq, stride=n), :].T
        for b in range(bsz):
            yt_ref[b, pl.ds(j, nh, stride=SCAN_PITCH), :] = mt[b * nh:(b + 1) * nh, :]
    zg = _token_shift(zg_ref, prev_ref, mu_ref, 0, zg_ref.shape[-1], True, time_axis=0)
    for b in range(bsz):
        yt = jnp.concatenate([yt_ref[b, h * SCAN_PITCH:h * SCAN_PITCH + n, :] for h in range(nh)], axis=0)
        g = _dot_f32(jax.nn.sigmoid(zg[b]), g2h_ref[...], g2l_ref[...])
        o_ref[b] = (yt.T * g).astype(o_ref.dtype)


def _gated_from_scan_bh(y, zl, mul, g2h, g2l, *, zg_block, nh, name):
    t, n, _ = y.shape
    bsz = zl.shape[0]
    d = nh * n
    lg = g2h.shape[0]
    assert bsz * nh == LANES and n == HEAD_SIZE and t % RELAYOUT_ROWS == 0
    per = RELAYOUT_ROWS // SUBLANES
    return pl.pallas_call(
        functools.partial(_gated_from_scan_body, nh=nh, bsz=bsz),
        grid=(t // RELAYOUT_ROWS,),
        in_specs=[pl.BlockSpec((RELAYOUT_ROWS * n, LANES), lambda i: (i, 0)),
                  pl.BlockSpec((bsz, RELAYOUT_ROWS, lg), lambda i: (0, i, zg_block)),
                  pl.BlockSpec((bsz, SUBLANES, lg), lambda i: (0, jnp.maximum(i * per - 1, 0), zg_block)),
                  pl.BlockSpec((1, lg), lambda i: (0, zg_block)),
                  pl.BlockSpec(g2h.shape, lambda i: (0, 0)), pl.BlockSpec(g2l.shape, lambda i: (0, 0))],
        out_specs=pl.BlockSpec((bsz, RELAYOUT_ROWS, d), lambda i: (0, i, 0)),
        out_shape=jax.ShapeDtypeStruct((bsz, t, d), BF16),
        scratch_shapes=[pltpu.VMEM((bsz, nh * SCAN_PITCH, RELAYOUT_ROWS), F32)],
        compiler_params=_cparams("parallel"),
        name=name,
    )(y.reshape(t * n, LANES), zl, zl, mul, g2h, g2l)


def _scan_body(r_ref, k_ref, v_ref, d_ref, a_ref, s0_ref, kk_ref, ka_ref, rk_ref, lw_ref, lb_ref,
               y_ref, s_ref, yrow_ref):
    n = HEAD_SIZE
    tc = r_ref.shape[0]

    @pl.when(pl.program_id(1) == 0)
    def _():
        s_ref[...] = s0_ref[...]

    def colsum(x):
        return jnp.sum(x, axis=0, keepdims=True)

    def step(t, gamma):
        r, k, v, w, a = (x[t, :n, :] for x in (r_ref, k_ref, v_ref, d_ref, a_ref))
        kk = k * kk_ref[...]
        kk = kk / jnp.maximum(jnp.sqrt(colsum(kk * kk)), KK_EPS)
        k2 = k * (1.0 + (a - 1.0) * ka_ref[...])
        kk_n = kk * gamma
        gamma = gamma * w
        inv = 1.0 / gamma
        b_n = kk * a * inv
        k_n = k2 * inv
        r_n = r * gamma
        for i in range(n):
            s_i = s_ref[i]
            s_kk = colsum(s_i * kk_n)
            s_i = s_i - s_kk * b_n + v_ref[t, i:i + 1, :] * k_n
            s_ref[i] = s_i
            yrow_ref[i:i + 1, :] = colsum(s_i * r_n)
        y = yrow_ref[...]
        ym = jnp.mean(y, axis=0, keepdims=True)
        yc = y - ym
        yv = jnp.mean(yc * yc, axis=0, keepdims=True)
        yn = yc * lax.rsqrt(yv + GN_EPS) * lw_ref[...] + lb_ref[...]
        bonus = colsum(r * k2 * rk_ref[...]) * v
        y_ref[t] = yn + bonus
        return gamma

    gamma = lax.fori_loop(0, tc, step, jnp.ones((n, LANES), F32))
    s_ref[...] = s_ref[...] * gamma


def _wkv_scan(r, k, v, d, a, s0, kkp, kap, rkp, lnw, lnb, *, tc, name):
    g, t, pitch, _ = r.shape
    n = HEAD_SIZE
    tc = min(tc, t)
    seq_in = pl.BlockSpec((None, tc, pitch, LANES), lambda gi, ci: (gi, ci, 0, 0))
    seq = pl.BlockSpec((None, tc, n, LANES), lambda gi, ci: (gi, ci, 0, 0))
    par = pl.BlockSpec((None, n, LANES), lambda gi, ci: (gi, 0, 0))
    st = pl.BlockSpec((None, n, n, LANES), lambda gi, ci: (gi, 0, 0, 0))
    return pl.pallas_call(
        _scan_body,
        grid=(g, t // tc),
        in_specs=[seq_in] * 5 + [st] + [par] * 5,
        out_specs=[seq, st],
        out_shape=[jax.ShapeDtypeStruct((g, t, n, LANES), F32),
                   jax.ShapeDtypeStruct((g, n, n, LANES), F32)],
        scratch_shapes=[pltpu.VMEM((n, LANES), F32)],
        compiler_params=_cparams("parallel", "arbitrary"),
        name=name,
    )(r, k, v, d, a, s0, kkp, kap, rkp, lnw, lnb)


def _gate_body(y_ref, zg_ref, prev_ref, mu_ref, g2h_ref, g2l_ref, o_ref, *, fresh):
    nb, tt, lg = zg_ref.shape
    zg = _token_shift(zg_ref, prev_ref, mu_ref, 0, lg, fresh).reshape(nb * tt, lg)
    g = _dot_f32(jax.nn.sigmoid(zg), g2h_ref[...], g2l_ref[...])
    o_ref[...] = (y_ref[...].reshape(nb * tt, -1) * g).reshape(o_ref.shape).astype(o_ref.dtype)


def _rwkv_gate(y3, zl, prevl, mul, g2h, g2l, *, zg_block, nb, tt, fresh, name):
    bsz, t, dr = y3.shape
    lg = g2h.shape[0]
    return pl.pallas_call(
        functools.partial(_gate_body, fresh=fresh),
        grid=(bsz // nb, t // tt),
        in_specs=[pl.BlockSpec((nb, tt, dr), lambda i, j: (i, j, 0))]
        + _shift_specs(zl, prevl, mul, nb=nb, tt=tt, width=lg, cb=zg_block, fresh=fresh)
        + [pl.BlockSpec(g2h.shape, lambda i, j: (0, 0)), pl.BlockSpec(g2l.shape, lambda i, j: (0, 0))],
        out_specs=pl.BlockSpec((nb, tt, dr), lambda i, j: (i, j, 0)),
        out_shape=jax.ShapeDtypeStruct((bsz, t, dr), BF16),
        compiler_params=_cparams("parallel", "parallel"),
        name=name,
    )(y3, zl, prevl, mul, g2h, g2l)


CONV_ROW_CHUNK = 32
CONV_LANE_CHUNK = 512


def _conv_body(u_ref, hist_ref, cw_ref, cb_ref, lw_ref, lb_ref, o_ref, full_ref, fs_ref, c_ref, *,
               hist_rows, taps, fresh):
    nb, tt, dc = u_ref.shape
    hin = hist_ref.shape[1]
    hist = hist_ref[...]
    if fresh:
        hist = jnp.where(pl.program_id(1) == 0, 0.0, hist)
    if hin < hist_rows:
        full_ref[:, :hist_rows - hin, :] = jnp.zeros((nb, hist_rows - hin, dc), F32)
    full_ref[:, hist_rows - hin:hist_rows, :] = hist
    full_ref[:, hist_rows:, :] = u_ref[...]
    lead = hist_rows - (taps - 1)
    span = hist_rows + tt - SUBLANES
    for s in range(1, SUBLANES):
        fs_ref[s - 1] = full_ref[:, s:s + span, :]
    rc = min(CONV_ROW_CHUNK, tt)
    lc = min(CONV_LANE_CHUNK, dc)

    def row_chunk(ri, carry, *, b, c0):
        r0 = pl.multiple_of(ri * rc, rc)
        acc = jnp.broadcast_to(cb_ref[:, c0:c0 + lc], (rc, lc))
        for j in range(taps):
            s = (lead + j) % SUBLANES
            rows = pl.ds(pl.multiple_of(r0 + (lead + j - s), SUBLANES), rc)
            win = full_ref[b, rows, c0:c0 + lc] if s == 0 else fs_ref[s - 1, b, rows, c0:c0 + lc]
            acc = acc + win * cw_ref[j:j + 1, c0:c0 + lc]
        c_ref[b, pl.ds(r0, rc), c0:c0 + lc] = acc
        return carry

    for b in range(nb):
        for c0 in range(0, dc, lc):
            lax.fori_loop(0, tt // rc, functools.partial(row_chunk, b=b, c0=c0), 0)
    acc = c_ref[...]
    mu = jnp.mean(acc, axis=-1, keepdims=True)
    xc = acc - mu
    var = jnp.mean(xc * xc, axis=-1, keepdims=True)
    y = xc * lax.rsqrt(var + LN_EPS) * lw_ref[...] + lb_ref[...]
    o_ref[...] = (y * jax.nn.sigmoid(y)).astype(o_ref.dtype)


def _conv_module(u3, conv0, cw, cb, lnw, lnb, *, nb, tt, name):
    bsz, t, dc = u3.shape
    taps = cw.shape[0]
    hist_rows = _round_up(taps - 1, SUBLANES)
    fresh = conv0 is None
    if fresh:
        assert nb == 1 and tt % hist_rows == 0
        hist, hin = u3, hist_rows
        hist_map = lambda i, j: (i, jnp.maximum(j * (tt // hist_rows) - 1, 0), 0)
    else:
        assert tt == t
        hist, hin = conv0, conv0.shape[1]
        hist_map = lambda i, j: (i, 0, 0)
    row = pl.BlockSpec((1, dc), lambda i, j: (0, 0))
    return pl.pallas_call(
        functools.partial(_conv_body, hist_rows=hist_rows, taps=taps, fresh=fresh),
        grid=(bsz // nb, t // tt),
        in_specs=[pl.BlockSpec((nb, tt, dc), lambda i, j: (i, j, 0)),
                  pl.BlockSpec((nb, hin, dc), hist_map),
                  pl.BlockSpec((taps, dc), lambda i, j: (0, 0)), row, row, row],
        out_specs=pl.BlockSpec((nb, tt, dc), lambda i, j: (i, j, 0)),
        out_shape=jax.ShapeDtypeStruct((bsz, t, dc), BF16),
        scratch_shapes=[pltpu.VMEM((nb, hist_rows + tt, dc), F32),
                        pltpu.VMEM((SUBLANES - 1, nb, hist_rows + tt - SUBLANES, dc), F32),
                        pltpu.VMEM((nb, tt, dc), F32)],
        compiler_params=_cparams("parallel", "parallel"),
        name=name,
    )(u3, hist, cw, cb.reshape(1, dc), lnw.reshape(1, dc), lnb.reshape(1, dc))


def _split_hi_lo(w, rows):
    w = jnp.pad(w.astype(F32), ((0, rows - w.shape[0]), (0, 0)))
    hi = w.astype(BF16)
    return hi, (w - hi.astype(F32)).astype(BF16)


def _layer(h, pe, wkv0, shift0, conv0, lanes_are_batch_head, p):
    bsz, t, d = h.shape
    m = bsz * t
    dr, dc, n = p["dr"], p["dc"], HEAD_SIZE
    nh = dr // n
    hbuf = conv0.shape[1] if conv0 is not None else p["conv_w"].shape[0] - 1
    x2 = h.reshape(m, d)
    fresh = shift0 is None

    xn = _rmsnorm(x2, p["g_mix"], out_dtype=BF16, name="mix_norm")
    in_proj = functools.partial(_matmul, w_transposed=True)
    zr = in_proj([xn], [p["w_in_t"]], n_out=3 * dr, **_tiles("in_proj_rkv")).reshape(bsz, t, 3 * dr)
    zl = in_proj([xn], [p["w_lora_t"]], **_tiles("in_proj_lora")).reshape(bsz, t, -1)
    c0 = p["conv_col0"]
    u = in_proj([xn], [p["w_in_t"], p["w_in_t"]], epi="glu", n_out=dc, w_row0=[c0, c0 + dc], **_tiles("in_proj_glu"))
    new_shift = _rmsnorm(h[:, -1, :], p["g_mix"], name="shift_norm")

    if fresh:
        tt, nb = 256, 1
        prevr, prevl = zr, zl
    else:
        tt, nb = t, 32
        s0b = shift0.astype(BF16)
        prevr = in_proj([s0b], [p["w_in_t"]], n_out=3 * dr, **_tiles("shift_proj")).reshape(bsz, 1, 3 * dr)
        prevl = in_proj([s0b], [p["w_lora_t"]], **_tiles("shift_lora")).reshape(bsz, 1, -1)
    prep = functools.partial(_rwkv_prep, zr, prevr, p["mu_rkv"], zl, prevl, p["mu_lora"], p["w0"], p["w2h"], p["w2l"],
                             p["a0"], p["a2h"], p["a2l"], nb=nb, tt=tt, fresh=fresh, name="rwkv_prep")

    if lanes_are_batch_head:
        assert fresh
        scan_in = [_to_scan_bh(zr, nh=nh, col_block=c, mu=p["mu_rkv"], name="mix_to_scan") for c in range(3)]
        scan_in += _prep_to_scan(zl, p["mu_lora"], p["w0"], p["w2h"], p["w2l"], p["a0"], p["a2h"], p["a2l"], nh=nh,
                                 name="prep_to_scan")
        par = lambda q: jnp.tile(q.reshape(nh, n).T, (1, bsz))[None]
        s0 = jnp.zeros((1, n, n, LANES), F32)
    else:
        to_scan = lambda x: jnp.transpose(x.reshape(bsz, t, nh, n), (2, 1, 3, 0))
        scan_in = [to_scan(x) for x in prep()]
        par = lambda q: jnp.broadcast_to(q.reshape(nh, n)[:, :, None], (nh, n, LANES))
        s0 = jnp.transpose(wkv0, (1, 2, 3, 0))
    y_s, s_new = _wkv_scan(*scan_in, s0, par(p["k_k"]), par(p["k_a"]), par(p["r_k"]), par(p["lnx_w"]),
                           par(p["lnx_b"]), tc=32, name="wkv_scan")
    if lanes_are_batch_head:
        y_rwkv = _gated_from_scan_bh(y_s[0], zl, p["mu_lora"], p["g2h"], p["g2l"], zg_block=p["zg_block"], nh=nh,
                                     name="gated_from_scan").reshape(m, dr)
        new_wkv = jnp.transpose(s_new.reshape(n, n, bsz, nh), (2, 3, 0, 1))
    else:
        y_tok = jnp.transpose(y_s, (3, 1, 0, 2)).reshape(bsz, t, dr)
        y_rwkv = _rwkv_gate(y_tok, zl, prevl, p["mu_lora"], p["g2h"], p["g2l"], zg_block=p["zg_block"], nb=nb, tt=tt,
                            fresh=fresh, name="rwkv_gate").reshape(m, dr)
        new_wkv = jnp.transpose(s_new, (3, 0, 1, 2))

    u3 = u.reshape(bsz, t, dc)
    if conv0 is None:
        new_conv = u3[:, t - hbuf:, :]
        cnb = 1
    else:
        new_conv = jnp.concatenate([conv0, u3], axis=1)[:, -hbuf:, :]
        cnb = 4
    y_conv = _conv_module(u3, conv0, p["conv_w"], p["conv_b"], p["conv_ln_w"], p["conv_ln_b"], nb=cnb, tt=tt,
                          name="conv_module").reshape(m, dc)

    assert dr == dc
    h1 = _matmul([y_rwkv, y_conv], [p["w_out"], p["w_out"]], w_blocks=[(0, 0), (1, 0)], res=x2, epi="res",
                 **_tiles("out_proj"))
    dff = p["w_down"].shape[0]
    tf = _pick_tile(dff, TILES["ffn_gate_up"][1])
    h1n = _rmsnorm(h1, p["g_ffn"], out_dtype=BF16, name="ffn_norm")
    act = _matmul([h1n], [p["w_gate_up"], p["w_gate_up"]], epi="swiglu", n_out=dff, out_dtype=BF16,
                  w_blocks=[(0, 0), (0, dff // tf)], **_tiles("ffn_gate_up"))
    h2 = _matmul([act], [p["w_down"]], res=h1, epi="res", **_tiles("ffn_down"))
    h2n = _rmsnorm(h2, p["g_ple"], out_dtype=BF16, name="ple_norm")
    h3 = _matmul([h2n, pe.reshape(m, -1)], [p["w_ple_gate"], p["w_ple_proj"]], res=h2, epi="ple", **_tiles("ple"))
    return h3.reshape(bsz, t, d), new_wkv, new_shift, new_conv


def _prep_params(i, g_mix, w_in, mu_shift, w0, w2, a0, a2, g2, k_k, k_a, r_k, lnx_w, lnx_b, conv_w, conv_b,
                 conv_ln_w, conv_ln_b, w_out, g_ffn, w_gate_up, w_down, g_ple, w_ple_gate, w_ple_proj):
    dr = w0.shape[-1]
    dc = conv_w.shape[-1]
    lw, la, lg = w2.shape[1], a2.shape[1], g2.shape[1]
    lwp, lap, lgp = (_round_up(x, LANES) for x in (lw, la, lg))

    def lora_groups(x):
        pad = lambda y, to: jnp.pad(y, [(0, to - y.shape[0])] + [(0, 0)] * (y.ndim - 1))
        return jnp.concatenate([pad(x[:lw], lwp), pad(x[lw:lw + la], lap), pad(x[lw + la:lw + la + lg], lgp)], axis=0)

    lora_n = lw + la + lg
    wt = jnp.swapaxes(w_in[i], 0, 1)
    w2h, w2l = _split_hi_lo(w2[i], lwp)
    a2h, a2l = _split_hi_lo(a2[i], lap)
    g2h, g2l = _split_hi_lo(g2[i], lgp)
    assert (lwp + lap) % lgp == 0
    return dict(
        dr=dr, dc=dc,
        g_mix=g_mix[i], w_in_t=wt,
        w_lora_t=lora_groups(lax.optimization_barrier(wt[3 * dr:3 * dr + lora_n])).astype(BF16),
        conv_col0=3 * dr + lora_n,
        mu_rkv=mu_shift[i][None, :3 * dr], mu_lora=lora_groups(mu_shift[i][3 * dr:])[None, :],
        zg_block=(lwp + lap) // lgp,
        w0=w0[i][None, :], w2h=w2h, w2l=w2l, a0=a0[i][None, :], a2h=a2h, a2l=a2l, g2h=g2h, g2l=g2l,
        k_k=k_k[i], k_a=k_a[i], r_k=r_k[i], lnx_w=lnx_w[i], lnx_b=lnx_b[i],
        conv_w=conv_w[i], conv_b=conv_b[i], conv_ln_w=conv_ln_w[i], conv_ln_b=conv_ln_b[i],
        w_out=w_out[i], g_ffn=g_ffn[i], w_gate_up=w_gate_up[i], w_down=w_down[i].astype(BF16),
        g_ple=g_ple[i], w_ple_gate=w_ple_gate[i], w_ple_proj=w_ple_proj[i],
    )


def kernel(x_prompt, x_sample, state_wkv, state_shift, state_conv, p_prompt, p_sample, g_mix, w_in, mu_shift, w0, w2, a0, a2, g2, k_k, k_a, r_k, lnx_w, lnx_b, conv_w, conv_b, conv_ln_w, conv_ln_b, w_out, g_ffn, w_gate_up, w_down, g_ple, w_ple_gate, w_ple_proj, g_final):
    depth = w_in.shape[0]
    layer_params = (g_mix, w_in, mu_shift, w0, w2, a0, a2, g2, k_k, k_a, r_k, lnx_w, lnx_b, conv_w, conv_b,
                    conv_ln_w, conv_ln_b, w_out, g_ffn, w_gate_up, w_down, g_ple, w_ple_gate, w_ple_proj)
    hp, hs = x_prompt, x_sample
    outs = [[] for _ in range(6)]
    for i in range(depth):
        p = _prep_params(i, *layer_params)
        hp, s_w, s_sh, s_c = _layer(hp, p_prompt[i], None, None, None, True, p)
        outs[0].append(s_w); outs[1].append(s_sh); outs[2].append(s_c)
        hs, s_w, s_sh, s_c = _layer(hs, p_sample[i], state_wkv[i], state_shift[i], state_conv[i], False, p)
        outs[3].append(s_w); outs[4].append(s_sh); outs[5].append(s_c)
    d = hp.shape[-1]
    y_prompt = _rmsnorm(hp.reshape(-1, d), g_final, name="final_norm").reshape(hp.shape)
    y_sample = _rmsnorm(hs.reshape(-1, d), g_final, name="final_norm").reshape(hs.shape)
    return (y_prompt, y_sample) + tuple(jnp.stack(o) for o in outs)
```

```python
import functools

import jax
import jax.numpy as jnp
from jax import lax
from jax.experimental import pallas as pl
from jax.experimental.pallas import tpu as pltpu

HEAD_SIZE = 64
RMS_EPS = 1e-6
LN_EPS = 1e-5
GN_EPS = 64e-5
KK_EPS = 1e-12
LANES = 128
SUBLANES = 8
VMEM_LIMIT_BYTES = 56 * 2 ** 20

F32 = jnp.float32
BF16 = jnp.bfloat16


TILES = dict(in_proj_rkv=(2048, 512, 1), in_proj_lora=(1024, 512, 2), shift_proj=(128, 512, 2),
             shift_lora=(128, 512, 2), in_proj_glu=(2048, 256, 1), out_proj=(1024, 512, 2),
             ffn_gate_up=(2048, 256, 1), ffn_down=(512, 512, 2), ple=(1024, 512, 2))


def _tiles(name):
    tm, tn, x_buffers = TILES[name]
    return dict(tm=tm, tn=tn, x_buffers=x_buffers, name=name)


def _cparams(*sem):
    return pltpu.CompilerParams(dimension_semantics=sem, vmem_limit_bytes=VMEM_LIMIT_BYTES)


def _round_up(n, m):
    return (n + m - 1) // m * m


def _pick_tile(n, pref):
    if n <= pref:
        return n
    t = pref - pref % LANES
    while t > LANES and n % t:
        t -= LANES
    assert n % t == 0, (n, pref)
    return t


def _mm_body(*refs, n_x, n_w, epi, w_transposed):
    refs = list(refs)
    x_refs = [refs.pop(0) for _ in range(n_x)]
    w_refs = [refs.pop(0) for _ in range(n_w)]
    res_ref = refs.pop(0) if epi in ("res", "ple") else None
    out_ref = refs.pop(0)

    def dot(x_ref, w_ref):
        contract = (((1,), (1 if w_transposed else 0,)), ((), ()))
        return lax.dot_general(x_ref[...].astype(BF16), w_ref[...].astype(BF16), contract,
                               preferred_element_type=F32)

    if epi in ("store", "res"):
        acc = dot(x_refs[0], w_refs[0])
        for xr, wr in zip(x_refs[1:], w_refs[1:]):
            acc = acc + dot(xr, wr)
        if epi == "res":
            acc = res_ref[...] + acc
        out_ref[...] = acc.astype(out_ref.dtype)
    elif epi == "glu":
        out_ref[...] = (dot(x_refs[0], w_refs[0]) * jax.nn.sigmoid(dot(x_refs[0], w_refs[1]))).astype(out_ref.dtype)
    elif epi == "swiglu":
        gate = dot(x_refs[0], w_refs[0])
        out_ref[...] = (gate * jax.nn.sigmoid(gate) * dot(x_refs[0], w_refs[1])).astype(out_ref.dtype)
    elif epi == "ple":
        pg = jax.nn.sigmoid(dot(x_refs[0], w_refs[0]))
        out_ref[...] = (res_ref[...] + pg * dot(x_refs[1], w_refs[1])).astype(out_ref.dtype)
    else:
        raise ValueError(epi)


def _matmul(xs, ws, *, res=None, epi="store", out_dtype=F32, n_out=None, w_blocks=None, w_transposed=False,
            w_row0=None, tm, tn, x_buffers, name):
    m = xs[0].shape[0]
    n_out = ws[0].shape[0 if w_transposed else 1] if n_out is None else n_out
    tm = _pick_tile(m, tm)
    tn = _pick_tile(n_out, tn)
    w_blocks = [(0, 0)] * len(ws) if w_blocks is None else w_blocks
    in_specs = [pl.BlockSpec((tm, x.shape[1]), lambda i, j: (i, 0), pipeline_mode=pl.Buffered(x_buffers))
                for x in xs]
    for idx, (rb, cb) in enumerate(w_blocks):
        k = xs[min(idx, len(xs) - 1)].shape[1]
        if w_transposed and w_row0 is not None:
            assert rb == 0 and cb == 0 and w_row0[idx] % SUBLANES == 0
            in_specs.append(pl.BlockSpec((pl.Element(tn), pl.Element(k)),
                                         functools.partial(lambda i, j, r0: (pl.multiple_of(r0 + j * tn, SUBLANES), 0),
                                                           r0=w_row0[idx])))
        elif w_transposed:
            in_specs.append(pl.BlockSpec((tn, k), functools.partial(lambda i, j, rb, cb: (j + cb, rb), rb=rb, cb=cb)))
        else:
            in_specs.append(pl.BlockSpec((k, tn), functools.partial(lambda i, j, rb, cb: (rb, j + cb), rb=rb, cb=cb)))
    if res is not None:
        in_specs.append(pl.BlockSpec((tm, tn), lambda i, j: (i, j)))
    return pl.pallas_call(
        functools.partial(_mm_body, n_x=len(xs), n_w=len(ws), epi=epi, w_transposed=w_transposed),
        grid=(m // tm, n_out // tn),
        in_specs=in_specs,
        out_specs=pl.BlockSpec((tm, tn), lambda i, j: (i, j)),
        out_shape=jax.ShapeDtypeStruct((m, n_out), out_dtype),
        compiler_params=_cparams("parallel", "arbitrary"),
        name=name,
    )(*xs, *ws, *([] if res is None else [res]))


def _rms_body(x_ref, g_ref, o_ref):
    x = x_ref[...].astype(F32)
    ms = jnp.mean(x * x, axis=-1, keepdims=True)
    o_ref[...] = (x * lax.rsqrt(ms + RMS_EPS) * g_ref[...]).astype(o_ref.dtype)


def _rmsnorm(x, gain, *, out_dtype=F32, tm=512, name="rmsnorm"):
    m, d = x.shape
    tm = _pick_tile(m, tm)
    return pl.pallas_call(
        _rms_body,
        grid=(m // tm,),
        in_specs=[pl.BlockSpec((tm, d), lambda i: (i, 0)), pl.BlockSpec((1, d), lambda i: (0, 0))],
        out_specs=pl.BlockSpec((tm, d), lambda i: (i, 0)),
        out_shape=jax.ShapeDtypeStruct((m, d), out_dtype),
        compiler_params=_cparams("parallel"),
        name=name,
    )(x, gain.reshape(1, d).astype(F32))


def _split3(x):
    hi = x.astype(BF16)
    r1 = x - hi.astype(F32)
    mid = r1.astype(BF16)
    lo = (r1 - mid.astype(F32)).astype(BF16)
    return hi, mid, lo


def _dot_f32(a, b_hi, b_lo):
    a_hi, a_mid, _ = _split3(a)
    dot = lambda p, q: jnp.dot(p, q, preferred_element_type=F32)
    return dot(a_hi, b_hi) + (dot(a_mid, b_hi) + dot(a_hi, b_lo))


def _token_shift(z_ref, prev_ref, mu_ref, lo, width, fresh, time_axis=1):
    nb, tt, _ = z_ref.shape
    z = z_ref[:, :, lo:lo + width]
    first = prev_ref[:, prev_ref.shape[1] - 1:, lo:lo + width]
    if fresh:
        first = jnp.where(pl.program_id(time_axis) == 0, 0.0, first)
    tpos = lax.broadcasted_iota(jnp.int32, (nb, tt, 1), 1)
    zp = jnp.where(tpos == 0, first, pltpu.roll(z, 1, axis=1))
    return z + (zp - z) * mu_ref[:, lo:lo + width]


def _shift_specs(z, prev, mu, *, nb, tt, width, cb, fresh):
    if fresh:
        assert prev is z and tt % SUBLANES == 0
        prev_spec = pl.BlockSpec((nb, SUBLANES, width),
                                 lambda i, j: (i, jnp.maximum(j * (tt // SUBLANES) - 1, 0), cb))
    else:
        prev_spec = pl.BlockSpec((nb, 1, width), lambda i, j: (i, 0, cb))
    return [pl.BlockSpec((nb, tt, width), lambda i, j: (i, j, cb)), prev_spec,
            pl.BlockSpec((1, width), lambda i, j: (0, cb))]


def _prep_body(zr_ref, prevr_ref, mur_ref, zl_ref, prevl_ref, mul_ref, w0_ref, w2h_ref, w2l_ref, a0_ref, a2h_ref,
               a2l_ref, r_ref, k_ref, v_ref, d_ref, a_ref, *, dr, lw, la, fresh):
    nb, tt, _ = zl_ref.shape
    for c, o_ref in enumerate((r_ref, k_ref, v_ref)):
        o_ref[...] = _token_shift(zr_ref, prevr_ref, mur_ref, c * dr, dr, fresh)
    zw = _token_shift(zl_ref, prevl_ref, mul_ref, 0, lw, fresh).reshape(nb * tt, lw)
    za = _token_shift(zl_ref, prevl_ref, mul_ref, lw, la, fresh).reshape(nb * tt, la)
    dec, a = _decay_and_rate(zw, za, w0_ref, w2h_ref, w2l_ref, a0_ref, a2h_ref, a2l_ref)
    d_ref[...] = dec.reshape(nb, tt, dr)
    a_ref[...] = a.reshape(nb, tt, dr)


def _decay_and_rate(zw, za, w0_ref, w2h_ref, w2l_ref, a0_ref, a2h_ref, a2l_ref):
    xw = w0_ref[...] + _dot_f32(jnp.tanh(zw), w2h_ref[...], w2l_ref[...])
    w_log = jnp.minimum(xw, 0.0) - jnp.log1p(jnp.exp(-jnp.abs(xw))) - 0.5
    xa = a0_ref[...] + _dot_f32(za, a2h_ref[...], a2l_ref[...])
    return jnp.exp(-jnp.exp(w_log)), jax.nn.sigmoid(xa)


def _rwkv_prep(zr, prevr, mur, zl, prevl, mul, w0, w2h, w2l, a0, a2h, a2l, *, nb, tt, fresh, name):
    bsz, t, _ = zl.shape
    dr = w0.shape[-1]
    lw, la = w2h.shape[0], a2h.shape[0]
    row = lambda c: pl.BlockSpec((1, c), lambda i, j: (0, 0))
    full = lambda a: pl.BlockSpec(a.shape, lambda i, j: (0, 0))
    shift = functools.partial(_shift_specs, nb=nb, tt=tt, cb=0, fresh=fresh)
    return pl.pallas_call(
        functools.partial(_prep_body, dr=dr, lw=lw, la=la, fresh=fresh),
        grid=(bsz // nb, t // tt),
        in_specs=shift(zr, prevr, mur, width=zr.shape[-1]) + shift(zl, prevl, mul, width=zl.shape[-1])
        + [row(dr), full(w2h), full(w2l), row(dr), full(a2h), full(a2l)],
        out_specs=[pl.BlockSpec((nb, tt, dr), lambda i, j: (i, j, 0))] * 5,
        out_shape=[jax.ShapeDtypeStruct((bsz, t, dr), F32)] * 5,
        compiler_params=_cparams("parallel", "parallel"),
        name=name,
    )(zr, prevr, mur, zl, prevl, mul, w0, w2h, w2l, a0, a2h, a2l)


RELAYOUT_ROWS = 128
SCAN_PITCH = HEAD_SIZE + SUBLANES


def _to_scan_body(*refs, nh, bsz, mix):
    x_refs, refs = refs[:bsz], refs[bsz:]
    if mix:
        prev_refs, mu_ref, refs = refs[:bsz], refs[bsz], refs[bsz + 1:]
    o_ref, xt_ref = refs
    n, tq = HEAD_SIZE, RELAYOUT_ROWS
    for b in range(bsz):
        x = x_refs[b][0]
        if mix:
            first = jnp.where(pl.program_id(0) == 0, 0.0, prev_refs[b][0, SUBLANES - 1:, :])
            row = lax.broadcasted_iota(jnp.int32, (tq, 1), 0)
            xp = jnp.where(row == 0, first, pltpu.roll(x, 1, axis=0))
            x = x + (xp - x) * mu_ref[...]
        _stash_transposed(x, xt_ref, b, nh)
    _emit_scan_rows(xt_ref, o_ref, nh, bsz)


def _stash_transposed(x, xt_ref, b, nh):
    n = HEAD_SIZE
    xt = x.T
    for h in range(nh):
        xt_ref[b, h * SCAN_PITCH:h * SCAN_PITCH + n, :] = xt[h * n:(h + 1) * n, :]


def _emit_scan_rows(xt_ref, o_ref, nh, bsz):
    n, tq = HEAD_SIZE, RELAYOUT_ROWS
    for j in range(n):
        m = jnp.concatenate([xt_ref[b, pl.ds(j, nh, stride=SCAN_PITCH), :] for b in range(bsz)], axis=0)
        o_ref[pl.ds(j, tq, stride=SCAN_PITCH), :] = m.T
    for j in range(n, SCAN_PITCH):
        o_ref[pl.ds(j, tq, stride=SCAN_PITCH), :] = jnp.zeros((tq, LANES), F32)


def _prep_to_scan_body(zl_ref, prev_ref, mu_ref, w0_ref, w2h_ref, w2l_ref, a0_ref, a2h_ref, a2l_ref,
                       d_ref, a_ref, xtd_ref, xta_ref, *, nh, lw, la):
    bsz = zl_ref.shape[0]
    zw = _token_shift(zl_ref, prev_ref, mu_ref, 0, lw, True, time_axis=0)
    za = _token_shift(zl_ref, prev_ref, mu_ref, lw, la, True, time_axis=0)
    for b in range(bsz):
        dec, a = _decay_and_rate(zw[b], za[b], w0_ref, w2h_ref, w2l_ref, a0_ref, a2h_ref, a2l_ref)
        _stash_transposed(dec, xtd_ref, b, nh)
        _stash_transposed(a, xta_ref, b, nh)
    _emit_scan_rows(xtd_ref, d_ref, nh, bsz)
    _emit_scan_rows(xta_ref, a_ref, nh, bsz)


def _prep_to_scan(zl, mul, w0, w2h, w2l, a0, a2h, a2l, *, nh, name):
    bsz, t, lo = zl.shape
    lw, la = w2h.shape[0], a2h.shape[0]
    assert bsz * nh == LANES and t % RELAYOUT_ROWS == 0
    per = RELAYOUT_ROWS // SUBLANES
    full = lambda a: pl.BlockSpec(a.shape, lambda i: (0,) * a.ndim)
    out_spec = pl.BlockSpec((RELAYOUT_ROWS * SCAN_PITCH, LANES), lambda i: (i, 0))
    scratch = pltpu.VMEM((bsz, nh * SCAN_PITCH, RELAYOUT_ROWS), F32)
    outs = pl.pallas_call(
        functools.partial(_prep_to_scan_body, nh=nh, lw=lw, la=la),
        grid=(t // RELAYOUT_ROWS,),
        in_specs=[pl.BlockSpec((bsz, RELAYOUT_ROWS, lo), lambda i: (0, i, 0)),
                  pl.BlockSpec((bsz, SUBLANES, lo), lambda i: (0, jnp.maximum(i * per - 1, 0), 0)),
                  full(mul), full(w0), full(w2h), full(w2l), full(a0), full(a2h), full(a2l)],
        out_specs=[out_spec] * 2,
        out_shape=[jax.ShapeDtypeStruct((t * SCAN_PITCH, LANES), F32)] * 2,
        scratch_shapes=[scratch, scratch],
        compiler_params=_cparams("parallel"),
        name=name,
    )(zl, zl, mul, w0, w2h, w2l, a0, a2h, a2l)
    return [o.reshape(1, t, SCAN_PITCH, LANES) for o in outs]


def _to_scan_bh(x3, *, nh, name, col_block=0, mu=None):
    bsz, t, _ = x3.shape
    d = nh * HEAD_SIZE
    assert bsz * nh == LANES and t % RELAYOUT_ROWS == 0
    in_specs = [pl.BlockSpec((1, RELAYOUT_ROWS, d), functools.partial(lambda i, b: (b, i, col_block), b=b))
                for b in range(bsz)]
    args = [x3] * bsz
    if mu is not None:
        per = RELAYOUT_ROWS // SUBLANES
        in_specs += [pl.BlockSpec((1, SUBLANES, d),
                                  functools.partial(lambda i, b: (b, jnp.maximum(i * per - 1, 0), col_block), b=b))
                     for b in range(bsz)] + [pl.BlockSpec((1, d), lambda i: (0, col_block))]
        args += [x3] * bsz + [mu]
    out = pl.pallas_call(
        functools.partial(_to_scan_body, nh=nh, bsz=bsz, mix=mu is not None),
        grid=(t // RELAYOUT_ROWS,),
        in_specs=in_specs,
        out_specs=pl.BlockSpec((RELAYOUT_ROWS * SCAN_PITCH, LANES), lambda i: (i, 0)),
        out_shape=jax.ShapeDtypeStruct((t * SCAN_PITCH, LANES), F32),
        scratch_shapes=[pltpu.VMEM((bsz, nh * SCAN_PITCH, RELAYOUT_ROWS), F32)],
        compiler_params=_cparams("parallel"),
        name=name,
    )(*args)
    return out.reshape(1, t, SCAN_PITCH, LANES)


def _gated_from_scan_body(y_ref, zg_ref, prev_ref, mu_ref, g2h_ref, g2l_ref, o_ref, yt_ref, *, nh, bsz):
    n, tq = HEAD_SIZE, RELAYOUT_ROWS
    for j in range(n):
        mt = y_ref[pl.ds(j, tq, stride=n), :].T
        for b in range(bsz):
            yt_ref[b, pl.ds(j, nh, stride=SCAN_PITCH), :] = mt[b * nh:(b + 1) * nh, :]
    zg = _token_shift(zg_ref, prev_ref, mu_ref, 0, zg_ref.shape[-1], True, time_axis=0)
    for b in range(bsz):
        yt = jnp.concatenate([yt_ref[b, h * SCAN_PITCH:h * SCAN_PITCH + n, :] for h in range(nh)], axis=0)
        g = _dot_f32(jax.nn.sigmoid(zg[b]), g2h_ref[...], g2l_ref[...])
        o_ref[b] = (yt.T * g).astype(o_ref.dtype)


def _gated_from_scan_bh(y, zl, mul, g2h, g2l, *, zg_block, nh, name):
    t, n, _ = y.shape
    bsz = zl.shape[0]
    d = nh * n
    lg = g2h.shape[0]
    assert bsz * nh == LANES and n == HEAD_SIZE and t % RELAYOUT_ROWS == 0
    per = RELAYOUT_ROWS // SUBLANES
    return pl.pallas_call(
        functools.partial(_gated_from_scan_body, nh=nh, bsz=bsz),
        grid=(t // RELAYOUT_ROWS,),
        in_specs=[pl.BlockSpec((RELAYOUT_ROWS * n, LANES), lambda i: (i, 0)),
                  pl.BlockSpec((bsz, RELAYOUT_ROWS, lg), lambda i: (0, i, zg_block)),
                  pl.BlockSpec((bsz, SUBLANES, lg), lambda i: (0, jnp.maximum(i * per - 1, 0), zg_block)),
                  pl.BlockSpec((1, lg), lambda i: (0, zg_block)),
                  pl.BlockSpec(g2h.shape, lambda i: (0, 0)), pl.BlockSpec(g2l.shape, lambda i: (0, 0))],
        out_specs=pl.BlockSpec((bsz, RELAYOUT_ROWS, d), lambda i: (0, i, 0)),
        out_shape=jax.ShapeDtypeStruct((bsz, t, d), BF16),
        scratch_shapes=[pltpu.VMEM((bsz, nh * SCAN_PITCH, RELAYOUT_ROWS), F32)],
        compiler_params=_cparams("parallel"),
        name=name,
    )(y.reshape(t * n, LANES), zl, zl, mul, g2h, g2l)


def _partial_colsum(x):
    return x.reshape(-1, SUBLANES, LANES).sum(axis=0)


_FOLD_SLOT = (6, 2, 4, 0, 7, 3, 5, 1)


def _fold8(parts):
    assert SUBLANES == 8 and len(parts) == 8
    sub = lax.broadcasted_iota(jnp.int32, (SUBLANES, LANES), 0)

    def level(a, b, sh):
        a = a + pltpu.roll(a, sh, axis=0)
        b = b + pltpu.roll(b, SUBLANES - sh, axis=0)
        return jnp.where((sub & sh) != 0, a, b)

    def level4(a, b):
        a = a + pltpu.roll(a, 4, axis=0)
        b = b + pltpu.roll(b, 4, axis=0)
        return jnp.where((sub & 4) == 0, a, b)

    c = [level4(parts[2 * m], parts[2 * m + 1]) for m in range(4)]
    d = [level(c[0], c[1], 2), level(c[2], c[3], 2)]
    return level(d[0], d[1], 1)


def _scan_body(r_ref, k_ref, v_ref, d_ref, a_ref, s0_ref, kk_ref, ka_ref, rk_ref, lw_ref, lb_ref,
               y_ref, s_ref, yrow_ref):
    n = HEAD_SIZE
    tc = r_ref.shape[0]

    @pl.when(pl.program_id(1) == 0)
    def _():
        s_ref[...] = s0_ref[...]

    def colsum(x):
        return jnp.sum(x, axis=0, keepdims=True)

    def step(t, gamma):
        r, k, v, w, a = (x[t, :n, :] for x in (r_ref, k_ref, v_ref, d_ref, a_ref))
        kk = k * kk_ref[...]
        kk = kk / jnp.maximum(jnp.sqrt(colsum(kk * kk)), KK_EPS)
        k2 = k * (1.0 + (a - 1.0) * ka_ref[...])
        kk_n = kk * gamma
        gamma = gamma * w
        inv = 1.0 / gamma
        b_n = kk * a * inv
        k_n = k2 * inv
        r_n = r * gamma
        for i0 in range(0, n, SUBLANES):
            parts = [None] * SUBLANES
            for q in range(SUBLANES):
                i = i0 + q
                s_i = s_ref[i]
                s_kk = colsum(s_i * kk_n)
                s_i = s_i - s_kk * b_n + v_ref[t, i:i + 1, :] * k_n
                s_ref[i] = s_i
                parts[_FOLD_SLOT[q]] = _partial_colsum(s_i * r_n)
            yrow_ref[i0:i0 + SUBLANES, :] = _fold8(parts)
        y = yrow_ref[...]
        ym = jnp.mean(y, axis=0, keepdims=True)
        yc = y - ym
        yv = jnp.mean(yc * yc, axis=0, keepdims=True)
        yn = yc * lax.rsqrt(yv + GN_EPS) * lw_ref[...] + lb_ref[...]
        bonus = colsum(r * k2 * rk_ref[...]) * v
        y_ref[t] = yn + bonus
        return gamma

    gamma = lax.fori_loop(0, tc, step, jnp.ones((n, LANES), F32))
    s_ref[...] = s_ref[...] * gamma


def _wkv_scan(r, k, v, d, a, s0, kkp, kap, rkp, lnw, lnb, *, tc, name):
    g, t, pitch, _ = r.shape
    n = HEAD_SIZE
    tc = min(tc, t)
    seq_in = pl.BlockSpec((None, tc, pitch, LANES), lambda gi, ci: (gi, ci, 0, 0))
    seq = pl.BlockSpec((None, tc, n, LANES), lambda gi, ci: (gi, ci, 0, 0))
    par = pl.BlockSpec((None, n, LANES), lambda gi, ci: (gi, 0, 0))
    st = pl.BlockSpec((None, n, n, LANES), lambda gi, ci: (gi, 0, 0, 0))
    return pl.pallas_call(
        _scan_body,
        grid=(g, t // tc),
        in_specs=[seq_in] * 5 + [st] + [par] * 5,
        out_specs=[seq, st],
        out_shape=[jax.ShapeDtypeStruct((g, t, n, LANES), F32),
                   jax.ShapeDtypeStruct((g, n, n, LANES), F32)],
        scratch_shapes=[pltpu.VMEM((n, LANES), F32)],
        compiler_params=_cparams("parallel", "arbitrary"),
        name=name,
    )(r, k, v, d, a, s0, kkp, kap, rkp, lnw, lnb)


def _gate_body(y_ref, zg_ref, prev_ref, mu_ref, g2h_ref, g2l_ref, o_ref, *, fresh):
    nb, tt, lg = zg_ref.shape
    zg = _token_shift(zg_ref, prev_ref, mu_ref, 0, lg, fresh).reshape(nb * tt, lg)
    g = _dot_f32(jax.nn.sigmoid(zg), g2h_ref[...], g2l_ref[...])
    o_ref[...] = (y_ref[...].reshape(nb * tt, -1) * g).reshape(o_ref.shape).astype(o_ref.dtype)


def _rwkv_gate(y3, zl, prevl, mul, g2h, g2l, *, zg_block, nb, tt, fresh, name):
    bsz, t, dr = y3.shape
    lg = g2h.shape[0]
    return pl.pallas_call(
        functools.partial(_gate_body, fresh=fresh),
        grid=(bsz // nb, t // tt),
        in_specs=[pl.BlockSpec((nb, tt, dr), lambda i, j: (i, j, 0))]
        + _shift_specs(zl, prevl, mul, nb=nb, tt=tt, width=lg, cb=zg_block, fresh=fresh)
        + [pl.BlockSpec(g2h.shape, lambda i, j: (0, 0)), pl.BlockSpec(g2l.shape, lambda i, j: (0, 0))],
        out_specs=pl.BlockSpec((nb, tt, dr), lambda i, j: (i, j, 0)),
        out_shape=jax.ShapeDtypeStruct((bsz, t, dr), BF16),
        compiler_params=_cparams("parallel", "parallel"),
        name=name,
    )(y3, zl, prevl, mul, g2h, g2l)


CONV_ROW_CHUNK = 32
CONV_LANE_CHUNK = 512


def _conv_body(u_ref, hist_ref, cw_ref, cb_ref, lw_ref, lb_ref, o_ref, full_ref, fs_ref, c_ref, *,
               hist_rows, taps, fresh):
    nb, tt, dc = u_ref.shape
    hin = hist_ref.shape[1]
    hist = hist_ref[...]
    if fresh:
        hist = jnp.where(pl.program_id(1) == 0, 0.0, hist)
    if hin < hist_rows:
        full_ref[:, :hist_rows - hin, :] = jnp.zeros((nb, hist_rows - hin, dc), F32)
    full_ref[:, hist_rows - hin:hist_rows, :] = hist
    full_ref[:, hist_rows:, :] = u_ref[...]
    lead = hist_rows - (taps - 1)
    span = hist_rows + tt - SUBLANES
    for s in range(1, SUBLANES):
        fs_ref[s - 1] = full_ref[:, s:s + span, :]
    rc = min(CONV_ROW_CHUNK, tt)
    lc = min(CONV_LANE_CHUNK, dc)

    def row_chunk(ri, carry, *, b, c0):
        r0 = pl.multiple_of(ri * rc, rc)
        acc = jnp.broadcast_to(cb_ref[:, c0:c0 + lc], (rc, lc))
        for j in range(taps):
            s = (lead + j) % SUBLANES
            rows = pl.ds(pl.multiple_of(r0 + (lead + j - s), SUBLANES), rc)
            win = full_ref[b, rows, c0:c0 + lc] if s == 0 else fs_ref[s - 1, b, rows, c0:c0 + lc]
            acc = acc + win * cw_ref[j:j + 1, c0:c0 + lc]
        c_ref[b, pl.ds(r0, rc), c0:c0 + lc] = acc
        return carry

    for b in range(nb):
        for c0 in range(0, dc, lc):
            lax.fori_loop(0, tt // rc, functools.partial(row_chunk, b=b, c0=c0), 0)
    acc = c_ref[...]
    mu = jnp.mean(acc, axis=-1, keepdims=True)
    xc = acc - mu
    var = jnp.mean(xc * xc, axis=-1, keepdims=True)
    y = xc * lax.rsqrt(var + LN_EPS) * lw_ref[...] + lb_ref[...]
    o_ref[...] = (y * jax.nn.sigmoid(y)).astype(o_ref.dtype)


def _conv_module(u3, conv0, cw, cb, lnw, lnb, *, nb, tt, name):
    bsz, t, dc = u3.shape
    taps = cw.shape[0]
    hist_rows = _round_up(taps - 1, SUBLANES)
    fresh = conv0 is None
    if fresh:
        assert nb == 1 and tt % hist_rows == 0
        hist, hin = u3, hist_rows
        hist_map = lambda i, j: (i, jnp.maximum(j * (tt // hist_rows) - 1, 0), 0)
    else:
        assert tt == t
        hist, hin = conv0, conv0.shape[1]
        hist_map = lambda i, j: (i, 0, 0)
    row = pl.BlockSpec((1, dc), lambda i, j: (0, 0))
    return pl.pallas_call(
        functools.partial(_conv_body, hist_rows=hist_rows, taps=taps, fresh=fresh),
        grid=(bsz // nb, t // tt),
        in_specs=[pl.BlockSpec((nb, tt, dc), lambda i, j: (i, j, 0)),
                  pl.BlockSpec((nb, hin, dc), hist_map),
                  pl.BlockSpec((taps, dc), lambda i, j: (0, 0)), row, row, row],
        out_specs=pl.BlockSpec((nb, tt, dc), lambda i, j: (i, j, 0)),
        out_shape=jax.ShapeDtypeStruct((bsz, t, dc), BF16),
        scratch_shapes=[pltpu.VMEM((nb, hist_rows + tt, dc), F32),
                        pltpu.VMEM((SUBLANES - 1, nb, hist_rows + tt - SUBLANES, dc), F32),
                        pltpu.VMEM((nb, tt, dc), F32)],
        compiler_params=_cparams("parallel", "parallel"),
        name=name,
    )(u3, hist, cw, cb.reshape(1, dc), lnw.reshape(1, dc), lnb.reshape(1, dc))


def _split_hi_lo(w, rows):
    w = jnp.pad(w.astype(F32), ((0, rows - w.shape[0]), (0, 0)))
    hi = w.astype(BF16)
    return hi, (w - hi.astype(F32)).astype(BF16)


def _layer(h, pe, wkv0, shift0, conv0, lanes_are_batch_head, p):
    bsz, t, d = h.shape
    m = bsz * t
    dr, dc, n = p["dr"], p["dc"], HEAD_SIZE
    nh = dr // n
    hbuf = conv0.shape[1] if conv0 is not None else p["conv_w"].shape[0] - 1
    x2 = h.reshape(m, d)
    fresh = shift0 is None

    xn = _rmsnorm(x2, p["g_mix"], out_dtype=BF16, name="mix_norm")
    in_proj = functools.partial(_matmul, w_transposed=True)
    zr = in_proj([xn], [p["w_in_t"]], n_out=3 * dr, **_tiles("in_proj_rkv")).reshape(bsz, t, 3 * dr)
    zl = in_proj([xn], [p["w_lora_t"]], **_tiles("in_proj_lora")).reshape(bsz, t, -1)
    c0 = p["conv_col0"]
    u = in_proj([xn], [p["w_in_t"], p["w_in_t"]], epi="glu", n_out=dc, w_row0=[c0, c0 + dc], **_tiles("in_proj_glu"))
    new_shift = _rmsnorm(h[:, -1, :], p["g_mix"], name="shift_norm")

    if fresh:
        tt, nb = 256, 1
        prevr, prevl = zr, zl
    else:
        tt, nb = t, 32
        s0b = shift0.astype(BF16)
        prevr = in_proj([s0b], [p["w_in_t"]], n_out=3 * dr, **_tiles("shift_proj")).reshape(bsz, 1, 3 * dr)
        prevl = in_proj([s0b], [p["w_lora_t"]], **_tiles("shift_lora")).reshape(bsz, 1, -1)
    prep = functools.partial(_rwkv_prep, zr, prevr, p["mu_rkv"], zl, prevl, p["mu_lora"], p["w0"], p["w2h"], p["w2l"],
                             p["a0"], p["a2h"], p["a2l"], nb=nb, tt=tt, fresh=fresh, name="rwkv_prep")

    if lanes_are_batch_head:
        assert fresh
        scan_in = [_to_scan_bh(zr, nh=nh, col_block=c, mu=p["mu_rkv"], name="mix_to_scan") for c in range(3)]
        scan_in += _prep_to_scan(zl, p["mu_lora"], p["w0"], p["w2h"], p["w2l"], p["a0"], p["a2h"], p["a2l"], nh=nh,
                                 name="prep_to_scan")
        par = lambda q: jnp.tile(q.reshape(nh, n).T, (1, bsz))[None]
        s0 = jnp.zeros((1, n, n, LANES), F32)
    else:
        to_scan = lambda x: jnp.transpose(x.reshape(bsz, t, nh, n), (2, 1, 3, 0))
        scan_in = [to_scan(x) for x in prep()]
        par = lambda q: jnp.broadcast_to(q.reshape(nh, n)[:, :, None], (nh, n, LANES))
        s0 = jnp.transpose(wkv0, (1, 2, 3, 0))
    y_s, s_new = _wkv_scan(*scan_in, s0, par(p["k_k"]), par(p["k_a"]), par(p["r_k"]), par(p["lnx_w"]),
                           par(p["lnx_b"]), tc=32, name="wkv_scan")
    if lanes_are_batch_head:
        y_rwkv = _gated_from_scan_bh(y_s[0], zl, p["mu_lora"], p["g2h"], p["g2l"], zg_block=p["zg_block"], nh=nh,
                                     name="gated_from_scan").reshape(m, dr)
        new_wkv = jnp.transpose(s_new.reshape(n, n, bsz, nh), (2, 3, 0, 1))
    else:
        y_tok = jnp.transpose(y_s, (3, 1, 0, 2)).reshape(bsz, t, dr)
        y_rwkv = _rwkv_gate(y_tok, zl, prevl, p["mu_lora"], p["g2h"], p["g2l"], zg_block=p["zg_block"], nb=nb, tt=tt,
                            fresh=fresh, name="rwkv_gate").reshape(m, dr)
        new_wkv = jnp.transpose(s_new, (3, 0, 1, 2))

    u3 = u.reshape(bsz, t, dc)
    if conv0 is None:
        new_conv = u3[:, t - hbuf:, :]
        cnb = 1
    else:
        new_conv = jnp.concatenate([conv0, u3], axis=1)[:, -hbuf:, :]
        cnb = 4
    y_conv = _conv_module(u3, conv0, p["conv_w"], p["conv_b"], p["conv_ln_w"], p["conv_ln_b"], nb=cnb, tt=tt,
                          name="conv_module").reshape(m, dc)

    assert dr == dc
    h1 = _matmul([y_rwkv, y_conv], [p["w_out"], p["w_out"]], w_blocks=[(0, 0), (1, 0)], res=x2, epi="res",
                 **_tiles("out_proj"))
    dff = p["w_down"].shape[0]
    tf = _pick_tile(dff, TILES["ffn_gate_up"][1])
    h1n = _rmsnorm(h1, p["g_ffn"], out_dtype=BF16, name="ffn_norm")
    act = _matmul([h1n], [p["w_gate_up"], p["w_gate_up"]], epi="swiglu", n_out=dff, out_dtype=BF16,
                  w_blocks=[(0, 0), (0, dff // tf)], **_tiles("ffn_gate_up"))
    h2 = _matmul([act], [p["w_down"]], res=h1, epi="res", **_tiles("ffn_down"))
    h2n = _rmsnorm(h2, p["g_ple"], out_dtype=BF16, name="ple_norm")
    h3 = _matmul([h2n, pe.reshape(m, -1)], [p["w_ple_gate"], p["w_ple_proj"]], res=h2, epi="ple", **_tiles("ple"))
    return h3.reshape(bsz, t, d), new_wkv, new_shift, new_conv


def _prep_params(i, g_mix, w_in, mu_shift, w0, w2, a0, a2, g2, k_k, k_a, r_k, lnx_w, lnx_b, conv_w, conv_b,
                 conv_ln_w, conv_ln_b, w_out, g_ffn, w_gate_up, w_down, g_ple, w_ple_gate, w_ple_proj):
    dr = w0.shape[-1]
    dc = conv_w.shape[-1]
    lw, la, lg = w2.shape[1], a2.shape[1], g2.shape[1]
    lwp, lap, lgp = (_round_up(x, LANES) for x in (lw, la, lg))

    def lora_groups(x):
        pad = lambda y, to: jnp.pad(y, [(0, to - y.shape[0])] + [(0, 0)] * (y.ndim - 1))
        return jnp.concatenate([pad(x[:lw], lwp), pad(x[lw:lw + la], lap), pad(x[lw + la:lw + la + lg], lgp)], axis=0)

    lora_n = lw + la + lg
    wt = jnp.swapaxes(w_in[i], 0, 1)
    w2h, w2l = _split_hi_lo(w2[i], lwp)
    a2h, a2l = _split_hi_lo(a2[i], lap)
    g2h, g2l = _split_hi_lo(g2[i], lgp)
    assert (lwp + lap) % lgp == 0
    return dict(
        dr=dr, dc=dc,
        g_mix=g_mix[i], w_in_t=wt,
        w_lora_t=lora_groups(lax.optimization_barrier(wt[3 * dr:3 * dr + lora_n])).astype(BF16),
        conv_col0=3 * dr + lora_n,
        mu_rkv=mu_shift[i][None, :3 * dr], mu_lora=lora_groups(mu_shift[i][3 * dr:])[None, :],
        zg_block=(lwp + lap) // lgp,
        w0=w0[i][None, :], w2h=w2h, w2l=w2l, a0=a0[i][None, :], a2h=a2h, a2l=a2l, g2h=g2h, g2l=g2l,
        k_k=k_k[i], k_a=k_a[i], r_k=r_k[i], lnx_w=lnx_w[i], lnx_b=lnx_b[i],
        conv_w=conv_w[i], conv_b=conv_b[i], conv_ln_w=conv_ln_w[i], conv_ln_b=conv_ln_b[i],
        w_out=w_out[i].astype(BF16), g_ffn=g_ffn[i], w_gate_up=w_gate_up[i], w_down=w_down[i].astype(BF16),
        g_ple=g_ple[i], w_ple_gate=w_ple_gate[i].astype(BF16), w_ple_proj=w_ple_proj[i].astype(BF16),
    )


def kernel(x_prompt, x_sample, state_wkv, state_shift, state_conv, p_prompt, p_sample, g_mix, w_in, mu_shift, w0, w2, a0, a2, g2, k_k, k_a, r_k, lnx_w, lnx_b, conv_w, conv_b, conv_ln_w, conv_ln_b, w_out, g_ffn, w_gate_up, w_down, g_ple, w_ple_gate, w_ple_proj, g_final):
    depth = w_in.shape[0]
    layer_params = (g_mix, w_in, mu_shift, w0, w2, a0, a2, g2, k_k, k_a, r_k, lnx_w, lnx_b, conv_w, conv_b,
                    conv_ln_w, conv_ln_b, w_out, g_ffn, w_gate_up, w_down, g_ple, w_ple_gate, w_ple_proj)
    hp, hs = x_prompt, x_sample
    outs = [[] for _ in range(6)]
    for i in range(depth):
        p = _prep_params(i, *layer_params)
        hp, s_w, s_sh, s_c = _layer(hp, p_prompt[i], None, None, None, True, p)
        outs[0].append(s_w); outs[1].append(s_sh); outs[2].append(s_c)
        hs, s_w, s_sh, s_c = _layer(hs, p_sample[i], state_wkv[i], state_shift[i], state_conv[i], False, p)
        outs[3].append(s_w); outs[4].append(s_sh); outs[5].append(s_c)
    d = hp.shape[-1]
    y_prompt = _rmsnorm(hp.reshape(-1, d), g_final, name="final_norm").reshape(hp.shape)
    y_sample = _rmsnorm(hs.reshape(-1, d), g_final, name="final_norm").reshape(hs.shape)
    return (y_prompt, y_sample) + tuple(jnp.stack(o) for o in outs)
```
